```python
import jax
import jax.numpy as jnp
from jax import lax
import numpy as np

D_MODEL = 2048
BATCH = 8
SEQ = 4096
DEPTH = 4

N_MIXERS = 4
N_META = 16
EPS = 1e-6
D_FF = 256 * ((8 * D_MODEL // 3 + 255) // 256)

MLSTM_HEADS = 8
MLSTM_DV = D_MODEL // MLSTM_HEADS
MLSTM_DK = MLSTM_DV // 2
MLSTM_CHUNK = 64
MLSTM_QK_W = MLSTM_HEADS * MLSTM_DK
MLSTM_V_W = MLSTM_HEADS * MLSTM_DV
MLSTM_IN = 2 * MLSTM_QK_W + 2 * MLSTM_V_W + 2 * MLSTM_HEADS

POOL_WINDOWS = (2, 4, 8, 16)
POOL_GROUP = D_MODEL // len(POOL_WINDOWS)

GDN_DK = 128
GDN_DV = 128
GDN_QK_HEADS = D_MODEL // GDN_DK
GDN_V_HEADS = 2 * GDN_QK_HEADS
GDN_CONV = 4
GDN_CHUNK = 64
GDN_QK_W = GDN_QK_HEADS * GDN_DK
GDN_V_W = GDN_V_HEADS * GDN_DV
GDN_CONV_CH = 2 * GDN_QK_W + GDN_V_W
GDN_IN = GDN_CONV_CH + GDN_V_W + 2 * GDN_V_HEADS

SWA_DH = 64
SWA_HQ = D_MODEL // SWA_DH
SWA_GROUP = 8
SWA_HKV = SWA_HQ // SWA_GROUP
SWA_WINDOW = 128
SWA_BLOCK = SWA_WINDOW
SWA_Q_W = SWA_HQ * SWA_DH
SWA_KV_W = SWA_HKV * SWA_DH
SWA_IN = SWA_Q_W + 2 * SWA_KV_W
ROPE_THETA = 10000.0

kernel_name = 'hybrid_interleaved_mlstm_pool_gdn_swa'


def _n_layers_of(mixer):
    return len(range(mixer, DEPTH, N_MIXERS))


def _rms_normalize(x):
    xf = x.astype(jnp.float32)
    return xf * lax.rsqrt(jnp.mean(xf * xf, axis=-1, keepdims=True) + EPS)


def _rmsnorm(x, w):
    return (_rms_normalize(x) * w.astype(jnp.float32)).astype(x.dtype)


def _swiglu(x, w_gate, w_up, w_down):
    return (jax.nn.silu(x @ w_gate) * (x @ w_up)) @ w_down


def _to_chunks(t, c):
    t = t.reshape(t.shape[0], t.shape[1], t.shape[2] // c, c, *t.shape[3:])
    return jnp.moveaxis(t, 2, 0)


def _from_chunks(t):
    t = jnp.moveaxis(t, 0, 2)
    return t.reshape(t.shape[0], t.shape[1], -1, *t.shape[4:])


def _meta_then_chunks(step, state, seqs, chunk):
    meta = tuple(s[:, :, :N_META] for s in seqs)
    real = tuple(_to_chunks(s[:, :, N_META:], chunk) for s in seqs)
    state, out_meta = step(state, meta)
    _, out_real = lax.scan(step, state, real)
    return jnp.concatenate([out_meta, _from_chunks(out_real)], axis=2)


def _mlstm_chunk(state, xs):
    c_st, n_st, m_st = state
    q, k, v, li, lf = xs
    c = q.shape[2]
    causal = jnp.tril(jnp.ones((c, c), bool))
    b = jnp.cumsum(lf, axis=-1)
    log_w = jnp.where(causal, b[..., :, None] - b[..., None, :] + li[..., None, :], -jnp.inf)
    log_init = b + m_st[..., None]
    m_t = jnp.maximum(log_init, jnp.max(log_w, axis=-1))
    w = jnp.exp(log_w - m_t[..., None])
    w_init = jnp.exp(log_init - m_t)
    qk = jnp.einsum('bhtd,bhsd->bhts', q, k) * w
    num = w_init[..., None] * jnp.einsum('bhtd,bhde->bhte', q, c_st) + jnp.einsum('bhts,bhse->bhte', qk, v)
    den = w_init * jnp.einsum('bhtd,bhd->bht', q, n_st) + jnp.sum(qk, axis=-1)
    h = num / jnp.maximum(jnp.abs(den), jnp.exp(-m_t))[..., None]
    log_end_init = b[..., -1] + m_st
    log_end = b[..., -1:] - b + li
    m_new = jnp.maximum(log_end_init, jnp.max(log_end, axis=-1))
    a_init = jnp.exp(log_end_init - m_new)
    a = jnp.exp(log_end - m_new[..., None])
    c_new = a_init[..., None, None] * c_st + jnp.einsum('bhs,bhsd,bhse->bhde', a, k, v)
    n_new = a_init[..., None] * n_st + jnp.einsum('bhs,bhsd->bhd', a, k)
    return (c_new, n_new, m_new), h


def _mlstm(u, w_in, b_if, norm_w, w_out):
    bsz, L, _ = u.shape
    p = u @ w_in
    q, k, v, og, gates = jnp.split(p, [MLSTM_QK_W, 2 * MLSTM_QK_W, 2 * MLSTM_QK_W + MLSTM_V_W,
                                       2 * MLSTM_QK_W + 2 * MLSTM_V_W], axis=-1)
    heads = lambda t, d: t.reshape(bsz, L, MLSTM_HEADS, d).transpose(0, 2, 1, 3).astype(jnp.float32)
    q = heads(q, MLSTM_DK) * (MLSTM_DK ** -0.5)
    k = heads(k, MLSTM_DK)
    v = heads(v, MLSTM_DV)
    gates = (gates.astype(jnp.float32) + b_if.astype(jnp.float32)).transpose(0, 2, 1)
    li = gates[:, :MLSTM_HEADS]
    lf = jax.nn.log_sigmoid(gates[:, MLSTM_HEADS:])
    state0 = (jnp.zeros((bsz, MLSTM_HEADS, MLSTM_DK, MLSTM_DV), jnp.float32),
              jnp.zeros((bsz, MLSTM_HEADS, MLSTM_DK), jnp.float32),
              jnp.zeros((bsz, MLSTM_HEADS), jnp.float32))
    hh = _meta_then_chunks(_mlstm_chunk, state0, (q, k, v, li, lf), MLSTM_CHUNK)
    hh = _rms_normalize(hh.transpose(0, 2, 1, 3)).reshape(bsz, L, MLSTM_V_W)
    hh = hh * norm_w.astype(jnp.float32) * jax.nn.sigmoid(og.astype(jnp.float32))
    return hh.astype(u.dtype) @ w_out


def _pool_mixer(u, w_group, scale):
    bsz, L, _ = u.shape
    uf = u.astype(jnp.float32)
    cs = jnp.cumsum(uf, axis=1)
    count = jnp.arange(L) + 1
    outs = []
    for gi, win in enumerate(POOL_WINDOWS):
        lo, hi = gi * POOL_GROUP, (gi + 1) * POOL_GROUP
        c = cs[..., lo:hi]
        lag = jnp.pad(c[:, :L - win], ((0, 0), (win, 0), (0, 0)))
        mean = (c - lag) / jnp.minimum(count, win).astype(jnp.float32)[None, :, None]
        outs.append(mean - uf[..., lo:hi])
    pooled = jnp.stack(outs, axis=2).astype(u.dtype)
    y = jnp.einsum('blgc,gcd->blgd', pooled, w_group).reshape(bsz, L, D_MODEL)
    return y * scale


def _causal_depthwise_conv(x, w):
    return lax.conv_general_dilated(x, w[:, None, :], window_strides=(1,), padding=[(w.shape[0] - 1, 0)],
                                    dimension_numbers=('NWC', 'WIO', 'NWC'), feature_group_count=x.shape[-1])


def _l2norm(x):
    return x * lax.rsqrt(jnp.sum(x * x, axis=-1, keepdims=True) + EPS)


def _gdn_chunk(s_st, xs):
    q, k, v, g, beta = xs
    c = q.shape[2]
    causal = jnp.tril(jnp.ones((c, c), bool))
    strict = jnp.tril(jnp.ones((c, c), bool), -1)
    gc = jnp.cumsum(g, axis=-1)
    decay = jnp.exp(jnp.where(causal, gc[..., :, None] - gc[..., None, :], -jnp.inf))
    kb = k * beta[..., None]
    lower = jnp.where(strict, jnp.einsum('bhtd,bhsd->bhts', kb, k) * decay, 0.0)
    a_mat = lower + jnp.eye(c, dtype=lower.dtype)
    rhs = jnp.concatenate([v * beta[..., None], kb * jnp.exp(gc)[..., None]], axis=-1)
    sol = lax.linalg.triangular_solve(a_mat, rhs, left_side=True, lower=True, unit_diagonal=True)
    u_vec, w_vec = sol[..., :GDN_DV], sol[..., GDN_DV:]
    v_new = u_vec - jnp.einsum('bhtd,bhde->bhte', w_vec, s_st)
    attn = jnp.einsum('bhtd,bhsd->bhts', q, k) * decay
    o = jnp.einsum('bhtd,bhde->bhte', q * jnp.exp(gc)[..., None], s_st) + jnp.einsum('bhts,bhse->bhte', attn, v_new)
    g_last = gc[..., -1]
    s_new = jnp.exp(g_last)[..., None, None] * s_st + jnp.einsum(
        'bhsd,bhse->bhde', k * jnp.exp(g_last[..., None] - gc)[..., None], v_new)
    return s_new, o


def _gated_deltanet(u, w_in, conv_w, a_log, dt_bias, norm_w, w_out):
    bsz, L, _ = u.shape
    p = u @ w_in
    qkv, z, b_pre, a_pre = jnp.split(p, [GDN_CONV_CH, GDN_CONV_CH + GDN_V_W, GDN_CONV_CH + GDN_V_W + GDN_V_HEADS], axis=-1)
    qkv = jax.nn.silu(_causal_depthwise_conv(qkv, conv_w))
    q, k, v = jnp.split(qkv, [GDN_QK_W, 2 * GDN_QK_W], axis=-1)
    heads = lambda t, n, d: t.reshape(bsz, L, n, d).transpose(0, 2, 1, 3).astype(jnp.float32)
    rep = GDN_V_HEADS // GDN_QK_HEADS
    q = jnp.repeat(_l2norm(heads(q, GDN_QK_HEADS, GDN_DK)) * (GDN_DK ** -0.5), rep, axis=1)
    k = jnp.repeat(_l2norm(heads(k, GDN_QK_HEADS, GDN_DK)), rep, axis=1)
    v = heads(v, GDN_V_HEADS, GDN_DV)
    beta = jax.nn.sigmoid(b_pre.astype(jnp.float32)).transpose(0, 2, 1)
    g = (-jnp.exp(a_log.astype(jnp.float32))
         * jax.nn.softplus(a_pre.astype(jnp.float32) + dt_bias.astype(jnp.float32))).transpose(0, 2, 1)
    s0 = jnp.zeros((bsz, GDN_V_HEADS, GDN_DK, GDN_DV), jnp.float32)
    o = _meta_then_chunks(_gdn_chunk, s0, (q, k, v, g, beta), GDN_CHUNK)
    o = _rms_normalize(o.transpose(0, 2, 1, 3)) * norm_w.astype(jnp.float32)
    o = o * jax.nn.silu(z.astype(jnp.float32).reshape(bsz, L, GDN_V_HEADS, GDN_DV))
    return o.reshape(bsz, L, GDN_V_W).astype(u.dtype) @ w_out


def _rope_tables(L, d):
    inv = ROPE_THETA ** (-jnp.arange(0, d, 2, dtype=jnp.float32) / d)
    ang = jnp.arange(L, dtype=jnp.float32)[:, None] * inv[None, :]
    ang = jnp.concatenate([ang, ang], axis=-1)
    return jnp.cos(ang), jnp.sin(ang)


def _rope(x, cos, sin):
    shape = (1, x.shape[1]) + (1,) * (x.ndim - 3) + (x.shape[-1],)
    cos, sin = cos.reshape(shape), sin.reshape(shape)
    x1, x2 = jnp.split(x, 2, axis=-1)
    return x * cos + jnp.concatenate([-x2, x1], axis=-1) * sin


def _swa_sinks(u, w_qkv, b_qkv, sinks, w_out, b_out):
    bsz, L, _ = u.shape
    p = (u @ w_qkv + b_qkv).astype(jnp.float32)
    q, k, v = jnp.split(p, [SWA_Q_W, SWA_Q_W + SWA_KV_W], axis=-1)
    q = q.reshape(bsz, L, SWA_HKV, SWA_GROUP, SWA_DH)
    k = k.reshape(bsz, L, SWA_HKV, SWA_DH)
    v = v.reshape(bsz, L, SWA_HKV, SWA_DH)
    cos, sin = _rope_tables(L, SWA_DH)
    q, k = _rope(q, cos, sin), _rope(k, cos, sin)
    nb = -(-L // SWA_BLOCK)
    lp = nb * SWA_BLOCK
    pad_end = lambda t: jnp.pad(t, ((0, 0), (0, lp - L)) + ((0, 0),) * (t.ndim - 2))
    q, k, v = pad_end(q), pad_end(k), pad_end(v)
    qb = q.reshape(bsz, nb, SWA_BLOCK, SWA_HKV, SWA_GROUP, SWA_DH)

    def band(t):
        tp = jnp.pad(t, ((0, 0), (SWA_BLOCK, 0), (0, 0), (0, 0)))
        prev = tp[:, :lp].reshape(bsz, nb, SWA_BLOCK, SWA_HKV, SWA_DH)
        cur = tp[:, SWA_BLOCK:].reshape(bsz, nb, SWA_BLOCK, SWA_HKV, SWA_DH)
        return jnp.concatenate([prev, cur], axis=2)

    kb, vb = band(k), band(v)
    s = jnp.einsum('bnqhgd,bnkhd->bnhgqk', qb, kb) * (SWA_DH ** -0.5)
    blk = jnp.arange(nb)[:, None, None] * SWA_BLOCK
    qpos = blk + jnp.arange(SWA_BLOCK)[None, :, None]
    kpos = blk - SWA_BLOCK + jnp.arange(2 * SWA_BLOCK)[None, None, :]
    mask = (kpos <= qpos) & (qpos - kpos < SWA_WINDOW) & (kpos >= 0)
    s = jnp.where(mask[None, :, None, None], s, -jnp.inf)
    sink = jnp.broadcast_to(sinks.astype(jnp.float32).reshape(1, 1, SWA_HKV, SWA_GROUP, 1, 1), s.shape[:-1] + (1,))
    prob = jax.nn.softmax(jnp.concatenate([s, sink], axis=-1), axis=-1)[..., :-1]
    o = jnp.einsum('bnhgqk,bnkhd->bnqhgd', prob, vb).reshape(bsz, lp, SWA_Q_W)[:, :L]
    return o.astype(u.dtype) @ w_out + b_out


def _fwd_setup_inputs(seed: int = 0) -> dict:
    key = jax.random.key(seed)
    ks = iter(jax.random.split(key, 40))
    nrm = lambda shape, s=1.0: s * jax.random.normal(next(ks), shape, jnp.float32)
    dense = lambda shape: nrm(shape, shape[-2] ** -0.5)
    gain = lambda shape: 1.0 + nrm(shape, 0.02)
    na, nb, nc, nd = (_n_layers_of(m) for m in range(N_MIXERS))
    x = nrm((BATCH, SEQ, D_MODEL))
    meta_tokens = nrm((N_META, D_MODEL))
    norm_w = gain((DEPTH, 3, D_MODEL))
    ffn_w_gate = dense((DEPTH, 2, D_MODEL, D_FF))
    ffn_w_up = dense((DEPTH, 2, D_MODEL, D_FF))
    ffn_w_down = dense((DEPTH, 2, D_FF, D_MODEL))
    mlstm_w_in = dense((na, D_MODEL, MLSTM_IN))
    b_i = nrm((na, MLSTM_HEADS), 0.1)
    b_f = jnp.linspace(3.0, 6.0, MLSTM_HEADS, dtype=jnp.float32)[None, :] + nrm((na, MLSTM_HEADS), 0.1)
    mlstm_b_if = jnp.concatenate([b_i, b_f], axis=-1)
    mlstm_norm_w = gain((na, MLSTM_V_W))
    mlstm_w_out = dense((na, MLSTM_V_W, D_MODEL))
    pool_w = dense((nb, len(POOL_WINDOWS), POOL_GROUP, POOL_GROUP))
    pool_scale = gain((nb, D_MODEL))
    gdn_w_in = dense((nc, D_MODEL, GDN_IN))
    gdn_conv_w = nrm((nc, GDN_CONV, GDN_CONV_CH), GDN_CONV ** -0.5)
    gdn_a_log = jnp.log(jax.random.uniform(next(ks), (nc, GDN_V_HEADS), jnp.float32, 1.0, 16.0))
    dt = jnp.exp(jax.random.uniform(next(ks), (nc, GDN_V_HEADS), jnp.float32,
                                    float(np.log(1e-3)), float(np.log(1e-1))))
    gdn_dt_bias = dt + jnp.log(-jnp.expm1(-dt))
    gdn_norm_w = gain((nc, GDN_DV))
    gdn_w_out = dense((nc, GDN_V_W, D_MODEL))
    swa_w_qkv = dense((nd, D_MODEL, SWA_IN))
    swa_b_qkv = nrm((nd, SWA_IN), 0.02)
    swa_sinks = nrm((nd, SWA_HQ), 0.5)
    swa_w_out = dense((nd, SWA_Q_W, D_MODEL))
    swa_b_out = nrm((nd, D_MODEL), 0.02)
    final_norm_w = gain((D_MODEL,))
    return {'x': x, 'meta_tokens': meta_tokens, 'norm_w': norm_w,
            'ffn_w_gate': ffn_w_gate, 'ffn_w_up': ffn_w_up, 'ffn_w_down': ffn_w_down,
            'mlstm_w_in': mlstm_w_in, 'mlstm_b_if': mlstm_b_if, 'mlstm_norm_w': mlstm_norm_w, 'mlstm_w_out': mlstm_w_out,
            'pool_w': pool_w, 'pool_scale': pool_scale,
            'gdn_w_in': gdn_w_in, 'gdn_conv_w': gdn_conv_w, 'gdn_a_log': gdn_a_log, 'gdn_dt_bias': gdn_dt_bias,
            'gdn_norm_w': gdn_norm_w, 'gdn_w_out': gdn_w_out,
            'swa_w_qkv': swa_w_qkv, 'swa_b_qkv': swa_b_qkv, 'swa_sinks': swa_sinks, 'swa_w_out': swa_w_out,
            'swa_b_out': swa_b_out, 'final_norm_w': final_norm_w}


def _fwd_reference(x, meta_tokens, norm_w, ffn_w_gate, ffn_w_up, ffn_w_down,
              mlstm_w_in, mlstm_b_if, mlstm_norm_w, mlstm_w_out,
              pool_w, pool_scale,
              gdn_w_in, gdn_conv_w, gdn_a_log, gdn_dt_bias, gdn_norm_w, gdn_w_out,
              swa_w_qkv, swa_b_qkv, swa_sinks, swa_w_out, swa_b_out, final_norm_w):
    bsz = x.shape[0]
    meta = jnp.broadcast_to(meta_tokens.astype(x.dtype)[None], (bsz, N_META, D_MODEL))
    h = jnp.concatenate([meta, x], axis=1)
    for i in range(DEPTH):
        m, j = i % N_MIXERS, i // N_MIXERS
        h = h + 0.5 * _swiglu(_rmsnorm(h, norm_w[i, 0]), ffn_w_gate[i, 0], ffn_w_up[i, 0], ffn_w_down[i, 0])
        u = _rmsnorm(h, norm_w[i, 1])
        if m == 0:
            y = _mlstm(u, mlstm_w_in[j], mlstm_b_if[j], mlstm_norm_w[j], mlstm_w_out[j])
        elif m == 1:
            y = _pool_mixer(u, pool_w[j], pool_scale[j])
        elif m == 2:
            y = _gated_deltanet(u, gdn_w_in[j], gdn_conv_w[j], gdn_a_log[j], gdn_dt_bias[j], gdn_norm_w[j], gdn_w_out[j])
        else:
            y = _swa_sinks(u, swa_w_qkv[j], swa_b_qkv[j], swa_sinks[j], swa_w_out[j], swa_b_out[j])
        h = h + y
        h = h + 0.5 * _swiglu(_rmsnorm(h, norm_w[i, 2]), ffn_w_gate[i, 1], ffn_w_up[i, 1], ffn_w_down[i, 1])
    return _rmsnorm(h, final_norm_w)[:, N_META:]


import jax as _jax
import jax.numpy as _jnp

TWIN_FORMAT = 'train_step'
FWD_PARAMS = ['x', 'meta_tokens', 'norm_w', 'ffn_w_gate', 'ffn_w_up', 'ffn_w_down', 'mlstm_w_in', 'mlstm_b_if', 'mlstm_norm_w', 'mlstm_w_out', 'pool_w', 'pool_scale', 'gdn_w_in', 'gdn_conv_w', 'gdn_a_log', 'gdn_dt_bias', 'gdn_norm_w', 'gdn_w_out', 'swa_w_qkv', 'swa_b_qkv', 'swa_sinks', 'swa_w_out', 'swa_b_out', 'final_norm_w']
TWIN_WEIGHTS = ['meta_tokens', 'norm_w', 'ffn_w_gate', 'ffn_w_up', 'ffn_w_down', 'mlstm_w_in', 'mlstm_b_if', 'mlstm_norm_w', 'mlstm_w_out', 'pool_w', 'pool_scale', 'gdn_w_in', 'gdn_conv_w', 'gdn_a_log', 'gdn_dt_bias', 'gdn_norm_w', 'gdn_w_out', 'swa_w_qkv', 'swa_b_qkv', 'swa_sinks', 'swa_w_out', 'swa_b_out', 'final_norm_w']
TWIN_DIFF_INPUT = 'x'
TWIN_INPUTS = ['x', 'meta_tokens', 'norm_w', 'ffn_w_gate', 'ffn_w_up', 'ffn_w_down', 'mlstm_w_in', 'mlstm_b_if', 'mlstm_norm_w', 'mlstm_w_out', 'pool_w', 'pool_scale', 'gdn_w_in', 'gdn_conv_w', 'gdn_a_log', 'gdn_dt_bias', 'gdn_norm_w', 'gdn_w_out', 'swa_w_qkv', 'swa_b_qkv', 'swa_sinks', 'swa_w_out', 'swa_b_out', 'final_norm_w', 'loss_target', 'm_meta_tokens', 'm_norm_w', 'm_ffn_w_gate', 'm_ffn_w_up', 'm_ffn_w_down', 'm_mlstm_w_in', 'm_mlstm_b_if', 'm_mlstm_norm_w', 'm_mlstm_w_out', 'm_pool_w', 'm_pool_scale', 'm_gdn_w_in', 'm_gdn_conv_w', 'm_gdn_a_log', 'm_gdn_dt_bias', 'm_gdn_norm_w', 'm_gdn_w_out', 'm_swa_w_qkv', 'm_swa_b_qkv', 'm_swa_sinks', 'm_swa_w_out', 'm_swa_b_out', 'm_final_norm_w', 'v_meta_tokens', 'v_norm_w', 'v_ffn_w_gate', 'v_ffn_w_up', 'v_ffn_w_down', 'v_mlstm_w_in', 'v_mlstm_b_if', 'v_mlstm_norm_w', 'v_mlstm_w_out', 'v_pool_w', 'v_pool_scale', 'v_gdn_w_in', 'v_gdn_conv_w', 'v_gdn_a_log', 'v_gdn_dt_bias', 'v_gdn_norm_w', 'v_gdn_w_out', 'v_swa_w_qkv', 'v_swa_b_qkv', 'v_swa_sinks', 'v_swa_w_out', 'v_swa_b_out', 'v_final_norm_w']
TWIN_OUTPUTS = ['loss', 'grad_x', 'grad_meta_tokens', 'grad_norm_w', 'grad_ffn_w_gate', 'grad_ffn_w_up', 'grad_ffn_w_down', 'grad_mlstm_w_in', 'grad_mlstm_b_if', 'grad_mlstm_norm_w', 'grad_mlstm_w_out', 'grad_pool_w', 'grad_pool_scale', 'grad_gdn_w_in', 'grad_gdn_conv_w', 'grad_gdn_a_log', 'grad_gdn_dt_bias', 'grad_gdn_norm_w', 'grad_gdn_w_out', 'grad_swa_w_qkv', 'grad_swa_b_qkv', 'grad_swa_sinks', 'grad_swa_w_out', 'grad_swa_b_out', 'grad_final_norm_w', 'delta_meta_tokens', 'delta_norm_w', 'delta_ffn_w_gate', 'delta_ffn_w_up', 'delta_ffn_w_down', 'delta_mlstm_w_in', 'delta_mlstm_b_if', 'delta_mlstm_norm_w', 'delta_mlstm_w_out', 'delta_pool_w', 'delta_pool_scale', 'delta_gdn_w_in', 'delta_gdn_conv_w', 'delta_gdn_a_log', 'delta_gdn_dt_bias', 'delta_gdn_norm_w', 'delta_gdn_w_out', 'delta_swa_w_qkv', 'delta_swa_b_qkv', 'delta_swa_sinks', 'delta_swa_w_out', 'delta_swa_b_out', 'delta_final_norm_w', 'new_m_meta_tokens', 'new_m_norm_w', 'new_m_ffn_w_gate', 'new_m_ffn_w_up', 'new_m_ffn_w_down', 'new_m_mlstm_w_in', 'new_m_mlstm_b_if', 'new_m_mlstm_norm_w', 'new_m_mlstm_w_out', 'new_m_pool_w', 'new_m_pool_scale', 'new_m_gdn_w_in', 'new_m_gdn_conv_w', 'new_m_gdn_a_log', 'new_m_gdn_dt_bias', 'new_m_gdn_norm_w', 'new_m_gdn_w_out', 'new_m_swa_w_qkv', 'new_m_swa_b_qkv', 'new_m_swa_sinks', 'new_m_swa_w_out', 'new_m_swa_b_out', 'new_m_final_norm_w', 'new_v_meta_tokens', 'new_v_norm_w', 'new_v_ffn_w_gate', 'new_v_ffn_w_up', 'new_v_ffn_w_down', 'new_v_mlstm_w_in', 'new_v_mlstm_b_if', 'new_v_mlstm_norm_w', 'new_v_mlstm_w_out', 'new_v_pool_w', 'new_v_pool_scale', 'new_v_gdn_w_in', 'new_v_gdn_conv_w', 'new_v_gdn_a_log', 'new_v_gdn_dt_bias', 'new_v_gdn_norm_w', 'new_v_gdn_w_out', 'new_v_swa_w_qkv', 'new_v_swa_b_qkv', 'new_v_swa_sinks', 'new_v_swa_w_out', 'new_v_swa_b_out', 'new_v_final_norm_w']
TWIN_LEAF_KINDS = {'loss': 'loss', 'grad_x': 'grad_x', 'grad_meta_tokens': 'grad_w', 'grad_norm_w': 'grad_w', 'grad_ffn_w_gate': 'grad_w', 'grad_ffn_w_up': 'grad_w', 'grad_ffn_w_down': 'grad_w', 'grad_mlstm_w_in': 'grad_w', 'grad_mlstm_b_if': 'grad_w', 'grad_mlstm_norm_w': 'grad_w', 'grad_mlstm_w_out': 'grad_w', 'grad_pool_w': 'grad_w', 'grad_pool_scale': 'grad_w', 'grad_gdn_w_in': 'grad_w', 'grad_gdn_conv_w': 'grad_w', 'grad_gdn_a_log': 'grad_w', 'grad_gdn_dt_bias': 'grad_w', 'grad_gdn_norm_w': 'grad_w', 'grad_gdn_w_out': 'grad_w', 'grad_swa_w_qkv': 'grad_w', 'grad_swa_b_qkv': 'grad_w', 'grad_swa_sinks': 'grad_w', 'grad_swa_w_out': 'grad_w', 'grad_swa_b_out': 'grad_w', 'grad_final_norm_w': 'grad_w', 'delta_meta_tokens': 'delta_w', 'delta_norm_w': 'delta_w', 'delta_ffn_w_gate': 'delta_w', 'delta_ffn_w_up': 'delta_w', 'delta_ffn_w_down': 'delta_w', 'delta_mlstm_w_in': 'delta_w', 'delta_mlstm_b_if': 'delta_w', 'delta_mlstm_norm_w': 'delta_w', 'delta_mlstm_w_out': 'delta_w', 'delta_pool_w': 'delta_w', 'delta_pool_scale': 'delta_w', 'delta_gdn_w_in': 'delta_w', 'delta_gdn_conv_w': 'delta_w', 'delta_gdn_a_log': 'delta_w', 'delta_gdn_dt_bias': 'delta_w', 'delta_gdn_norm_w': 'delta_w', 'delta_gdn_w_out': 'delta_w', 'delta_swa_w_qkv': 'delta_w', 'delta_swa_b_qkv': 'delta_w', 'delta_swa_sinks': 'delta_w', 'delta_swa_w_out': 'delta_w', 'delta_swa_b_out': 'delta_w', 'delta_final_norm_w': 'delta_w', 'new_m_meta_tokens': 'new_m', 'new_m_norm_w': 'new_m', 'new_m_ffn_w_gate': 'new_m', 'new_m_ffn_w_up': 'new_m', 'new_m_ffn_w_down': 'new_m', 'new_m_mlstm_w_in': 'new_m', 'new_m_mlstm_b_if': 'new_m', 'new_m_mlstm_norm_w': 'new_m', 'new_m_mlstm_w_out': 'new_m', 'new_m_pool_w': 'new_m', 'new_m_pool_scale': 'new_m', 'new_m_gdn_w_in': 'new_m', 'new_m_gdn_conv_w': 'new_m', 'new_m_gdn_a_log': 'new_m', 'new_m_gdn_dt_bias': 'new_m', 'new_m_gdn_norm_w': 'new_m', 'new_m_gdn_w_out': 'new_m', 'new_m_swa_w_qkv': 'new_m', 'new_m_swa_b_qkv': 'new_m', 'new_m_swa_sinks': 'new_m', 'new_m_swa_w_out': 'new_m', 'new_m_swa_b_out': 'new_m', 'new_m_final_norm_w': 'new_m', 'new_v_meta_tokens': 'new_v', 'new_v_norm_w': 'new_v', 'new_v_ffn_w_gate': 'new_v', 'new_v_ffn_w_up': 'new_v', 'new_v_ffn_w_down': 'new_v', 'new_v_mlstm_w_in': 'new_v', 'new_v_mlstm_b_if': 'new_v', 'new_v_mlstm_norm_w': 'new_v', 'new_v_mlstm_w_out': 'new_v', 'new_v_pool_w': 'new_v', 'new_v_pool_scale': 'new_v', 'new_v_gdn_w_in': 'new_v', 'new_v_gdn_conv_w': 'new_v', 'new_v_gdn_a_log': 'new_v', 'new_v_gdn_dt_bias': 'new_v', 'new_v_gdn_norm_w': 'new_v', 'new_v_gdn_w_out': 'new_v', 'new_v_swa_w_qkv': 'new_v', 'new_v_swa_b_qkv': 'new_v', 'new_v_swa_sinks': 'new_v', 'new_v_swa_w_out': 'new_v', 'new_v_swa_b_out': 'new_v', 'new_v_final_norm_w': 'new_v'}


def _forward(args):
    return _fwd_reference(*[args[k] for k in FWD_PARAMS])


def _output_shape():
    def fwd():
        inp = _fwd_setup_inputs(0)
        return _fwd_reference(*[inp[k] for k in FWD_PARAMS])
    out = _jax.eval_shape(fwd)
    return out.shape, out.dtype

N_MICROBATCH = 1
ADAM_LR = 0.001
ADAM_B1 = 0.9
ADAM_B2 = 0.999
ADAM_EPS = 1e-08
ADAM_WD = 0.01
ADAM_STEP = 10
PER_EXAMPLE_BATCH_AXIS = {'x': 0, 'loss_target': 0}
SHARED_INPUTS = []
_WEIGHT_DTYPES = {'meta_tokens': _jnp.float32, 'norm_w': _jnp.float32, 'ffn_w_gate': _jnp.float32, 'ffn_w_up': _jnp.float32, 'ffn_w_down': _jnp.float32, 'mlstm_w_in': _jnp.float32, 'mlstm_b_if': _jnp.float32, 'mlstm_norm_w': _jnp.float32, 'mlstm_w_out': _jnp.float32, 'pool_w': _jnp.float32, 'pool_scale': _jnp.float32, 'gdn_w_in': _jnp.float32, 'gdn_conv_w': _jnp.float32, 'gdn_a_log': _jnp.float32, 'gdn_dt_bias': _jnp.float32, 'gdn_norm_w': _jnp.float32, 'gdn_w_out': _jnp.float32, 'swa_w_qkv': _jnp.float32, 'swa_b_qkv': _jnp.float32, 'swa_sinks': _jnp.float32, 'swa_w_out': _jnp.float32, 'swa_b_out': _jnp.float32, 'final_norm_w': _jnp.float32}
MOMENT_SCALE = {'meta_tokens': 7.386847e-03, 'norm_w': 4.914290e-02, 'ffn_w_gate': 1.511279e-02, 'ffn_w_up': 1.464451e-02, 'ffn_w_down': 2.427727e-02, 'mlstm_w_in': 6.313514e-02, 'mlstm_b_if': 3.223813e-01, 'mlstm_norm_w': 5.345519e-02, 'mlstm_w_out': 5.254458e-02, 'pool_w': 6.187510e-02, 'pool_scale': 1.351433e-01, 'gdn_w_in': 2.219964e-02, 'gdn_conv_w': 2.183103e-02, 'gdn_a_log': 9.087635e-02, 'gdn_dt_bias': 8.736623e-02, 'gdn_norm_w': 1.392504e-01, 'gdn_w_out': 3.245364e-02, 'swa_w_qkv': 1.361152e-02, 'swa_b_qkv': 5.698508e-02, 'swa_sinks': 7.834127e-04, 'swa_w_out': 9.311648e-03, 'swa_b_out': 5.679545e-02, 'final_norm_w': 1.603793e+01}


def _to_microbatches(a, axis):
    t = _jnp.moveaxis(a, axis, 0)
    t = t.reshape((N_MICROBATCH, t.shape[0] // N_MICROBATCH) + t.shape[1:])
    return _jnp.moveaxis(t, 1, axis + 1)


def setup_inputs(seed: int = 0) -> dict:
    inp = _fwd_setup_inputs(seed)
    key = _jax.random.fold_in(_jax.random.key(seed), 7919)
    shape, _ = _output_shape()
    out = dict(inp)
    out["loss_target"] = _jax.random.normal(_jax.random.fold_in(key, 0), shape, _jnp.float32)
    for i, name in enumerate(TWIN_WEIGHTS):
        w = inp[name].astype(_jnp.float32)
        if MOMENT_SCALE is None:
            s = _jnp.sqrt(_jnp.mean(_jnp.square(w)) + 1e-30)
        else:
            s = MOMENT_SCALE[name]
        km, kv = _jax.random.split(_jax.random.fold_in(key, i + 1))
        out[name] = w
        out["m_" + name] = s * _jax.random.normal(km, w.shape, _jnp.float32)
        out["v_" + name] = (s * s) * _jax.random.uniform(kv, w.shape, _jnp.float32, 0.5, 1.5)
    if N_MICROBATCH > 1:
        for name, axis in PER_EXAMPLE_BATCH_AXIS.items():
            out[name] = _to_microbatches(out[name], axis)
    return {'x': out['x'], 'meta_tokens': out['meta_tokens'], 'norm_w': out['norm_w'], 'ffn_w_gate': out['ffn_w_gate'], 'ffn_w_up': out['ffn_w_up'], 'ffn_w_down': out['ffn_w_down'], 'mlstm_w_in': out['mlstm_w_in'], 'mlstm_b_if': out['mlstm_b_if'], 'mlstm_norm_w': out['mlstm_norm_w'], 'mlstm_w_out': out['mlstm_w_out'], 'pool_w': out['pool_w'], 'pool_scale': out['pool_scale'], 'gdn_w_in': out['gdn_w_in'], 'gdn_conv_w': out['gdn_conv_w'], 'gdn_a_log': out['gdn_a_log'], 'gdn_dt_bias': out['gdn_dt_bias'], 'gdn_norm_w': out['gdn_norm_w'], 'gdn_w_out': out['gdn_w_out'], 'swa_w_qkv': out['swa_w_qkv'], 'swa_b_qkv': out['swa_b_qkv'], 'swa_sinks': out['swa_sinks'], 'swa_w_out': out['swa_w_out'], 'swa_b_out': out['swa_b_out'], 'final_norm_w': out['final_norm_w'], 'loss_target': out['loss_target'], 'm_meta_tokens': out['m_meta_tokens'], 'm_norm_w': out['m_norm_w'], 'm_ffn_w_gate': out['m_ffn_w_gate'], 'm_ffn_w_up': out['m_ffn_w_up'], 'm_ffn_w_down': out['m_ffn_w_down'], 'm_mlstm_w_in': out['m_mlstm_w_in'], 'm_mlstm_b_if': out['m_mlstm_b_if'], 'm_mlstm_norm_w': out['m_mlstm_norm_w'], 'm_mlstm_w_out': out['m_mlstm_w_out'], 'm_pool_w': out['m_pool_w'], 'm_pool_scale': out['m_pool_scale'], 'm_gdn_w_in': out['m_gdn_w_in'], 'm_gdn_conv_w': out['m_gdn_conv_w'], 'm_gdn_a_log': out['m_gdn_a_log'], 'm_gdn_dt_bias': out['m_gdn_dt_bias'], 'm_gdn_norm_w': out['m_gdn_norm_w'], 'm_gdn_w_out': out['m_gdn_w_out'], 'm_swa_w_qkv': out['m_swa_w_qkv'], 'm_swa_b_qkv': out['m_swa_b_qkv'], 'm_swa_sinks': out['m_swa_sinks'], 'm_swa_w_out': out['m_swa_w_out'], 'm_swa_b_out': out['m_swa_b_out'], 'm_final_norm_w': out['m_final_norm_w'], 'v_meta_tokens': out['v_meta_tokens'], 'v_norm_w': out['v_norm_w'], 'v_ffn_w_gate': out['v_ffn_w_gate'], 'v_ffn_w_up': out['v_ffn_w_up'], 'v_ffn_w_down': out['v_ffn_w_down'], 'v_mlstm_w_in': out['v_mlstm_w_in'], 'v_mlstm_b_if': out['v_mlstm_b_if'], 'v_mlstm_norm_w': out['v_mlstm_norm_w'], 'v_mlstm_w_out': out['v_mlstm_w_out'], 'v_pool_w': out['v_pool_w'], 'v_pool_scale': out['v_pool_scale'], 'v_gdn_w_in': out['v_gdn_w_in'], 'v_gdn_conv_w': out['v_gdn_conv_w'], 'v_gdn_a_log': out['v_gdn_a_log'], 'v_gdn_dt_bias': out['v_gdn_dt_bias'], 'v_gdn_norm_w': out['v_gdn_norm_w'], 'v_gdn_w_out': out['v_gdn_w_out'], 'v_swa_w_qkv': out['v_swa_w_qkv'], 'v_swa_b_qkv': out['v_swa_b_qkv'], 'v_swa_sinks': out['v_swa_sinks'], 'v_swa_w_out': out['v_swa_w_out'], 'v_swa_b_out': out['v_swa_b_out'], 'v_final_norm_w': out['v_final_norm_w']}


def _loss(weights, diff, rest, loss_target):
    with _jax.named_scope("forward"):
        args = {**rest, TWIN_DIFF_INPUT: diff, **{k: w.astype(_WEIGHT_DTYPES[k]) for k, w in weights.items()}}
        y = _forward(args)
    with _jax.named_scope("loss_head"):
        err = _jnp.square(y.astype(_jnp.float32) - loss_target)
        return 0.5 * _jnp.sum(_jnp.mean(err, axis=-1)) if err.ndim else 0.5 * err


def _adamw(w, g, m, v):
    m = ADAM_B1 * m + (1.0 - ADAM_B1) * g
    v = ADAM_B2 * v + (1.0 - ADAM_B2) * _jnp.square(g)
    m_hat = m / (1.0 - ADAM_B1 ** ADAM_STEP)
    v_hat = v / (1.0 - ADAM_B2 ** ADAM_STEP)
    delta = -ADAM_LR * (m_hat / (_jnp.sqrt(v_hat) + ADAM_EPS) + ADAM_WD * w)
    return delta, m, v


def reference(x, meta_tokens, norm_w, ffn_w_gate, ffn_w_up, ffn_w_down, mlstm_w_in, mlstm_b_if, mlstm_norm_w, mlstm_w_out, pool_w, pool_scale, gdn_w_in, gdn_conv_w, gdn_a_log, gdn_dt_bias, gdn_norm_w, gdn_w_out, swa_w_qkv, swa_b_qkv, swa_sinks, swa_w_out, swa_b_out, final_norm_w, loss_target, m_meta_tokens, m_norm_w, m_ffn_w_gate, m_ffn_w_up, m_ffn_w_down, m_mlstm_w_in, m_mlstm_b_if, m_mlstm_norm_w, m_mlstm_w_out, m_pool_w, m_pool_scale, m_gdn_w_in, m_gdn_conv_w, m_gdn_a_log, m_gdn_dt_bias, m_gdn_norm_w, m_gdn_w_out, m_swa_w_qkv, m_swa_b_qkv, m_swa_sinks, m_swa_w_out, m_swa_b_out, m_final_norm_w, v_meta_tokens, v_norm_w, v_ffn_w_gate, v_ffn_w_up, v_ffn_w_down, v_mlstm_w_in, v_mlstm_b_if, v_mlstm_norm_w, v_mlstm_w_out, v_pool_w, v_pool_scale, v_gdn_w_in, v_gdn_conv_w, v_gdn_a_log, v_gdn_dt_bias, v_gdn_norm_w, v_gdn_w_out, v_swa_w_qkv, v_swa_b_qkv, v_swa_sinks, v_swa_w_out, v_swa_b_out, v_final_norm_w):
    given = dict(x=x, meta_tokens=meta_tokens, norm_w=norm_w, ffn_w_gate=ffn_w_gate, ffn_w_up=ffn_w_up, ffn_w_down=ffn_w_down, mlstm_w_in=mlstm_w_in, mlstm_b_if=mlstm_b_if, mlstm_norm_w=mlstm_norm_w, mlstm_w_out=mlstm_w_out, pool_w=pool_w, pool_scale=pool_scale, gdn_w_in=gdn_w_in, gdn_conv_w=gdn_conv_w, gdn_a_log=gdn_a_log, gdn_dt_bias=gdn_dt_bias, gdn_norm_w=gdn_norm_w, gdn_w_out=gdn_w_out, swa_w_qkv=swa_w_qkv, swa_b_qkv=swa_b_qkv, swa_sinks=swa_sinks, swa_w_out=swa_w_out, swa_b_out=swa_b_out, final_norm_w=final_norm_w, loss_target=loss_target, m_meta_tokens=m_meta_tokens, m_norm_w=m_norm_w, m_ffn_w_gate=m_ffn_w_gate, m_ffn_w_up=m_ffn_w_up, m_ffn_w_down=m_ffn_w_down, m_mlstm_w_in=m_mlstm_w_in, m_mlstm_b_if=m_mlstm_b_if, m_mlstm_norm_w=m_mlstm_norm_w, m_mlstm_w_out=m_mlstm_w_out, m_pool_w=m_pool_w, m_pool_scale=m_pool_scale, m_gdn_w_in=m_gdn_w_in, m_gdn_conv_w=m_gdn_conv_w, m_gdn_a_log=m_gdn_a_log, m_gdn_dt_bias=m_gdn_dt_bias, m_gdn_norm_w=m_gdn_norm_w, m_gdn_w_out=m_gdn_w_out, m_swa_w_qkv=m_swa_w_qkv, m_swa_b_qkv=m_swa_b_qkv, m_swa_sinks=m_swa_sinks, m_swa_w_out=m_swa_w_out, m_swa_b_out=m_swa_b_out, m_final_norm_w=m_final_norm_w, v_meta_tokens=v_meta_tokens, v_norm_w=v_norm_w, v_ffn_w_gate=v_ffn_w_gate, v_ffn_w_up=v_ffn_w_up, v_ffn_w_down=v_ffn_w_down, v_mlstm_w_in=v_mlstm_w_in, v_mlstm_b_if=v_mlstm_b_if, v_mlstm_norm_w=v_mlstm_norm_w, v_mlstm_w_out=v_mlstm_w_out, v_pool_w=v_pool_w, v_pool_scale=v_pool_scale, v_gdn_w_in=v_gdn_w_in, v_gdn_conv_w=v_gdn_conv_w, v_gdn_a_log=v_gdn_a_log, v_gdn_dt_bias=v_gdn_dt_bias, v_gdn_norm_w=v_gdn_norm_w, v_gdn_w_out=v_gdn_w_out, v_swa_w_qkv=v_swa_w_qkv, v_swa_b_qkv=v_swa_b_qkv, v_swa_sinks=v_swa_sinks, v_swa_w_out=v_swa_w_out, v_swa_b_out=v_swa_b_out, v_final_norm_w=v_final_norm_w)
    weights = {n: given[n] for n in TWIN_WEIGHTS}
    shared = {n: given[n] for n in SHARED_INPUTS}
    per_example = {n: given[n] for n in ['x']}
    grad_fn = _jax.value_and_grad(_loss, argnums=(0, 1))

    def one_microbatch(ex, loss_target):
        ex = dict(ex)
        diff = ex.pop(TWIN_DIFF_INPUT)
        return grad_fn(weights, diff, {**shared, **ex}, loss_target)

    if N_MICROBATCH == 1:
        loss, (grad_w, grad_x) = one_microbatch(per_example, given["loss_target"])
    else:
        def body(carry, xs):
            loss_sum, grad_sum = carry
            l_k, (gw_k, gx_k) = one_microbatch(xs[0], xs[1])
            with _jax.named_scope("update"):
                return (loss_sum + l_k, _jax.tree.map(_jnp.add, grad_sum, gw_k)), gx_k

        init = (_jnp.zeros((), _jnp.float32), _jax.tree.map(_jnp.zeros_like, weights))
        (loss, grad_w), grad_x = _jax.lax.scan(body, init, (per_example, given["loss_target"]))
    with _jax.named_scope("update"):
        delta_w, new_m, new_v = {}, {}, {}
        for n in TWIN_WEIGHTS:
            delta_w[n], new_m[n], new_v[n] = _adamw(weights[n], grad_w[n], given["m_" + n], given["v_" + n])
    return (loss, grad_x, *[grad_w[n] for n in TWIN_WEIGHTS], *[delta_w[n] for n in TWIN_WEIGHTS],
            *[new_m[n] for n in TWIN_WEIGHTS], *[new_v[n] for n in TWIN_WEIGHTS])
```

```python
import functools
import math

import jax
import jax.numpy as jnp
import numpy as np
from jax import lax
from jax.experimental import pallas as pl
from jax.experimental.pallas import tpu as pltpu

F32 = jnp.float32
BF16 = jnp.bfloat16
MESH = pl.DeviceIdType.MESH

EPS = 1e-6
CHUNK = 64
ROW_BLOCK = 128
SWA_WINDOW = 128
SWA_GROUP = 8
POOL_WINDOWS = (2, 4, 8, 16)
GDN_CONV = 4
ROPE_THETA = 10000.0
NEG = -1e30
N_CHIPS = 4
N_DEV = 8
LANES = 128
WIDE_TILE = 896
VMEM_LIMIT = 56 * 1024 * 1024
VMEM_BUDGET = 36 * 1024 * 1024

ADAM_LR = 0.001
ADAM_B1 = 0.9
ADAM_B2 = 0.999
ADAM_EPS = 1e-08
ADAM_WD = 0.01
ADAM_STEP = 10

_NN = ((1,), (0,))
_NT = ((1,), (1,))
_TN = ((0,), (0,))


def _cp(dims=None):
    return pltpu.CompilerParams(dimension_semantics=dims, vmem_limit_bytes=VMEM_LIMIT)


def _round_up(n, m):
    return -(-n // m) * m


def _div(n, cands):
    for c in cands:
        if c <= n and n % c == 0:
            return c
    return n


def _pad_cols(n):
    return _round_up(n, WIDE_TILE) if n > 2048 else _round_up(n, LANES)


def _dg(a, b, dims, prec=None):
    return lax.dot_general(a, b, (dims, ((), ())), precision=prec, preferred_element_type=F32)


def _make_dots(cast):
    prec = None if cast is not None else lax.Precision.HIGHEST
    c = (lambda t: t.astype(cast)) if cast is not None else (lambda t: t)
    rnn = lambda a, b: _dg(c(a), c(b), _NN, prec)
    rnt = lambda a, b: _dg(c(a), c(b), _NT, prec)
    rtn = lambda a, b: _dg(c(a), c(b), _TN, prec)

    @jax.custom_vjp
    def nn(a, b):
        return rnn(a, b)
    nn.defvjp(lambda a, b: (rnn(a, b), (a, b)), lambda r, ct: (rnt(ct, r[1]), rtn(r[0], ct)))

    @jax.custom_vjp
    def nt(a, b):
        return rnt(a, b)
    nt.defvjp(lambda a, b: (rnt(a, b), (a, b)), lambda r, ct: (rnn(ct, r[1]), rtn(ct, r[0])))

    @jax.custom_vjp
    def tn(a, b):
        return rtn(a, b)
    tn.defvjp(lambda a, b: (rtn(a, b), (a, b)), lambda r, ct: (rnt(r[1], ct), rnn(r[0], ct)))
    return nn, nt, tn, rnn, rnt, rtn


_bnn, _bnt, _btn, _, _, _ = _make_dots(BF16)
_hnn, _hnt, _htn, _rhnn, _rhnt, _rhtn = _make_dots(None)


def _sigmoid(x):
    return 1.0 / (1.0 + jnp.exp(-x))


def _silu(x):
    return x * _sigmoid(x)


def _softplus(x):
    return jnp.maximum(x, 0.0) + jnp.log(1.0 + jnp.exp(-jnp.abs(x)))


def _log_sigmoid(x):
    return -_softplus(-x)


def _iota(shape, dim):
    return lax.broadcasted_iota(jnp.int32, shape, dim)


def _matmul(name, form, grid, a_ops, b_ops, acc_ids, n_acc, extras, outs, epilogue, tm, tn,
            alias_inputs=(), alias_map=None):
    na, nb, ne, nal, no = len(a_ops), len(b_ops), len(extras), len(alias_inputs), len(outs)
    nk = grid[2]
    dims = {"nn": _NN, "nt": _NT, "tn": _TN}[form]

    def body(*refs):
        a_refs = refs[:na]
        b_refs = refs[na:na + nb]
        e_refs = refs[na + nb:na + nb + ne]
        o_refs = refs[na + nb + ne + nal:na + nb + ne + nal + no]
        acc = refs[-1]
        k = pl.program_id(2)

        @pl.when(k == 0)
        def _():
            acc[...] = jnp.zeros_like(acc)

        for p in range(na):
            acc[acc_ids[p]] += _dg(a_refs[p][...].astype(BF16), b_refs[p][...].astype(BF16), dims)

        @pl.when(k == nk - 1)
        def _():
            res = epilogue([acc[i] for i in range(n_acc)], [e[...] for e in e_refs])
            for o, r in zip(o_refs, res):
                o[...] = r.astype(o.dtype)

    ops = list(a_ops) + list(b_ops) + list(extras)
    in_specs = [pl.BlockSpec(bs, im) for (_, bs, im) in ops] + [pl.BlockSpec(memory_space=pl.ANY)] * nal
    aliases = {}
    if alias_map:
        aliases = {len(ops) + i: o for i, o in alias_map.items()}
    res = pl.pallas_call(
        body, name=name, grid=grid,
        in_specs=in_specs,
        out_specs=[pl.BlockSpec(bs, im) for (_, bs, im) in outs],
        out_shape=[s for (s, _, _) in outs],
        scratch_shapes=[pltpu.VMEM((n_acc, tm, tn), F32)],
        input_output_aliases=aliases,
        compiler_params=_cp(("parallel", "parallel", "arbitrary")),
    )(*[o[0] for o in ops], *alias_inputs)
    return res


def _tiles(M, N, K, fixed_bytes_per_tm_tn, k_cands=(512, 384, 256, 128), n_cands=(2048, 1792, 1408, 1280, 1024, 896, 768, 640, 512, 384, 256, 128),
           m_cands=(1408, 1056, 704, 528, 384, 256, 128, 64, 32, 16, 8), a_bytes=2, b_bytes=2, n_pairs=1):
    tn = _div(N, n_cands)
    tk = _div(K, k_cands)
    for tm in m_cands:
        if tm > M or M % tm:
            continue
        est = tm * tn * fixed_bytes_per_tm_tn + n_pairs * 2 * (tm * tk * a_bytes + tk * tn * b_bytes)
        if est <= VMEM_BUDGET:
            return tm, tn, tk
    return _div(M, (8,)), tn, tk


def _mm(name, form, a, b, out_dtype, epilogue=None, extras=(), extra_kinds=(), n_out=1, out_dtypes=None):
    if form == "nn":
        (M, K), N = a.shape, b.shape[1]
    elif form == "nt":
        (M, K), N = a.shape, b.shape[0]
    else:
        (K, M), N = a.shape, b.shape[1]
    out_dtypes = out_dtypes or [out_dtype] * n_out
    per = 4 + sum(2 * jnp.dtype(d).itemsize for d in out_dtypes)
    per += sum(2 * e.dtype.itemsize for e, kd in zip(extras, extra_kinds) if kd == "tile")
    kc = (1408, 1056, 704, 512, 384, 256, 128) if form == "tn" else (896, 512, 384, 256, 128)
    mc = (1024, 896, 768, 640, 512, 384, 256, 128) if form == "tn" else (1408, 1056, 704, 528, 384, 256, 128, 64, 32, 16, 8)
    tm, tn, tk = _tiles(M, N, K, per, k_cands=kc, m_cands=mc, a_bytes=a.dtype.itemsize, b_bytes=b.dtype.itemsize)
    grid = (M // tm, N // tn, K // tk)
    if form == "nn":
        a_op = (a, (tm, tk), lambda i, j, k: (i, k))
        b_op = (b, (tk, tn), lambda i, j, k: (k, j))
    elif form == "nt":
        a_op = (a, (tm, tk), lambda i, j, k: (i, k))
        b_op = (b, (tn, tk), lambda i, j, k: (j, k))
    else:
        a_op = (a, (tk, tm), lambda i, j, k: (k, i))
        b_op = (b, (tk, tn), lambda i, j, k: (k, j))
    e_ops = []
    for e, kd in zip(extras, extra_kinds):
        if kd == "tile":
            e_ops.append((e, (tm, tn), lambda i, j, k: (i, j)))
        else:
            e_ops.append((e, (1, tn), lambda i, j, k: (0, j)))
    outs = [(jax.ShapeDtypeStruct((M, N), d), (tm, tn), lambda i, j, k: (i, j)) for d in out_dtypes]

    def epi(accs, ex):
        if epilogue is None:
            return [accs[0]]
        r = epilogue(accs[0], *ex)
        return list(r) if isinstance(r, (tuple, list)) else [r]

    res = _matmul(name, form, grid, [a_op], [b_op], [0], 1, e_ops, outs, epi, tm, tn)
    return res[0] if len(res) == 1 else res


def _rowwise(name, fn, ins, outs, tr, rows):
    n_in, n_out = len(ins), len(outs)
    nblk = rows // tr

    def body(*refs):
        i = pl.program_id(0)
        vals = fn(i * tr, *[r[...] for r in refs[:n_in]])
        for o, v, (_, _, kind) in zip(refs[n_in:], vals, outs):
            if kind == "row":
                o[...] = v.astype(o.dtype)
            else:
                @pl.when(i == 0)
                def _(o=o):
                    o[...] = jnp.zeros_like(o)
                o[...] += v.astype(o.dtype)

    in_specs = []
    for arr, w, cb in ins:
        if arr.shape[0] == 1 and rows != 1:
            in_specs.append(pl.BlockSpec((1, w), lambda i, cb=cb: (0, cb)))
        else:
            in_specs.append(pl.BlockSpec((tr, w), lambda i, cb=cb: (i, cb)))
    out_specs, out_shape = [], []
    for w, d, kind in outs:
        if kind == "row":
            out_specs.append(pl.BlockSpec((tr, w), lambda i: (i, 0)))
            out_shape.append(jax.ShapeDtypeStruct((rows, w), d))
        else:
            out_specs.append(pl.BlockSpec((1, w), lambda i: (0, 0)))
            out_shape.append(jax.ShapeDtypeStruct((1, w), d))
    return pl.pallas_call(
        body, name=name, grid=(nblk,), in_specs=in_specs, out_specs=out_specs, out_shape=out_shape,
        compiler_params=_cp(("arbitrary",)),
    )(*[a for a, _, _ in ins])


def _full(arr):
    return (arr, arr.shape[1], 0)


def _rmsnorm_fwd(name, h, w, out_dtype):
    D = h.shape[1]

    def fn(_, hb, wb):
        rstd = lax.rsqrt(jnp.mean(hb * hb, axis=1, keepdims=True) + EPS)
        return [hb * rstd * wb]
    return _rowwise(name, fn, [_full(h), _full(w)], [(D, out_dtype, "row")], ROW_BLOCK, h.shape[0])[0]


def _rmsnorm_bwd(name, h, w, dn, dh_in):
    D = h.shape[1]

    def fn(_, hb, wb, dnb, dhb):
        rstd = lax.rsqrt(jnp.mean(hb * hb, axis=1, keepdims=True) + EPS)
        xhat = hb * rstd
        dxh = dnb.astype(F32) * wb
        dh = rstd * (dxh - xhat * jnp.mean(dxh * xhat, axis=1, keepdims=True))
        return [dhb + dh, jnp.sum(dnb.astype(F32) * xhat, axis=0, keepdims=True)]
    return _rowwise(name, fn, [_full(h), _full(w), _full(dn), _full(dh_in)],
                    [(D, F32, "row"), (D, F32, "acc")], ROW_BLOCK, h.shape[0])


def _headnorm_fn(group, act):
    def f(o, gate, w):
        rstd = lax.rsqrt(jnp.mean(o * o, axis=1, keepdims=True) + EPS)
        return o * rstd * w * act(gate)
    return f


def _headnorm_fwd(name, o, gate_arr, gate_w, gate_cb, w, group, act):
    N = o.shape[1]
    f = _headnorm_fn(group, act)

    def fn(_, ob, gb, wb):
        parts = [f(ob[:, s:s + group], gb[:, s:s + group], wb[:, s:s + group]) for s in range(0, N, group)]
        return [jnp.concatenate(parts, axis=1)]
    return _rowwise(name, fn, [_full(o), (gate_arr, gate_w, gate_cb), _full(w)], [(N, BF16, "row")], ROW_BLOCK, o.shape[0])[0]


def _headnorm_bwd(name, o, gate_arr, gate_w, gate_cb, w, dout, group, act):
    N = o.shape[1]
    f = _headnorm_fn(group, act)

    def fn(_, ob, gb, wb, db):
        dos, dgs, dws = [], [], []
        for s in range(0, N, group):
            _, vjp = jax.vjp(f, ob[:, s:s + group], gb[:, s:s + group], jnp.broadcast_to(wb[:, s:s + group], (ob.shape[0], group)))
            do, dgt, dw = vjp(db[:, s:s + group])
            dos.append(do)
            dgs.append(dgt)
            dws.append(jnp.sum(dw, axis=0, keepdims=True))
        return [jnp.concatenate(dos, axis=1), jnp.concatenate(dgs, axis=1), jnp.concatenate(dws, axis=1)]
    return _rowwise(name, fn, [_full(o), (gate_arr, gate_w, gate_cb), _full(w), _full(dout)],
                    [(N, F32, "row"), (N, F32, "row"), (N, F32, "acc")], ROW_BLOCK, o.shape[0])


def _colsum(name, a):
    def fn(_, ab):
        return [jnp.sum(ab.astype(F32), axis=0, keepdims=True)]
    tr = _div(a.shape[0], (512, 384, 256, 128, 64))
    return _rowwise(name, fn, [_full(a)], [(a.shape[1], F32, "acc")], tr, a.shape[0])[0]


def _loss_head(name, h, w, tgt, n_meta, seq):
    D = h.shape[1]
    tr = ROW_BLOCK

    def fn(row0, hb, wb, tb):
        row = row0 + _iota((tr, 1), 0)
        valid = (row >= n_meta) & (row < n_meta + seq)
        rstd = lax.rsqrt(jnp.mean(hb * hb, axis=1, keepdims=True) + EPS)
        xhat = hb * rstd
        err = jnp.where(valid, xhat * wb - tb, 0.0)
        loss = 0.5 * jnp.sum(jnp.mean(err * err, axis=1, keepdims=True), axis=0, keepdims=True)
        dy = err * (1.0 / D)
        dxh = dy * wb
        dh = rstd * (dxh - xhat * jnp.mean(dxh * xhat, axis=1, keepdims=True))
        return [dh, jnp.sum(dy * xhat, axis=0, keepdims=True), jnp.broadcast_to(loss, (1, LANES))]
    return _rowwise(name, fn, [_full(h), _full(w), _full(tgt)],
                    [(D, F32, "row"), (D, F32, "acc"), (LANES, F32, "acc")], tr, h.shape[0])


def _chunk_valid(ci, n_meta):
    lim = jnp.where(ci == 0, n_meta, CHUNK)
    return _iota((CHUNK, 1), 0) < lim, _iota((1, CHUNK), 1) < lim


def _tri_masks():
    r = _iota((CHUNK, CHUNK), 0)
    c = _iota((CHUNK, CHUNK), 1)
    return r >= c, r > c, r <= c


def _mlstm_chunk(vc, vr, m_st, c_st, n_st, q, k, v, li_c, lf_c, li_r, lf_r):
    tril, _, triu = _tri_masks()
    dk = q.shape[1]
    li_c = jnp.where(vc, li_c, NEG)
    li_r = jnp.where(vr, li_r, NEG)
    lf_c = jnp.where(vc, _log_sigmoid(lf_c), 0.0)
    lf_r = jnp.where(vr, _log_sigmoid(lf_r), 0.0)
    b_c = jnp.sum(jnp.where(tril, lf_r, 0.0), axis=1, keepdims=True)
    b_r = jnp.sum(jnp.where(triu, lf_c, 0.0), axis=0, keepdims=True)
    b_last = jnp.sum(lf_r, axis=1, keepdims=True)
    log_w = jnp.where(tril, b_c - b_r + li_r, NEG)
    log_init = b_c + m_st
    m_t = lax.stop_gradient(jnp.maximum(log_init, jnp.max(log_w, axis=1, keepdims=True)))
    w = jnp.exp(log_w - m_t)
    w_init = jnp.exp(log_init - m_t)
    qs = q * (dk ** -0.5)
    qk = _bnt(qs, k) * w
    num = w_init * _bnn(qs, c_st) + _bnn(qk, v)
    den = w_init * jnp.sum(qs * n_st, axis=1, keepdims=True) + jnp.sum(qk, axis=1, keepdims=True)
    h = num / jnp.maximum(jnp.abs(den), jnp.exp(-m_t))
    log_end_init = b_last + m_st
    log_end_r = b_last - b_r + li_r
    m_new = lax.stop_gradient(jnp.maximum(log_end_init, jnp.max(log_end_r, axis=1, keepdims=True)))
    a_init = jnp.exp(log_end_init - m_new)
    a_c = jnp.exp(b_last - b_c + li_c - m_new)
    ka = k * a_c
    c_new = a_init * c_st + _btn(ka, v)
    n_new = a_init * n_st + jnp.sum(ka, axis=0, keepdims=True)
    return (c_new, n_new, h), m_new


def _mlstm_specs(heads, dk, dv, qoff, koff, voff):
    q = pl.BlockSpec((CHUNK, dk), lambda h, c: (c, qoff + h))
    k = pl.BlockSpec((CHUNK, dk), lambda h, c: (c, koff + h))
    v = pl.BlockSpec((CHUNK, dv), lambda h, c: (c, voff + h))
    return q, k, v


def _mlstm_core_fwd(pc, li_c, lf_c, li_r, lf_r, heads, dk, dv, n_meta):
    TC = pc.shape[0]
    NC = TC // CHUNK

    def body(q_ref, k_ref, v_ref, lic, lfc, lir, lfr, h_ref, cs_ref, ns_ref, ms_ref, c_s, n_s, m_s):
        ci = pl.program_id(1)

        @pl.when(ci == 0)
        def _():
            c_s[...] = jnp.zeros_like(c_s)
            n_s[...] = jnp.zeros_like(n_s)
            m_s[...] = jnp.zeros_like(m_s)

        cs_ref[0, 0] = c_s[...]
        ns_ref[0, 0] = n_s[...]
        ms_ref[0, 0] = m_s[...]
        vc, vr = _chunk_valid(ci, n_meta)
        (c_new, n_new, h), m_new = _mlstm_chunk(vc, vr, m_s[...], c_s[...], n_s[...], q_ref[...], k_ref[...], v_ref[...],
                                                lic[0], lfc[0], lir[0, 0], lfr[0, 0])
        h_ref[...] = h
        c_s[...] = c_new
        n_s[...] = n_new
        m_s[...] = m_new

    qs, ks, vs = _mlstm_specs(heads, dk, dv, 0, heads, (2 * heads * dk) // dv)
    colspec = pl.BlockSpec((1, CHUNK, 1), lambda h, c: (h, c, 0))
    rowspec = pl.BlockSpec((1, 1, 1, CHUNK), lambda h, c: (h, c, 0, 0))
    return pl.pallas_call(
        body, name="mlstm_core_fwd", grid=(heads, NC),
        in_specs=[qs, ks, vs, colspec, colspec, rowspec, rowspec],
        out_specs=[pl.BlockSpec((CHUNK, dv), lambda h, c: (c, h)),
                   pl.BlockSpec((1, 1, dk, dv), lambda h, c: (h, c, 0, 0)),
                   pl.BlockSpec((1, 1, 1, dk), lambda h, c: (h, c, 0, 0)),
                   pl.BlockSpec((1, 1, 1, 1), lambda h, c: (h, c, 0, 0))],
        out_shape=[jax.ShapeDtypeStruct((TC, heads * dv), F32),
                   jax.ShapeDtypeStruct((heads, NC, dk, dv), F32),
                   jax.ShapeDtypeStruct((heads, NC, 1, dk), F32),
                   jax.ShapeDtypeStruct((heads, NC, 1, 1), F32)],
        scratch_shapes=[pltpu.VMEM((dk, dv), F32), pltpu.VMEM((1, dk), F32), pltpu.VMEM((1, 1), F32)],
        compiler_params=_cp(("parallel", "arbitrary")),
    )(pc, pc, pc, li_c, lf_c, li_r, lf_r)


def _mlstm_core_bwd(pc, li_c, lf_c, li_r, lf_r, cs, ns, ms, dh, heads, dk, dv, n_meta):
    TC = pc.shape[0]
    NC = TC // CHUNK

    def body(q_ref, k_ref, v_ref, lic, lfc, lir, lfr, cs_ref, ns_ref, ms_ref, dh_ref,
             dq_ref, dk_ref, dv_ref, dlic, dlfc, dlir, dlfr, dc_s, dn_s):
        step = pl.program_id(1)
        ci = NC - 1 - step

        @pl.when(step == 0)
        def _():
            dc_s[...] = jnp.zeros_like(dc_s)
            dn_s[...] = jnp.zeros_like(dn_s)

        vc, vr = _chunk_valid(ci, n_meta)
        m_st = ms_ref[0, 0]
        fn = lambda *a: _mlstm_chunk(vc, vr, m_st, *a)
        _, vjp, _ = jax.vjp(fn, cs_ref[0, 0], ns_ref[0, 0], q_ref[...], k_ref[...], v_ref[...],
                            lic[0], lfc[0], lir[0, 0], lfr[0, 0], has_aux=True)
        dc, dn, dq, dkk, dvv, g0, g1, g2, g3 = vjp((dc_s[...], dn_s[...], dh_ref[...]))
        dq_ref[...] = dq
        dk_ref[...] = dkk
        dv_ref[...] = dvv
        dlic[0] = g0
        dlfc[0] = g1
        dlir[0, 0] = g2
        dlfr[0, 0] = g3
        dc_s[...] = dc
        dn_s[...] = dn

    rev = lambda c: NC - 1 - c
    q = pl.BlockSpec((CHUNK, dk), lambda h, c: (rev(c), h))
    k = pl.BlockSpec((CHUNK, dk), lambda h, c: (rev(c), heads + h))
    v = pl.BlockSpec((CHUNK, dv), lambda h, c: (rev(c), (2 * heads * dk) // dv + h))
    colspec = pl.BlockSpec((1, CHUNK, 1), lambda h, c: (h, rev(c), 0))
    rowspec = pl.BlockSpec((1, 1, 1, CHUNK), lambda h, c: (h, rev(c), 0, 0))
    return pl.pallas_call(
        body, name="mlstm_core_bwd", grid=(heads, NC),
        in_specs=[q, k, v, colspec, colspec, rowspec, rowspec,
                  pl.BlockSpec((1, 1, dk, dv), lambda h, c: (h, rev(c), 0, 0)),
                  pl.BlockSpec((1, 1, 1, dk), lambda h, c: (h, rev(c), 0, 0)),
                  pl.BlockSpec((1, 1, 1, 1), lambda h, c: (h, rev(c), 0, 0)),
                  pl.BlockSpec((CHUNK, dv), lambda h, c: (rev(c), h))],
        out_specs=[pl.BlockSpec((CHUNK, dk), lambda h, c: (rev(c), h)),
                   pl.BlockSpec((CHUNK, dk), lambda h, c: (rev(c), h)),
                   pl.BlockSpec((CHUNK, dv), lambda h, c: (rev(c), h)),
                   colspec, colspec, rowspec, rowspec],
        out_shape=[jax.ShapeDtypeStruct((TC, heads * dk), F32), jax.ShapeDtypeStruct((TC, heads * dk), F32),
                   jax.ShapeDtypeStruct((TC, heads * dv), F32),
                   jax.ShapeDtypeStruct(li_c.shape, F32), jax.ShapeDtypeStruct(lf_c.shape, F32),
                   jax.ShapeDtypeStruct(li_r.shape, F32), jax.ShapeDtypeStruct(lf_r.shape, F32)],
        scratch_shapes=[pltpu.VMEM((dk, dv), F32), pltpu.VMEM((1, dk), F32)],
        compiler_params=_cp(("parallel", "arbitrary")),
    )(pc, pc, pc, li_c, lf_c, li_r, lf_r, cs, ns, ms, dh)


@jax.custom_vjp
def _tri_solve(low, rhs):
    return _tri_solve_fwd(low, rhs)[0]


def _tri_solve_fwd(low, rhs):
    n = low.shape[0]
    eye = (_iota((n, n), 0) == _iota((n, n), 1)).astype(F32)
    neg = -low
    t = eye + neg
    p = neg
    for _ in range(int(math.log2(n)) - 1):
        p = _rhnn(p, p)
        t = t + _rhnn(t, p)
    sol = _rhnn(t, rhs)
    return sol, (t, sol)


def _tri_solve_bwd(res, ct):
    t, sol = res
    d_rhs = _rhtn(t, ct)
    return -_rhnt(d_rhs, sol), d_rhs


_tri_solve.defvjp(_tri_solve_fwd, _tri_solve_bwd)


def _l2norm(x):
    return x * lax.rsqrt(jnp.sum(x * x, axis=1, keepdims=True) + EPS)


def _gdn_chunk(vc, vr, s0, s1, q, k, v0, v1, g_c0, g_c1, b_c0, b_c1, g_r0, g_r1):
    tril, strict, triu = _tri_masks()
    dk = q.shape[1]
    qn = _l2norm(q) * (dk ** -0.5)
    kn = _l2norm(k)
    qk = _bnt(qn, kn)
    outs = []
    for s_st, v, g_c, b_c, g_r in ((s0, v0, g_c0, b_c0, g_r0), (s1, v1, g_c1, b_c1, g_r1)):
        g_c = jnp.where(vc, g_c, 0.0)
        g_r = jnp.where(vr, g_r, 0.0)
        b_c = jnp.where(vc, b_c, 0.0)
        gc_c = jnp.sum(jnp.where(tril, g_r, 0.0), axis=1, keepdims=True)
        gc_r = jnp.sum(jnp.where(triu, g_c, 0.0), axis=0, keepdims=True)
        g_last = jnp.sum(g_r, axis=1, keepdims=True)
        decay = jnp.exp(jnp.where(tril, gc_c - gc_r, NEG))
        kb = kn * b_c
        low = jnp.where(strict, _bnt(kb, kn) * decay, 0.0)
        eg = jnp.exp(gc_c)
        sol = _tri_solve(low, jnp.concatenate([v * b_c, kb * eg], axis=1))
        u_vec, w_vec = sol[:, :dk], sol[:, dk:]
        v_new = u_vec - _bnn(w_vec, s_st)
        o = _bnn(qn * eg, s_st) + _bnn(qk * decay, v_new)
        s_new = jnp.exp(g_last) * s_st + _btn(kn * jnp.exp(g_last - gc_c), v_new)
        outs.append((s_new, o))
    return outs[0][0], outs[1][0], outs[0][1], outs[1][1]


def _gdn_core_fwd(qkv, g_c, b_c, g_r, qk_heads, dk, n_meta):
    TC = qkv.shape[0]
    NC = TC // CHUNK
    H = qk_heads

    def body(q_ref, k_ref, v0_ref, v1_ref, gc0, gc1, bc0, bc1, gr0, gr1, o_ref, st_ref, s_s):
        ci = pl.program_id(1)

        @pl.when(ci == 0)
        def _():
            s_s[...] = jnp.zeros_like(s_s)

        st_ref[0, 0] = s_s[...]
        vc, vr = _chunk_valid(ci, n_meta)
        s0n, s1n, o0, o1 = _gdn_chunk(vc, vr, s_s[0], s_s[1], q_ref[...], k_ref[...], v0_ref[...], v1_ref[...],
                                      gc0[0], gc1[0], bc0[0], bc1[0], gr0[0, 0], gr1[0, 0])
        o_ref[:, 0:dk] = o0
        o_ref[:, dk:2 * dk] = o1
        s_s[0] = s0n
        s_s[1] = s1n

    blk = lambda off, e=None: pl.BlockSpec((CHUNK, dk), (lambda h, c: (c, off + h)) if e is None else (lambda h, c: (c, off + 2 * h + e)))
    col = lambda e: pl.BlockSpec((1, CHUNK, 1), lambda h, c: (2 * h + e, c, 0))
    row = lambda e: pl.BlockSpec((1, 1, 1, CHUNK), lambda h, c: (2 * h + e, c, 0, 0))
    return pl.pallas_call(
        body, name="gdn_core_fwd", grid=(H, NC),
        in_specs=[blk(0), blk(H), blk(2 * H, 0), blk(2 * H, 1), col(0), col(1), col(0), col(1), row(0), row(1)],
        out_specs=[pl.BlockSpec((CHUNK, 2 * dk), lambda h, c: (c, h)),
                   pl.BlockSpec((1, 1, 2, dk, dk), lambda h, c: (h, c, 0, 0, 0))],
        out_shape=[jax.ShapeDtypeStruct((TC, 2 * H * dk), F32), jax.ShapeDtypeStruct((H, NC, 2, dk, dk), F32)],
        scratch_shapes=[pltpu.VMEM((2, dk, dk), F32)],
        compiler_params=_cp(("parallel", "arbitrary")),
    )(qkv, qkv, qkv, qkv, g_c, g_c, b_c, b_c, g_r, g_r)


def _gdn_core_bwd(qkv, g_c, b_c, g_r, st, do, qk_heads, dk, n_meta):
    TC = qkv.shape[0]
    NC = TC // CHUNK
    H = qk_heads

    def body(q_ref, k_ref, v0_ref, v1_ref, gc0, gc1, bc0, bc1, gr0, gr1, st_ref, do_ref,
             dq_ref, dk_ref, dv_ref, dgc0, dgc1, dbc0, dbc1, dgr0, dgr1, ds_s):
        step = pl.program_id(1)
        ci = NC - 1 - step

        @pl.when(step == 0)
        def _():
            ds_s[...] = jnp.zeros_like(ds_s)

        vc, vr = _chunk_valid(ci, n_meta)
        fn = lambda *a: _gdn_chunk(vc, vr, *a)
        _, vjp = jax.vjp(fn, st_ref[0, 0, 0], st_ref[0, 0, 1], q_ref[...], k_ref[...], v0_ref[...], v1_ref[...],
                         gc0[0], gc1[0], bc0[0], bc1[0], gr0[0, 0], gr1[0, 0])
        d = vjp((ds_s[0], ds_s[1], do_ref[:, 0:dk], do_ref[:, dk:2 * dk]))
        ds_s[0] = d[0]
        ds_s[1] = d[1]
        dq_ref[...] = d[2]
        dk_ref[...] = d[3]
        dv_ref[:, 0:dk] = d[4]
        dv_ref[:, dk:2 * dk] = d[5]
        for ref, val in zip((dgc0, dgc1, dbc0, dbc1), d[6:10]):
            ref[0] = val
        for ref, val in zip((dgr0, dgr1), d[10:12]):
            ref[0, 0] = val

    rev = lambda c: NC - 1 - c
    blk = lambda off, e=None: pl.BlockSpec((CHUNK, dk), (lambda h, c: (rev(c), off + h)) if e is None else (lambda h, c: (rev(c), off + 2 * h + e)))
    col = lambda e: pl.BlockSpec((1, CHUNK, 1), lambda h, c: (2 * h + e, rev(c), 0))
    row = lambda e: pl.BlockSpec((1, 1, 1, CHUNK), lambda h, c: (2 * h + e, rev(c), 0, 0))
    ocol = pl.BlockSpec((1, CHUNK, 1), lambda h, c: (h, rev(c), 0))
    orow = pl.BlockSpec((1, 1, 1, CHUNK), lambda h, c: (h, rev(c), 0, 0))
    pair = pl.BlockSpec((CHUNK, 2 * dk), lambda h, c: (rev(c), h))
    one = pl.BlockSpec((CHUNK, dk), lambda h, c: (rev(c), h))
    gshape = jax.ShapeDtypeStruct((H, TC, 1), F32)
    rshape = jax.ShapeDtypeStruct((H, NC, 1, CHUNK), F32)
    return pl.pallas_call(
        body, name="gdn_core_bwd", grid=(H, NC),
        in_specs=[blk(0), blk(H), blk(2 * H, 0), blk(2 * H, 1), col(0), col(1), col(0), col(1), row(0), row(1),
                  pl.BlockSpec((1, 1, 2, dk, dk), lambda h, c: (h, rev(c), 0, 0, 0)), pair],
        out_specs=[one, one, pair, ocol, ocol, ocol, ocol, orow, orow],
        out_shape=[jax.ShapeDtypeStruct((TC, H * dk), F32), jax.ShapeDtypeStruct((TC, H * dk), F32),
                   jax.ShapeDtypeStruct((TC, 2 * H * dk), F32),
                   gshape, gshape, gshape, gshape, rshape, rshape],
        scratch_shapes=[pltpu.VMEM((2, dk, dk), F32)],
        compiler_params=_cp(("parallel", "arbitrary")),
    )(qkv, qkv, qkv, qkv, g_c, g_c, b_c, b_c, g_r, g_r, st, do)


def _gdn_gate_fn(a_pre, b_pre, a_log, dt_bias):
    return -jnp.exp(a_log) * _softplus(a_pre + dt_bias), _sigmoid(b_pre)


def _gdn_gates_fwd(a_pre, b_pre, a_log, dt_bias):
    n = a_pre.shape[1]

    def fn(_, ab, bb, al, dt):
        g, beta = _gdn_gate_fn(ab, bb, al, dt)
        return [g, beta]
    return _rowwise("gdn_gates_fwd", fn, [_full(a_pre), _full(b_pre), _full(a_log), _full(dt_bias)],
                    [(n, F32, "row"), (n, F32, "row")], ROW_BLOCK, a_pre.shape[0])


def _gdn_gates_bwd(a_pre, b_pre, a_log, dt_bias, dg, dbeta):
    n = a_pre.shape[1]

    def fn(_, ab, bb, al, dt, dgb, dbb):
        rows = ab.shape[0]
        _, vjp = jax.vjp(_gdn_gate_fn, ab, bb, jnp.broadcast_to(al, (rows, n)), jnp.broadcast_to(dt, (rows, n)))
        da, db, dal, ddt = vjp((dgb, dbb))
        return [da, db, jnp.sum(dal, axis=0, keepdims=True), jnp.sum(ddt, axis=0, keepdims=True)]
    return _rowwise("gdn_gates_bwd", fn, [_full(a_pre), _full(b_pre), _full(a_log), _full(dt_bias), _full(dg), _full(dbeta)],
                    [(n, F32, "row"), (n, F32, "row"), (n, F32, "acc"), (n, F32, "acc")], ROW_BLOCK, a_pre.shape[0])


def _shift_down(cur, prev, j):
    row = _iota(cur.shape, 0)
    return jnp.where(row >= j, pltpu.roll(cur, j, 0), pltpu.roll(prev, j, 0))


def _shift_up(cur, nxt, j):
    n = cur.shape[0]
    row = _iota(cur.shape, 0)
    return jnp.where(row < n - j, pltpu.roll(cur, n - j, 0), pltpu.roll(nxt, n - j, 0))


def _conv_acc(cur, prev, w):
    acc = cur * w[GDN_CONV - 1:GDN_CONV, :]
    for j in range(1, GDN_CONV):
        acc = acc + _shift_down(cur, prev, j) * w[GDN_CONV - 1 - j:GDN_CONV - j, :]
    return acc


def _conv_tiles(width):
    return _div(width, (1024, 512, 256, 128))


def _conv_fwd(p, w, width):
    TP = p.shape[0]
    nb, tn = TP // ROW_BLOCK, _conv_tiles(width)

    def body(cur_ref, prev_ref, w_ref, y_ref):
        i = pl.program_id(1)
        prev = jnp.where(i > 0, prev_ref[...], 0.0)
        y_ref[...] = _silu(_conv_acc(cur_ref[...], prev, w_ref[...]))

    return pl.pallas_call(
        body, name="gdn_conv_fwd", grid=(width // tn, nb),
        in_specs=[pl.BlockSpec((ROW_BLOCK, tn), lambda j, i: (i, j)),
                  pl.BlockSpec((ROW_BLOCK, tn), lambda j, i: (jnp.maximum(i - 1, 0), j)),
                  pl.BlockSpec((GDN_CONV, tn), lambda j, i: (0, j))],
        out_specs=pl.BlockSpec((ROW_BLOCK, tn), lambda j, i: (i, j)),
        out_shape=jax.ShapeDtypeStruct((TP, width), F32),
        compiler_params=_cp(("parallel", "arbitrary")),
    )(p, p, w)


def _conv_bwd_pre(p, w, dy, width):
    TP = p.shape[0]
    nb, tn = TP // ROW_BLOCK, _conv_tiles(width)

    def body(cur_ref, prev_ref, w_ref, dy_ref, da_ref, dw_ref):
        i = pl.program_id(1)
        cur = cur_ref[...]
        prev = jnp.where(i > 0, prev_ref[...], 0.0)
        acc = _conv_acc(cur, prev, w_ref[...])
        s = _sigmoid(acc)
        da = dy_ref[...] * (s * (1.0 + acc * (1.0 - s)))
        da_ref[...] = da

        @pl.when(i == 0)
        def _():
            dw_ref[...] = jnp.zeros_like(dw_ref)

        rows = [jnp.sum(da * (cur if j == 0 else _shift_down(cur, prev, j)), axis=0, keepdims=True)
                for j in range(GDN_CONV - 1, -1, -1)]
        dw_ref[...] += jnp.concatenate(rows, axis=0)

    return pl.pallas_call(
        body, name="gdn_conv_bwd_pre", grid=(width // tn, nb),
        in_specs=[pl.BlockSpec((ROW_BLOCK, tn), lambda j, i: (i, j)),
                  pl.BlockSpec((ROW_BLOCK, tn), lambda j, i: (jnp.maximum(i - 1, 0), j)),
                  pl.BlockSpec((GDN_CONV, tn), lambda j, i: (0, j)),
                  pl.BlockSpec((ROW_BLOCK, tn), lambda j, i: (i, j))],
        out_specs=[pl.BlockSpec((ROW_BLOCK, tn), lambda j, i: (i, j)), pl.BlockSpec((GDN_CONV, tn), lambda j, i: (0, j))],
        out_shape=[jax.ShapeDtypeStruct((TP, width), F32), jax.ShapeDtypeStruct((GDN_CONV, width), F32)],
        compiler_params=_cp(("parallel", "arbitrary")),
    )(p, p, w, dy)


def _conv_bwd_dx(da, w, width):
    TP = da.shape[0]
    nb, tn = TP // ROW_BLOCK, _conv_tiles(width)

    def body(cur_ref, nxt_ref, w_ref, dx_ref):
        i = pl.program_id(1)
        cur = cur_ref[...]
        nxt = jnp.where(i < nb - 1, nxt_ref[...], 0.0)
        w_all = w_ref[...]
        dx = cur * w_all[GDN_CONV - 1:GDN_CONV, :]
        for j in range(1, GDN_CONV):
            dx = dx + _shift_up(cur, nxt, j) * w_all[GDN_CONV - 1 - j:GDN_CONV - j, :]
        dx_ref[...] = dx

    return pl.pallas_call(
        body, name="gdn_conv_bwd_dx", grid=(width // tn, nb),
        in_specs=[pl.BlockSpec((ROW_BLOCK, tn), lambda j, i: (i, j)),
                  pl.BlockSpec((ROW_BLOCK, tn), lambda j, i: (jnp.minimum(i + 1, nb - 1), j)),
                  pl.BlockSpec((GDN_CONV, tn), lambda j, i: (0, j))],
        out_specs=pl.BlockSpec((ROW_BLOCK, tn), lambda j, i: (i, j)),
        out_shape=jax.ShapeDtypeStruct((TP, width), F32),
        compiler_params=_cp(("parallel", "arbitrary")),
    )(da, da, w)


def _pool_bands(i, win):
    n = ROW_BLOCK
    t = _iota((n, n), 0)
    s = _iota((n, n), 1)
    cnt = jnp.minimum(i * n + t + 1, win).astype(F32)
    cur = jnp.where((t - s >= 0) & (t - s < win), 1.0 / cnt, 0.0)
    prev = jnp.where((t + n - s < win) & (i > 0), 1.0 / cnt, 0.0)
    return cur, prev


def _pool_fwd(u):
    TP, D = u.shape
    nb, grp = TP // ROW_BLOCK, D // len(POOL_WINDOWS)

    def body(cur_ref, prev_ref, out_ref):
        i = pl.program_id(0)
        for gi, win in enumerate(POOL_WINDOWS):
            sl = slice(gi * grp, (gi + 1) * grp)
            bc, bp = _pool_bands(i, win)
            cur = cur_ref[:, sl]
            out_ref[:, sl] = (_rhnn(bc, cur) + _rhnn(bp, prev_ref[:, sl]) - cur).astype(out_ref.dtype)

    return pl.pallas_call(
        body, name="pool_fwd", grid=(nb,),
        in_specs=[pl.BlockSpec((ROW_BLOCK, D), lambda i: (i, 0)),
                  pl.BlockSpec((ROW_BLOCK, D), lambda i: (jnp.maximum(i - 1, 0), 0))],
        out_specs=pl.BlockSpec((ROW_BLOCK, D), lambda i: (i, 0)),
        out_shape=jax.ShapeDtypeStruct((TP, D), BF16),
        compiler_params=_cp(("arbitrary",)),
    )(u, u)


def _pool_bwd(dp):
    TP, D = dp.shape
    nb, grp = TP // ROW_BLOCK, D // len(POOL_WINDOWS)

    def body(cur_ref, nxt_ref, out_ref):
        i = pl.program_id(0)
        for gi, win in enumerate(POOL_WINDOWS):
            sl = slice(gi * grp, (gi + 1) * grp)
            bc, _ = _pool_bands(i, win)
            _, bp = _pool_bands(i + 1, win)
            cur = cur_ref[:, sl]
            nxt = jnp.where(i < nb - 1, nxt_ref[:, sl], 0.0)
            out_ref[:, sl] = _rhtn(bc, cur) + _rhtn(bp, nxt) - cur

    return pl.pallas_call(
        body, name="pool_bwd", grid=(nb,),
        in_specs=[pl.BlockSpec((ROW_BLOCK, D), lambda i: (i, 0)),
                  pl.BlockSpec((ROW_BLOCK, D), lambda i: (jnp.minimum(i + 1, nb - 1), 0))],
        out_specs=pl.BlockSpec((ROW_BLOCK, D), lambda i: (i, 0)),
        out_shape=jax.ShapeDtypeStruct((TP, D), F32),
        compiler_params=_cp(("arbitrary",)),
    )(dp, dp)


def _rot_matrix(dh):
    s = _iota((dh, dh), 0)
    t = _iota((dh, dh), 1)
    return jnp.where(s == t + dh // 2, -1.0, 0.0) + jnp.where(s == t - dh // 2, 1.0, 0.0)


def _swa_block(i, t_real, q, k_prev, k_cur, v_prev, v_cur, cos_q, sin_q, cos_p, sin_p, sink):
    n = ROW_BLOCK
    dh = q.shape[1]
    g = q.shape[0] // n
    rot = _rot_matrix(dh)
    rope = lambda x, c, s: x * c + _hnn(x, rot) * s
    qr = rope(q, jnp.concatenate([cos_q] * g, axis=0), jnp.concatenate([sin_q] * g, axis=0))
    kb = jnp.concatenate([rope(k_prev, cos_p, sin_p), rope(k_cur, cos_q, sin_q)], axis=0)
    vb = jnp.concatenate([v_prev, v_cur], axis=0)
    s = _bnt(qr, kb) * (dh ** -0.5)
    qpos = i * n + (_iota((g * n, 2 * n), 0) % n)
    kpos = (i - 1) * n + _iota((g * n, 2 * n), 1)
    mask = (kpos <= qpos) & (qpos - kpos < SWA_WINDOW) & (kpos >= 0) & (kpos < t_real)
    s = jnp.where(mask, s, NEG)
    m = lax.stop_gradient(jnp.maximum(jnp.max(s, axis=1, keepdims=True), sink))
    e = jnp.where(mask, jnp.exp(s - m), 0.0)
    den = jnp.sum(e, axis=1, keepdims=True) + jnp.exp(sink - m)
    return _bnn(e / den, vb)


def _swa_core(q, k, v, cos, sin, sink, t_real, do=None):
    hkv, g, TP, dh = q.shape
    n = ROW_BLOCK
    nb = TP // n
    bwd = do is not None

    def body(*refs):
        q_ref, kp_ref, kc_ref, vp_ref, vc_ref, cq, sq, cpv, spv, sink_ref = refs[:10]
        i = pl.program_id(1)
        fn = lambda *a: _swa_block(i, t_real, *a)
        args = (q_ref[0].reshape(g * n, dh), kp_ref[0], kc_ref[0], vp_ref[0], vc_ref[0],
                cq[...], sq[...], cpv[...], spv[...], sink_ref[0])
        if not bwd:
            refs[10][0] = fn(*args).reshape(g, n, dh)
            return
        do_ref, dq_ref, dkp_ref, dkc_ref, dvp_ref, dvc_ref, dsink_ref = refs[10:17]
        _, vjp = jax.vjp(fn, *args)
        d = vjp(do_ref[0].reshape(g * n, dh))
        dq_ref[0] = d[0].reshape(g, n, dh)
        dkp_ref[0] = d[1]
        dkc_ref[0] = d[2]
        dvp_ref[0] = d[3]
        dvc_ref[0] = d[4]

        @pl.when(i == 0)
        def _():
            dsink_ref[...] = jnp.zeros_like(dsink_ref)
        dsink_ref[0] += d[9]

    qspec = pl.BlockSpec((1, g, n, dh), lambda h, i: (h, 0, i, 0))
    cur = pl.BlockSpec((1, n, dh), lambda h, i: (h, i, 0))
    prev = pl.BlockSpec((1, n, dh), lambda h, i: (h, jnp.maximum(i - 1, 0), 0))
    tcur = pl.BlockSpec((n, dh), lambda h, i: (i, 0))
    tprev = pl.BlockSpec((n, dh), lambda h, i: (jnp.maximum(i - 1, 0), 0))
    sspec = pl.BlockSpec((1, g * n, 1), lambda h, i: (h, 0, 0))
    in_specs = [qspec, prev, cur, prev, cur, tcur, tcur, tprev, tprev, sspec]
    ins = [q, k, k, v, v, cos, sin, cos, sin, sink]
    if not bwd:
        out_specs, out_shape = [qspec], [jax.ShapeDtypeStruct(q.shape, F32)]
    else:
        in_specs.append(qspec)
        ins.append(do)
        kv = jax.ShapeDtypeStruct(k.shape, F32)
        out_specs = [qspec, cur, cur, cur, cur, sspec]
        out_shape = [jax.ShapeDtypeStruct(q.shape, F32), kv, kv, kv, kv, jax.ShapeDtypeStruct(sink.shape, F32)]
    return pl.pallas_call(
        body, name="swa_core_bwd" if bwd else "swa_core_fwd", grid=(hkv, nb),
        in_specs=in_specs, out_specs=out_specs, out_shape=out_shape,
        compiler_params=_cp(("parallel", "arbitrary")),
    )(*ins)


def _coords():
    return lax.axis_index("x"), lax.axis_index("y"), lax.axis_index("c")


def _other_chips(x, y):
    return [(1 - x, y), (x, 1 - y), (1 - x, 1 - y)]


def _rcopy(src, dst, send, recv, dev):
    return pltpu.make_async_remote_copy(src_ref=src, dst_ref=dst, send_sem=send, recv_sem=recv,
                                        device_id=dev, device_id_type=MESH)


ANY = pl.BlockSpec(memory_space=pl.ANY)


def _allgather(name, shards, split):
    n = len(shards)

    def body(*refs):
        ins, outs = refs[:n], refs[n:2 * n]
        send, recv, loc = refs[2 * n:]
        x, y, c = _coords()
        s_me = 2 * x + y
        chips = _other_chips(x, y)
        started = []
        for i in range(n):
            lc = pltpu.make_async_copy(ins[i], outs[i].at[s_me], loc.at[i])
            lc.start()
            started.append(lc)
        if split:
            halves = [shards[i].shape[0] // 2 for i in range(n)]
            mine = [pl.ds(c * h, h) for h in halves]
            other = [pl.ds((1 - c) * h, h) for h in halves]
        sends = []
        for i in range(n):
            for j, (px, py) in enumerate(chips):
                if split:
                    cp = _rcopy(ins[i].at[mine[i]], outs[i].at[s_me, mine[i]], send.at[i, j], recv.at[i, j], (px, py, c))
                else:
                    cp = _rcopy(ins[i], outs[i].at[s_me], send.at[i, j], recv.at[i, j], (px, py, c))
                cp.start()
                sends.append(cp)
        for i in range(n):
            for j, (px, py) in enumerate(chips):
                s_j = 2 * px + py
                if split:
                    land = outs[i].at[s_j, mine[i]]
                    _rcopy(land, land, send.at[i, j], recv.at[i, j], (px, py, c)).wait_recv()
                    fw = _rcopy(land, land, send.at[i, 3 + j], recv.at[i, 3 + j], (x, y, 1 - c))
                    fw.start()
                    sends.append(fw)
                else:
                    land = outs[i].at[s_j]
                    _rcopy(land, land, send.at[i, j], recv.at[i, j], (px, py, c)).wait_recv()
        if split:
            for i in range(n):
                for j, (px, py) in enumerate(chips):
                    land = outs[i].at[2 * px + py, other[i]]
                    _rcopy(land, land, send.at[i, 3 + j], recv.at[i, 3 + j], (x, y, 1 - c)).wait_recv()
        for cp in sends:
            cp.wait_send()
        for lc in started:
            lc.wait()

    nsem = 6 if split else 3
    return pl.pallas_call(
        body, name=name,
        in_specs=[ANY] * n, out_specs=[ANY] * n,
        out_shape=[jax.ShapeDtypeStruct((N_CHIPS,) + s.shape, s.dtype) for s in shards],
        scratch_shapes=[pltpu.SemaphoreType.DMA((n, nsem)), pltpu.SemaphoreType.DMA((n, nsem)), pltpu.SemaphoreType.DMA((n,))],
        compiler_params=pltpu.CompilerParams(has_side_effects=True),
    )(*shards)


def _rs_pair_exchange(name, grads):
    n = len(grads)

    def body(*refs):
        ins, bufs = refs[:n], refs[n:2 * n]
        send, recv = refs[2 * n:]
        x, y, c = _coords()
        cps = []
        for i in range(n):
            half = grads[i].shape[1] // 2
            cp = _rcopy(ins[i].at[pl.ds(0, N_CHIPS), pl.ds((1 - c) * half, half)], bufs[i], send.at[i], recv.at[i], (x, y, 1 - c))
            cp.start()
            cps.append(cp)
        for cp in cps:
            cp.wait_recv()
        for cp in cps:
            cp.wait_send()

    return pl.pallas_call(
        body, name=name, in_specs=[ANY] * n, out_specs=[ANY] * n,
        out_shape=[jax.ShapeDtypeStruct((N_CHIPS, g.shape[1] // 2, g.shape[2]), g.dtype) for g in grads],
        scratch_shapes=[pltpu.SemaphoreType.DMA((n,)), pltpu.SemaphoreType.DMA((n,))],
        compiler_params=pltpu.CompilerParams(has_side_effects=True),
    )(*grads)


def _row_tile(rows, cols, itemsize, n_bufs):
    for tr in (2048, 1024, 512, 256, 128, 64, 32, 16, 8):
        if rows % tr == 0 and 2 * n_bufs * tr * cols * itemsize <= VMEM_BUDGET // 2:
            return tr
    return rows


def _rs_pair_sum(name, g, buf, core):
    _, R, C = g.shape
    half = R // 2
    tr = _row_tile(half, C, 4, 3)
    nhb = half // tr

    def body(c_ref, g_ref, b_ref, o_ref):
        o_ref[...] = (g_ref[...].astype(F32) + b_ref[...].astype(F32)).astype(o_ref.dtype)

    return pl.pallas_call(
        body, name=name,
        grid_spec=pltpu.PrefetchScalarGridSpec(
            num_scalar_prefetch=1, grid=(N_CHIPS, nhb),
            in_specs=[pl.BlockSpec((1, tr, C), lambda s, r, c_ref: (s, c_ref[0] * nhb + r, 0)),
                      pl.BlockSpec((1, tr, C), lambda s, r, c_ref: (s, r, 0))],
            out_specs=pl.BlockSpec((1, tr, C), lambda s, r, c_ref: (s, r, 0))),
        out_shape=jax.ShapeDtypeStruct((N_CHIPS, half, C), BF16),
        compiler_params=_cp(("arbitrary", "arbitrary")),
    )(core, g, buf)


def _rs_all_to_all(name, parts):
    n = len(parts)

    def body(*refs):
        ins, bufs = refs[:n], refs[n:2 * n]
        send, recv, loc = refs[2 * n:]
        x, y, c = _coords()
        s_me = 2 * x + y
        chips = _other_chips(x, y)
        cps, lcs = [], []
        for i in range(n):
            lc = pltpu.make_async_copy(ins[i].at[s_me], bufs[i].at[s_me], loc.at[i])
            lc.start()
            lcs.append(lc)
            for j, (px, py) in enumerate(chips):
                cp = _rcopy(ins[i].at[2 * px + py], bufs[i].at[s_me], send.at[i, j], recv.at[i, j], (px, py, c))
                cp.start()
                cps.append(cp)
        for i in range(n):
            for j, (px, py) in enumerate(chips):
                land = bufs[i].at[2 * px + py]
                _rcopy(land, land, send.at[i, j], recv.at[i, j], (px, py, c)).wait_recv()
        for cp in cps:
            cp.wait_send()
        for lc in lcs:
            lc.wait()

    return pl.pallas_call(
        body, name=name, in_specs=[ANY] * n, out_specs=[ANY] * n,
        out_shape=[jax.ShapeDtypeStruct(p.shape, p.dtype) for p in parts],
        scratch_shapes=[pltpu.SemaphoreType.DMA((n, 3)), pltpu.SemaphoreType.DMA((n, 3)), pltpu.SemaphoreType.DMA((n,))],
        compiler_params=pltpu.CompilerParams(has_side_effects=True),
    )(*parts)


def _rs_sum4(name, buf, core):
    _, half, C = buf.shape
    tr = _row_tile(half, C, 4, 3)
    nhb = half // tr

    def body(c_ref, b_ref, o_ref):
        acc = b_ref[0].astype(F32)
        for s in range(1, N_CHIPS):
            acc = acc + b_ref[s].astype(F32)
        o_ref[...] = acc

    return pl.pallas_call(
        body, name=name,
        grid_spec=pltpu.PrefetchScalarGridSpec(
            num_scalar_prefetch=1, grid=(nhb,),
            in_specs=[pl.BlockSpec((N_CHIPS, tr, C), lambda r, c_ref: (0, r, 0))],
            out_specs=pl.BlockSpec((tr, C), lambda r, c_ref: (c_ref[0] * nhb + r, 0))),
        out_shape=jax.ShapeDtypeStruct((2 * half, C), F32),
        compiler_params=_cp(("arbitrary",)),
    )(core, buf)


def _rs_share_halves(name, outs):
    n = len(outs)

    def body(*refs):
        o = refs[n:2 * n]
        send, recv = refs[2 * n:]
        x, y, c = _coords()
        cps = []
        for i in range(n):
            half = outs[i].shape[0] // 2
            mine = o[i].at[pl.ds(c * half, half)]
            cp = _rcopy(mine, mine, send.at[i], recv.at[i], (x, y, 1 - c))
            cp.start()
            cps.append(cp)
        for i in range(n):
            half = outs[i].shape[0] // 2
            land = o[i].at[pl.ds((1 - c) * half, half)]
            _rcopy(land, land, send.at[i], recv.at[i], (x, y, 1 - c)).wait_recv()
        for cp in cps:
            cp.wait_send()

    return pl.pallas_call(
        body, name=name, in_specs=[ANY] * n, out_specs=[ANY] * n,
        out_shape=[jax.ShapeDtypeStruct(a.shape, a.dtype) for a in outs],
        scratch_shapes=[pltpu.SemaphoreType.DMA((n,)), pltpu.SemaphoreType.DMA((n,))],
        input_output_aliases={i: i for i in range(n)},
        compiler_params=pltpu.CompilerParams(has_side_effects=True),
    )(*outs)


def _reduce_scatter(tag, grads, core):
    bufs = _rs_pair_exchange(f"rs_pair_{tag}", grads)
    parts = [_rs_pair_sum(f"rs_pairsum_{tag}_{i}", g, b, core) for i, (g, b) in enumerate(zip(grads, bufs))]
    recv = _rs_all_to_all(f"rs_a2a_{tag}", parts)
    outs = [_rs_sum4(f"rs_sum4_{tag}_{i}", r, core) for i, r in enumerate(recv)]
    return _rs_share_halves(f"rs_share_{tag}", outs)


def _allreduce_small(name, v):
    rows = v.shape[0]

    def body(v_ref, o_ref, gath, send, recv):
        x, y, c = _coords()
        me = 4 * x + 2 * y + c

        def peer(kk):
            return (1 - x if kk & 4 else x, 1 - y if kk & 2 else y, 1 - c if kk & 1 else c)

        cps = []
        for kk in range(1, N_DEV):
            cp = _rcopy(v_ref, gath.at[me], send.at[kk - 1], recv.at[kk - 1], peer(kk))
            cp.start()
            cps.append(cp)
        gath[me] = v_ref[...]
        for kk in range(1, N_DEV):
            px, py, pc = peer(kk)
            land = gath.at[4 * px + 2 * py + pc]
            _rcopy(land, land, send.at[kk - 1], recv.at[kk - 1], (px, py, pc)).wait_recv()
        for cp in cps:
            cp.wait_send()
        acc = gath[0]
        for d in range(1, N_DEV):
            acc = acc + gath[d]
        o_ref[...] = acc

    return pl.pallas_call(
        body, name=name,
        in_specs=[pl.BlockSpec(memory_space=pltpu.VMEM)], out_specs=pl.BlockSpec(memory_space=pltpu.VMEM),
        out_shape=jax.ShapeDtypeStruct(v.shape, F32),
        scratch_shapes=[pltpu.VMEM((N_DEV, rows, LANES), F32), pltpu.SemaphoreType.DMA((N_DEV - 1,)), pltpu.SemaphoreType.DMA((N_DEV - 1,))],
        compiler_params=pltpu.CompilerParams(has_side_effects=True, vmem_limit_bytes=VMEM_LIMIT),
    )(v)


def _adamw(name, w, g, m, v):
    rows, cols = w.shape
    tr = _row_tile(rows, cols, 4, 7)

    def body(w_ref, g_ref, m_ref, v_ref, d_ref, nm_ref, nv_ref):
        gg = g_ref[...]
        nm = ADAM_B1 * m_ref[...] + (1.0 - ADAM_B1) * gg
        nv = ADAM_B2 * v_ref[...] + (1.0 - ADAM_B2) * (gg * gg)
        m_hat = nm / (1.0 - ADAM_B1 ** ADAM_STEP)
        v_hat = nv / (1.0 - ADAM_B2 ** ADAM_STEP)
        d_ref[...] = -ADAM_LR * (m_hat / (jnp.sqrt(v_hat) + ADAM_EPS) + ADAM_WD * w_ref[...])
        nm_ref[...] = nm
        nv_ref[...] = nv

    spec = pl.BlockSpec((tr, cols), lambda i: (i, 0))
    shp = jax.ShapeDtypeStruct((rows, cols), F32)
    return pl.pallas_call(
        body, name=name, grid=(rows // tr,), in_specs=[spec] * 4, out_specs=[spec] * 3, out_shape=[shp] * 3,
        compiler_params=_cp(("parallel",)),
    )(w, g, m, v)


def _as2d(a):
    if a.ndim == 1:
        return a.reshape(1, a.shape[0])
    return a.reshape(-1, a.shape[-1])


_WEIGHTS = ['meta_tokens', 'norm_w', 'ffn_w_gate', 'ffn_w_up', 'ffn_w_down', 'mlstm_w_in', 'mlstm_b_if', 'mlstm_norm_w',
            'mlstm_w_out', 'pool_w', 'pool_scale', 'gdn_w_in', 'gdn_conv_w', 'gdn_a_log', 'gdn_dt_bias', 'gdn_norm_w',
            'gdn_w_out', 'swa_w_qkv', 'swa_b_qkv', 'swa_sinks', 'swa_w_out', 'swa_b_out', 'final_norm_w']
_SMALL = [('meta_tokens', True), ('norm_w', True), ('pool_scale', True), ('gdn_conv_w', True), ('swa_b_qkv', True),
          ('swa_b_out', True), ('mlstm_b_if', False), ('mlstm_norm_w', False), ('gdn_a_log', False),
          ('gdn_dt_bias', False), ('gdn_norm_w', False), ('swa_sinks', False), ('final_norm_w', False)]


def _pack(vals):
    flat = jnp.concatenate([v.reshape(-1).astype(F32) for v in vals])
    n = _round_up(flat.shape[0], 8 * LANES)
    return jnp.pad(flat, (0, n - flat.shape[0])).reshape(n // LANES, LANES)


def _unpack(packed, shapes):
    flat = packed.reshape(-1)
    out, off = [], 0
    for s in shapes:
        n = int(np.prod(s))
        out.append(flat[off:off + n].reshape(s))
        off += n
    return out


def _to_chunks(a, n_meta, seq):
    pad = jnp.zeros((CHUNK - n_meta,) + a.shape[1:], a.dtype)
    return jnp.concatenate([a[:n_meta], pad, a[n_meta:n_meta + seq]], axis=0)


def _from_chunks(a, n_meta, seq, tp):
    pad = jnp.zeros((tp - n_meta - seq,) + a.shape[1:], a.dtype)
    return jnp.concatenate([a[:n_meta], a[CHUNK:CHUNK + seq], pad], axis=0)


def _col_row(g, heads):
    t = g.T
    return t[:, :, None], t.reshape(heads, -1, 1, CHUNK)


def _from_col_row(dc, dr):
    heads = dc.shape[0]
    return (dc[:, :, 0] + dr.reshape(heads, -1)).T


def kernel(x, meta_tokens, norm_w, ffn_w_gate, ffn_w_up, ffn_w_down, mlstm_w_in, mlstm_b_if, mlstm_norm_w, mlstm_w_out, pool_w, pool_scale, gdn_w_in, gdn_conv_w, gdn_a_log, gdn_dt_bias, gdn_norm_w, gdn_w_out, swa_w_qkv, swa_b_qkv, swa_sinks, swa_w_out, swa_b_out, final_norm_w, loss_target, m_meta_tokens, m_norm_w, m_ffn_w_gate, m_ffn_w_up, m_ffn_w_down, m_mlstm_w_in, m_mlstm_b_if, m_mlstm_norm_w, m_mlstm_w_out, m_pool_w, m_pool_scale, m_gdn_w_in, m_gdn_conv_w, m_gdn_a_log, m_gdn_dt_bias, m_gdn_norm_w, m_gdn_w_out, m_swa_w_qkv, m_swa_b_qkv, m_swa_sinks, m_swa_w_out, m_swa_b_out, m_final_norm_w, v_meta_tokens, v_norm_w, v_ffn_w_gate, v_ffn_w_up, v_ffn_w_down, v_mlstm_w_in, v_mlstm_b_if, v_mlstm_norm_w, v_mlstm_w_out, v_pool_w, v_pool_scale, v_gdn_w_in, v_gdn_conv_w, v_gdn_a_log, v_gdn_dt_bias, v_gdn_norm_w, v_gdn_w_out, v_swa_w_qkv, v_swa_b_qkv, v_swa_sinks, v_swa_w_out, v_swa_b_out, v_final_norm_w):
    args = locals()
    W = {n: args[n] for n in _WEIGHTS}
    M1 = {n: args["m_" + n] for n in _WEIGHTS}
    V2 = {n: args["v_" + n] for n in _WEIGHTS}

    SEQ, D = x.shape[1], x.shape[2]
    NM = meta_tokens.shape[0]
    T = NM + SEQ
    TP = _round_up(T, ROW_BLOCK)
    DEPTH = ffn_w_gate.shape[0]
    FFS = ffn_w_gate.shape[3]
    ML_H = mlstm_b_if.shape[1] // 2
    ML_DV = D // ML_H
    ML_DK = ML_DV // 2
    ML_IN = 2 * ML_H * ML_DK + 2 * D + 2 * ML_H
    ML_INP = _pad_cols(ML_IN)
    GD_DK = gdn_norm_w.shape[1]
    GD_VH = gdn_a_log.shape[1]
    GD_QH = GD_VH // 2
    GD_QKW = GD_QH * GD_DK
    GD_VW = GD_VH * GD_DK
    GD_CC = 2 * GD_QKW + GD_VW
    GD_IN = GD_CC + GD_VW + 2 * GD_VH
    GD_INP = _pad_cols(GD_IN)
    SW_HQ = swa_sinks.shape[1]
    SW_DH = D // SW_HQ
    SW_HKV = SW_HQ // SWA_GROUP
    SW_KVW = SW_HKV * SW_DH
    SW_IN = D + 2 * SW_KVW
    n_pool = len(POOL_WINDOWS)
    PG = D // n_pool

    cx, cy, cc = _coords()
    s_me = 2 * cx + cy
    core = cc.astype(jnp.int32).reshape(1)

    def my_cols(full, width):
        return lax.dynamic_slice_in_dim(full, s_me * width, width, axis=full.ndim - 1)

    big_names = ['ffn_w_gate', 'ffn_w_up', 'ffn_w_down', 'mlstm_w_in', 'mlstm_w_out', 'pool_w', 'gdn_w_in', 'gdn_w_out',
                 'swa_w_qkv', 'swa_w_out']
    big_shards = {
        'ffn_w_gate': ffn_w_gate, 'ffn_w_up': ffn_w_up, 'ffn_w_down': ffn_w_down,
        'mlstm_w_in': mlstm_w_in[0], 'mlstm_w_out': mlstm_w_out[0], 'pool_w': pool_w[0].reshape(n_pool * (PG // N_CHIPS), PG),
        'gdn_w_in': gdn_w_in[0], 'gdn_w_out': gdn_w_out[0], 'swa_w_qkv': swa_w_qkv[0], 'swa_w_out': swa_w_out[0]}
    gathered = _allgather("allgather_weights", [big_shards[n].astype(BF16) for n in big_names], True)
    G = dict(zip(big_names, gathered))
    wg_full, wu_full, wd_full = G['ffn_w_gate'], G['ffn_w_up'], G['ffn_w_down']

    def cols_full(g, pad_to=None):
        k = g.shape[1]
        full = jnp.transpose(g, (1, 0, 2)).reshape(k, -1)
        if pad_to is not None and pad_to > full.shape[1]:
            full = jnp.pad(full, ((0, 0), (0, pad_to - full.shape[1])))
        return full

    def rows_full(g):
        return g.reshape(-1, g.shape[2])

    ml_win = cols_full(G['mlstm_w_in'], ML_INP)
    ml_wout = rows_full(G['mlstm_w_out'])
    pool_full = jnp.transpose(G['pool_w'].reshape(N_CHIPS, n_pool, PG // N_CHIPS, PG), (1, 0, 2, 3)).reshape(n_pool, PG, PG)
    gd_win = cols_full(G['gdn_w_in'], GD_INP)
    gd_wout = rows_full(G['gdn_w_out'])
    sw_wqkv = cols_full(G['swa_w_qkv'])
    sw_wout = rows_full(G['swa_w_out'])

    small_sharded = [n for n, sh in _SMALL if sh]
    sm_shapes = [W[n].shape for n in small_sharded]
    sm_gath = _allgather("allgather_small", [_pack([W[n] for n in small_sharded])], False)[0]
    sm_parts = [_unpack(sm_gath[s], sm_shapes) for s in range(N_CHIPS)]
    SF = {n: jnp.concatenate([sm_parts[s][i] for s in range(N_CHIPS)], axis=-1) for i, n in enumerate(small_sharded)}
    meta_full, normw_full = SF['meta_tokens'], SF['norm_w']
    pool_scale_full, conv_full = SF['pool_scale'], SF['gdn_conv_w'][0]
    bqkv_full, bout_full = SF['swa_b_qkv'], SF['swa_b_out']

    nps = 1
    ff_tm = _tiles(TP, FFS, D, 2 * 4 + 3 * 2 * 2, n_pairs=2)[0]

    def ffn_fwd(h, li, wi, nw):
        n = _rmsnorm_fwd(f"ffn_norm_{li}_{wi}", h, nw, BF16)
        tm, tn, tk = ff_tm, FFS, _div(D, (512, 256, 128))
        bspec = lambda j, k: (j, li, wi, k, 0)
        g, u, a = _matmul(
            f"ffn_gateup_{li}_{wi}", "nn", (TP // tm, N_CHIPS, D // tk),
            [(n, (tm, tk), lambda i, j, k: (i, k))] * 2,
            [(wg_full, (None, None, None, tk, tn), lambda i, j, k: bspec(j, k)),
             (wu_full, (None, None, None, tk, tn), lambda i, j, k: bspec(j, k))],
            [0, 1], 2, [],
            [(jax.ShapeDtypeStruct((TP, N_CHIPS * FFS), BF16), (tm, tn), lambda i, j, k: (i, j))] * 3,
            lambda accs, ex: [accs[0], accs[1], _silu(accs[0]) * accs[1]], tm, tn)
        tm2, tn2, tk2 = _tiles(TP, D, FFS, 4 + 2 * 4 + 2 * 4, k_cands=(FFS,), n_cands=(1024, 512, 256, 128))
        kps = FFS // tk2
        h2 = _matmul(
            f"ffn_down_{li}_{wi}", "nn", (TP // tm2, D // tn2, N_CHIPS * kps),
            [(a, (tm2, tk2), lambda i, j, k: (i, k))],
            [(wd_full, (None, None, None, tk2, tn2), lambda i, j, k: (k // kps, li, wi, k % kps, j))],
            [0], 1, [(h, (tm2, tn2), lambda i, j, k: (i, j))],
            [(jax.ShapeDtypeStruct((TP, D), F32), (tm2, tn2), lambda i, j, k: (i, j))],
            lambda accs, ex: [ex[0] + 0.5 * accs[0]], tm2, tn2)[0]
        return h2, (h, n, g, u, a)

    def ffn_bwd(dh2, saved, li, wi, nw, gbufs):
        h, n, g, u, a = saved
        gg, gu, gd = gbufs
        slot = li * 2 + wi
        tm, tn = ff_tm, FFS
        tk = _div(D, (512, 256, 128))

        def epi(accs, ex):
            gb, ub = ex[0].astype(F32), ex[1].astype(F32)
            da = 0.5 * accs[0]
            s = _sigmoid(gb)
            return [da * ub * (s * (1.0 + gb * (1.0 - s))), da * (gb * s)]
        dg, du = _matmul(
            f"ffn_dact_{li}_{wi}", "nt", (TP // tm, N_CHIPS, D // tk),
            [(dh2, (tm, tk), lambda i, j, k: (i, k))],
            [(wd_full, (None, None, None, tn, tk), lambda i, j, k: (j, li, wi, 0, k))],
            [0], 1, [(g, (tm, tn), lambda i, j, k: (i, j)), (u, (tm, tn), lambda i, j, k: (i, j))],
            [(jax.ShapeDtypeStruct((TP, N_CHIPS * FFS), BF16), (tm, tn), lambda i, j, k: (i, j))] * 2, epi, tm, tn)
        tkr = _div(TP, (1408, 1056, 704, 384, 256, 128))
        tnd = _div(D, (1024, 512, 256, 128))
        gd = _matmul(
            f"ffn_dwd_{li}_{wi}", "tn", (N_CHIPS, D // tnd, TP // tkr),
            [(a, (tkr, FFS), lambda i, j, k: (k, i))], [(dh2, (tkr, tnd), lambda i, j, k: (k, j))],
            [0], 1, [], [(jax.ShapeDtypeStruct(gd.shape, BF16), (None, None, FFS, tnd), lambda i, j, k: (i, slot, 0, j))],
            lambda accs, ex: [0.5 * accs[0]], FFS, tnd, alias_inputs=[gd], alias_map={0: 0})[0]
        tmw = _div(D, (512, 256, 128))
        gg, gu = _matmul(
            f"ffn_dwgu_{li}_{wi}", "tn", (D // tmw, N_CHIPS, TP // tkr),
            [(n, (tkr, tmw), lambda i, j, k: (k, i))] * 2,
            [(dg, (tkr, FFS), lambda i, j, k: (k, j)), (du, (tkr, FFS), lambda i, j, k: (k, j))],
            [0, 1], 2, [],
            [(jax.ShapeDtypeStruct(gg.shape, BF16), (None, None, tmw, FFS), lambda i, j, k: (j, slot, i, 0))] * 2,
            lambda accs, ex: [accs[0], accs[1]], tmw, FFS, alias_inputs=[gg, gu], alias_map={0: 0, 1: 1})
        tm3 = _div(TP, (704, 528, 384, 256, 128))
        tn3 = _div(D, (1024, 512, 256, 128))
        dn = _matmul(
            f"ffn_dn_{li}_{wi}", "nt", (TP // tm3, D // tn3, N_CHIPS),
            [(dg, (tm3, FFS), lambda i, j, k: (i, k)), (du, (tm3, FFS), lambda i, j, k: (i, k))],
            [(wg_full, (None, None, None, tn3, FFS), lambda i, j, k: (k, li, wi, j, 0)),
             (wu_full, (None, None, None, tn3, FFS), lambda i, j, k: (k, li, wi, j, 0))],
            [0, 0], 1, [], [(jax.ShapeDtypeStruct((TP, D), F32), (tm3, tn3), lambda i, j, k: (i, j))],
            lambda accs, ex: [accs[0]], tm3, tn3)[0]
        dh, dnw = _rmsnorm_bwd(f"ffn_norm_bwd_{li}_{wi}", h, nw, dn, dh2)
        return dh, dnw, (gg, gu, gd)

    def mlstm_fwd(h, nw):
        u = _rmsnorm_fwd("mlstm_norm", h, nw, BF16)
        p = _mm("mlstm_in", "nn", u, ml_win, F32)
        pc = _to_chunks(p, NM, SEQ)
        qkw = ML_H * ML_DK
        gates = pc[:, 2 * qkw + 2 * D:2 * qkw + 2 * D + 2 * ML_H] + mlstm_b_if
        li_c, li_r = _col_row(gates[:, :ML_H], ML_H)
        lf_c, lf_r = _col_row(gates[:, ML_H:], ML_H)
        hh, cs, ns, ms = _mlstm_core_fwd(pc, li_c, lf_c, li_r, lf_r, ML_H, ML_DK, ML_DV, NM)
        hh_s = _from_chunks(hh, NM, SEQ, TP)
        og_cb = (2 * qkw + D) // D
        out = _headnorm_fwd("mlstm_post", hh_s, p, D, og_cb, mlstm_norm_w, ML_DV, _sigmoid)
        h2 = _mm("mlstm_out", "nn", out, ml_wout, F32, lambda acc, hb: hb + acc, [h], ["tile"])
        return h2, (h, u, p, pc, (li_c, lf_c, li_r, lf_r), (cs, ns, ms), hh_s, out, og_cb)

    def mlstm_bwd(dh2, saved, nw):
        h, u, p, pc, gts, sts, hh_s, out, og_cb = saved
        dout = _mm("mlstm_out_dx", "nt", dh2, ml_wout, F32)
        d_wout = _mm("mlstm_out_dw", "tn", out, dh2, BF16)
        dhh, dog, dnormw = _headnorm_bwd("mlstm_post_bwd", hh_s, p, D, og_cb, mlstm_norm_w, dout, ML_DV, _sigmoid)
        dq, dkk, dvv, d0, d1, d2, d3 = _mlstm_core_bwd(pc, *gts, *sts, _to_chunks(dhh, NM, SEQ), ML_H, ML_DK, ML_DV, NM)
        dgates = jnp.concatenate([_from_col_row(d0, d2), _from_col_row(d1, d3)], axis=1)
        dqkvg = _from_chunks(jnp.concatenate([dq, dkk, dvv], axis=1), NM, SEQ, TP)
        dgs = _from_chunks(dgates, NM, SEQ, TP)
        pad = jnp.zeros((TP, ML_INP - ML_IN), F32)
        dp = jnp.concatenate([dqkvg, dog, dgs, pad], axis=1)
        d_bif = _colsum("mlstm_dbias", jnp.pad(dgs, ((0, 0), (0, LANES - 2 * ML_H))))[:, :2 * ML_H]
        d_win = _mm("mlstm_in_dw", "tn", u, dp, BF16)[:, :ML_IN]
        du = _mm("mlstm_in_dx", "nt", dp, ml_win, F32)
        dh, dnw = _rmsnorm_bwd("mlstm_norm_bwd", h, nw, du, dh2)
        return dh, dnw, {'mlstm_w_in': d_win, 'mlstm_w_out': d_wout, 'mlstm_b_if': d_bif, 'mlstm_norm_w': dnormw}

    def pool_fwd_layer(h, nw):
        u = _rmsnorm_fwd("pool_norm", h, nw, F32)
        pooled = _pool_fwd(u)
        tm = _div(TP, (704, 528, 384, 256, 128))
        tk = _div(PG, (512, 256, 128))
        kpg = PG // tk
        h2, ypre = _matmul(
            "pool_mix", "nn", (TP // tm, n_pool, kpg),
            [(pooled, (tm, tk), lambda i, j, k: (i, j * kpg + k))],
            [(pool_full, (None, tk, PG), lambda i, j, k: (j, k, 0))],
            [0], 1, [(h, (tm, PG), lambda i, j, k: (i, j)), (pool_scale_full, (1, PG), lambda i, j, k: (0, j))],
            [(jax.ShapeDtypeStruct((TP, D), F32), (tm, PG), lambda i, j, k: (i, j))] * 2,
            lambda accs, ex: [ex[0] + accs[0] * ex[1], accs[0]], tm, PG)
        return h2, (h, pooled, ypre)

    def pool_bwd_layer(dh2, saved, nw):
        h, pooled, ypre = saved

        def fn(_, dyb, ypb, sb):
            return [dyb * sb, jnp.sum(dyb * ypb, axis=0, keepdims=True)]
        dys, dscale = _rowwise("pool_scale_bwd", fn, [_full(dh2), _full(ypre), _full(pool_scale_full)],
                               [(D, BF16, "row"), (D, F32, "acc")], ROW_BLOCK, TP)
        tm = _div(TP, (704, 528, 384, 256, 128))
        tk = _div(PG, (512, 256, 128))
        kpg = PG // tk
        dpooled = _matmul(
            "pool_mix_dx", "nt", (TP // tm, n_pool, kpg),
            [(dys, (tm, tk), lambda i, j, k: (i, j * kpg + k))],
            [(pool_full, (None, PG, tk), lambda i, j, k: (j, 0, k))],
            [0], 1, [], [(jax.ShapeDtypeStruct((TP, D), F32), (tm, PG), lambda i, j, k: (i, j))],
            lambda accs, ex: [accs[0]], tm, PG)[0]
        tkr = _div(TP, (1408, 1056, 704, 384, 256, 128))
        d_pw = _matmul(
            "pool_mix_dw", "tn", (1, n_pool, TP // tkr),
            [(pooled, (tkr, PG), lambda i, j, k: (k, j))], [(dys, (tkr, PG), lambda i, j, k: (k, j))],
            [0], 1, [], [(jax.ShapeDtypeStruct((n_pool, PG, PG), BF16), (None, PG, PG), lambda i, j, k: (j, 0, 0))],
            lambda accs, ex: [accs[0]], PG, PG)[0]
        du = _pool_bwd(dpooled)
        dh, dnw = _rmsnorm_bwd("pool_norm_bwd", h, nw, du, dh2)
        return dh, dnw, {'pool_w': d_pw, 'pool_scale': dscale}

    def gdn_fwd(h, nw):
        u = _rmsnorm_fwd("gdn_norm", h, nw, BF16)
        p = _mm("gdn_in", "nn", u, gd_win, F32)
        qkv_act = _conv_fwd(p, conv_full, GD_CC)
        b_pre = p[:, GD_CC + GD_VW:GD_CC + GD_VW + GD_VH]
        a_pre = p[:, GD_CC + GD_VW + GD_VH:GD_IN]
        g, beta = _gdn_gates_fwd(a_pre, b_pre, gdn_a_log, gdn_dt_bias)
        qkv_c = _to_chunks(qkv_act, NM, SEQ)
        g_c, g_r = _col_row(_to_chunks(g, NM, SEQ), GD_VH)
        b_c, _ = _col_row(_to_chunks(beta, NM, SEQ), GD_VH)
        o, st = _gdn_core_fwd(qkv_c, g_c, b_c, g_r, GD_QH, GD_DK, NM)
        o_s = _from_chunks(o, NM, SEQ, TP)
        nw_t = jnp.tile(gdn_norm_w, (1, GD_VH))
        z_cb = GD_CC // GD_VW
        out = _headnorm_fwd("gdn_post", o_s, p, GD_VW, z_cb, nw_t, GD_DK, _silu)
        h2 = _mm("gdn_out", "nn", out, gd_wout, F32, lambda acc, hb: hb + acc, [h], ["tile"])
        return h2, (h, u, p, qkv_c, (g_c, b_c, g_r), st, o_s, out, nw_t, z_cb, a_pre, b_pre)

    def gdn_bwd(dh2, saved, nw):
        h, u, p, qkv_c, gts, st, o_s, out, nw_t, z_cb, a_pre, b_pre = saved
        dout = _mm("gdn_out_dx", "nt", dh2, gd_wout, F32)
        d_wout = _mm("gdn_out_dw", "tn", out, dh2, BF16)
        do, dz, dnw_t = _headnorm_bwd("gdn_post_bwd", o_s, p, GD_VW, z_cb, nw_t, dout, GD_DK, _silu)
        dnormw = jnp.sum(dnw_t.reshape(GD_VH, GD_DK), axis=0, keepdims=True)
        res = _gdn_core_bwd(qkv_c, *gts, st, _to_chunks(do, NM, SEQ), GD_QH, GD_DK, NM)
        dq, dkk, dvv = res[0], res[1], res[2]
        pairs = lambda a, b: jnp.stack([a, b], axis=1).reshape((GD_VH,) + a.shape[1:])
        dgc, dbc, dgr = pairs(res[3], res[4]), pairs(res[5], res[6]), pairs(res[7], res[8])
        dg = _from_chunks(_from_col_row(dgc, dgr), NM, SEQ, TP)
        dbeta = _from_chunks(dbc[:, :, 0].T, NM, SEQ, TP)
        da_pre, db_pre, d_alog, d_dt = _gdn_gates_bwd(a_pre, b_pre, gdn_a_log, gdn_dt_bias, dg, dbeta)
        dact = _from_chunks(jnp.concatenate([dq, dkk, dvv], axis=1), NM, SEQ, TP)
        dacc, d_conv = _conv_bwd_pre(p, conv_full, dact, GD_CC)
        dqkv_pre = _conv_bwd_dx(dacc, conv_full, GD_CC)
        pad = jnp.zeros((TP, GD_INP - GD_IN), F32)
        dp = jnp.concatenate([dqkv_pre, dz, db_pre, da_pre, pad], axis=1)
        d_win = _mm("gdn_in_dw", "tn", u, dp, BF16)[:, :GD_IN]
        du = _mm("gdn_in_dx", "nt", dp, gd_win, F32)
        dh, dnw = _rmsnorm_bwd("gdn_norm_bwd", h, nw, du, dh2)
        return dh, dnw, {'gdn_w_in': d_win, 'gdn_w_out': d_wout, 'gdn_conv_w': d_conv, 'gdn_a_log': d_alog,
                         'gdn_dt_bias': d_dt, 'gdn_norm_w': dnormw}

    inv = ROPE_THETA ** (-jnp.arange(0, SW_DH, 2, dtype=F32) / SW_DH)
    ang = jnp.arange(TP, dtype=F32)[:, None] * inv[None, :]
    ang = jnp.concatenate([ang, ang], axis=-1)
    rope_cos, rope_sin = jnp.cos(ang), jnp.sin(ang)

    def swa_split(p):
        q = jnp.transpose(p[:, :D].reshape(TP, SW_HKV, SWA_GROUP, SW_DH), (1, 2, 0, 3))
        k = jnp.transpose(p[:, D:D + SW_KVW].reshape(TP, SW_HKV, SW_DH), (1, 0, 2))
        v = jnp.transpose(p[:, D + SW_KVW:].reshape(TP, SW_HKV, SW_DH), (1, 0, 2))
        return q, k, v

    def swa_fwd(h, nw):
        u = _rmsnorm_fwd("swa_norm", h, nw, BF16)
        p = _mm("swa_in", "nn", u, sw_wqkv, F32, lambda acc, bb: acc + bb, [bqkv_full], ["row"])
        q, k, v = swa_split(p)
        sink = jnp.repeat(swa_sinks.reshape(SW_HKV, SWA_GROUP), ROW_BLOCK, axis=1)[:, :, None]
        o = _swa_core(q, k, v, rope_cos, rope_sin, sink, T)[0]
        o2 = jnp.transpose(o, (2, 0, 1, 3)).reshape(TP, D).astype(BF16)
        h2 = _mm("swa_out", "nn", o2, sw_wout, F32, lambda acc, hb, bb: hb + acc + bb, [h, bout_full], ["tile", "row"])
        return h2, (h, u, q, k, v, sink, o2)

    def swa_bwd(dh2, saved, nw):
        h, u, q, k, v, sink, o2 = saved
        do = _mm("swa_out_dx", "nt", dh2, sw_wout, F32)
        d_wout = _mm("swa_out_dw", "tn", o2, dh2, BF16)
        d_bout = _colsum("swa_dbout", dh2)
        do4 = jnp.transpose(do.reshape(TP, SW_HKV, SWA_GROUP, SW_DH), (1, 2, 0, 3))
        dq, dkp, dkc, dvp, dvc, dsink = _swa_core(q, k, v, rope_cos, rope_sin, sink, T, do=do4)
        shift = lambda a: jnp.concatenate([a[:, ROW_BLOCK:], jnp.zeros_like(a[:, :ROW_BLOCK])], axis=1)
        dk = dkc + shift(dkp)
        dv = dvc + shift(dvp)
        dp = jnp.concatenate([jnp.transpose(dq, (2, 0, 1, 3)).reshape(TP, D),
                              jnp.transpose(dk, (1, 0, 2)).reshape(TP, SW_KVW),
                              jnp.transpose(dv, (1, 0, 2)).reshape(TP, SW_KVW)], axis=1)
        d_sinks = jnp.sum(dsink.reshape(SW_HKV, SWA_GROUP, ROW_BLOCK), axis=2).reshape(1, SW_HQ)
        d_bqkv = _colsum("swa_dbqkv", dp)
        d_wqkv = _mm("swa_in_dw", "tn", u, dp, BF16)
        du = _mm("swa_in_dx", "nt", dp, sw_wqkv, F32)
        dh, dnw = _rmsnorm_bwd("swa_norm_bwd", h, nw, du, dh2)
        return dh, dnw, {'swa_w_qkv': d_wqkv, 'swa_w_out': d_wout, 'swa_b_qkv': d_bqkv, 'swa_b_out': d_bout,
                         'swa_sinks': d_sinks}

    mixers_fwd = [mlstm_fwd, pool_fwd_layer, gdn_fwd, swa_fwd]
    mixers_bwd = [mlstm_bwd, pool_bwd_layer, gdn_bwd, swa_bwd]

    h = jnp.concatenate([meta_full, x[0], jnp.zeros((TP - T, D), F32)], axis=0)
    saved = []
    for li in range(DEPTH):
        nws = [normw_full[li, t].reshape(1, D) for t in range(3)]
        h, s0 = ffn_fwd(h, li, 0, nws[0])
        h, s1 = mixers_fwd[li % 4](h, nws[1])
        h, s2 = ffn_fwd(h, li, 1, nws[2])
        saved.append((s0, s1, s2, nws))

    tgt = jnp.concatenate([jnp.zeros((NM, D), F32), loss_target[0], jnp.zeros((TP - T, D), F32)], axis=0)
    dh, d_final_w, loss_vec = _loss_head("loss_head", h, final_norm_w.reshape(1, D), tgt, NM, SEQ)

    slots = DEPTH * 2
    gbufs = (lax.empty((N_CHIPS, slots, D, FFS), BF16), lax.empty((N_CHIPS, slots, D, FFS), BF16),
             lax.empty((N_CHIPS, slots, FFS, D), BF16))
    d_normw = [[None] * 3 for _ in range(DEPTH)]
    GR = {}
    for li in reversed(range(DEPTH)):
        s0, s1, s2, nws = saved[li]
        dh, d_normw[li][2], gbufs = ffn_bwd(dh, s2, li, 1, nws[2], gbufs)
        dh, d_normw[li][1], gm = mixers_bwd[li % 4](dh, s1, nws[1])
        GR.update(gm)
        dh, d_normw[li][0], gbufs = ffn_bwd(dh, s0, li, 0, nws[0], gbufs)
    grad_x = dh[NM:NM + SEQ][None]
    GR['meta_tokens'] = dh[:NM]
    GR['norm_w'] = jnp.stack([jnp.concatenate(r, axis=0) for r in d_normw], axis=0)
    GR['final_norm_w'] = d_final_w

    def col_shards(g, w):
        return jnp.transpose(g.reshape(g.shape[0], N_CHIPS, w), (1, 0, 2))

    def row_shards(g):
        return g.reshape(N_CHIPS, -1, g.shape[1])

    big_grads = [
        gbufs[0].reshape(N_CHIPS, slots * D, FFS), gbufs[1].reshape(N_CHIPS, slots * D, FFS),
        gbufs[2].reshape(N_CHIPS, slots * FFS, D),
        col_shards(GR['mlstm_w_in'], ML_IN // N_CHIPS), row_shards(GR['mlstm_w_out']),
        jnp.transpose(GR['pool_w'].reshape(n_pool, N_CHIPS, PG // N_CHIPS, PG), (1, 0, 2, 3)).reshape(N_CHIPS, -1, PG),
        col_shards(GR['gdn_w_in'], GD_IN // N_CHIPS), row_shards(GR['gdn_w_out']),
        col_shards(GR['swa_w_qkv'], SW_IN // N_CHIPS), row_shards(GR['swa_w_out'])]
    summed = _reduce_scatter("w", big_grads, core)
    grads = {n: s.reshape(W[n].shape) for n, s in zip(big_names, summed)}

    small_names = [n for n, _ in _SMALL]
    small_full_shapes = [GR[n].shape for n in small_names]
    packed = _pack([GR[n] for n in small_names] + [loss_vec[:, :1]])
    red = _allreduce_small("allreduce_small", packed)
    parts = _unpack(red, small_full_shapes + [(1, 1)])
    loss = parts[-1].reshape(())
    for (n, sharded), full in zip(_SMALL, parts[:-1]):
        full = full.reshape(W[n].shape[:-1] + (-1,))
        grads[n] = my_cols(full, W[n].shape[-1]) if sharded else full

    delta, new_m, new_v = {}, {}, {}
    for n in _WEIGHTS:
        shp = W[n].shape
        d, nm, nv = _adamw(f"adamw_{n}", _as2d(W[n]), _as2d(grads[n]), _as2d(M1[n]), _as2d(V2[n]))
        delta[n], new_m[n], new_v[n] = d.reshape(shp), nm.reshape(shp), nv.reshape(shp)

    return (loss, grad_x, *[grads[n] for n in _WEIGHTS], *[delta[n] for n in _WEIGHTS],
            *[new_m[n] for n in _WEIGHTS], *[new_v[n] for n in _WEIGHTS])
```

```python
import functools
import math

import jax
import jax.numpy as jnp
import numpy as np
from jax import lax
from jax.experimental import pallas as pl
from jax.experimental.pallas import tpu as pltpu

F32 = jnp.float32
BF16 = jnp.bfloat16
MESH = pl.DeviceIdType.MESH

EPS = 1e-6
CHUNK = 64
ROW_BLOCK = 128
SWA_WINDOW = 128
SWA_GROUP = 8
POOL_WINDOWS = (2, 4, 8, 16)
GDN_CONV = 4
ROPE_THETA = 10000.0
NEG = -1e30
N_CHIPS = 4
N_DEV = 8
LANES = 128
WIDE_TILE = 896
VMEM_LIMIT = 56 * 1024 * 1024
VMEM_BUDGET = 36 * 1024 * 1024

ADAM_LR = 0.001
ADAM_B1 = 0.9
ADAM_B2 = 0.999
ADAM_EPS = 1e-08
ADAM_WD = 0.01
ADAM_STEP = 10

_NN = ((1,), (0,))
_NT = ((1,), (1,))
_TN = ((0,), (0,))


def _cp(dims=None):
    return pltpu.CompilerParams(dimension_semantics=dims, vmem_limit_bytes=VMEM_LIMIT)


def _round_up(n, m):
    return -(-n // m) * m


def _div(n, cands):
    for c in cands:
        if c <= n and n % c == 0:
            return c
    return n


def _pad_cols(n):
    return _round_up(n, WIDE_TILE) if n > 2048 else _round_up(n, LANES)


def _dg(a, b, dims, prec=None, batched=False):
    if batched:
        dims = (((dims[0][0] + 1,), (dims[1][0] + 1,)), ((0,), (0,)))
    else:
        dims = (dims, ((), ()))
    return lax.dot_general(a, b, dims, precision=prec, preferred_element_type=F32)


def _make_dots(cast, batched=False):
    prec = None if cast is not None else lax.Precision.HIGHEST
    c = (lambda t: t.astype(cast)) if cast is not None else (lambda t: t)
    rnn = lambda a, b: _dg(c(a), c(b), _NN, prec, batched)
    rnt = lambda a, b: _dg(c(a), c(b), _NT, prec, batched)
    rtn = lambda a, b: _dg(c(a), c(b), _TN, prec, batched)

    @jax.custom_vjp
    def nn(a, b):
        return rnn(a, b)
    nn.defvjp(lambda a, b: (rnn(a, b), (a, b)), lambda r, ct: (rnt(ct, r[1]), rtn(r[0], ct)))

    @jax.custom_vjp
    def nt(a, b):
        return rnt(a, b)
    nt.defvjp(lambda a, b: (rnt(a, b), (a, b)), lambda r, ct: (rnn(ct, r[1]), rtn(ct, r[0])))

    @jax.custom_vjp
    def tn(a, b):
        return rtn(a, b)
    tn.defvjp(lambda a, b: (rtn(a, b), (a, b)), lambda r, ct: (rnt(r[1], ct), rnn(r[0], ct)))
    return nn, nt, tn, rnn, rnt, rtn


_bnn, _bnt, _btn, _rbnn, _rbnt, _rbtn = _make_dots(BF16)
_hnn, _hnt, _htn, _rhnn, _rhnt, _rhtn = _make_dots(None)
_qnn, _qnt, _qtn, _rqnn, _rqnt, _rqtn = _make_dots(BF16, batched=True)


def _sigmoid(x):
    return 1.0 / (1.0 + jnp.exp(-x))


def _silu(x):
    return x * _sigmoid(x)


def _softplus(x):
    return jnp.maximum(x, 0.0) + jnp.log(1.0 + jnp.exp(-jnp.abs(x)))


def _log_sigmoid(x):
    return -_softplus(-x)


def _iota(shape, dim):
    return lax.broadcasted_iota(jnp.int32, shape, dim)


def _matmul(name, form, grid, a_ops, b_ops, acc_ids, n_acc, extras, outs, epilogue, tm, tn,
            alias_inputs=(), alias_map=None):
    na, nb, ne, nal, no = len(a_ops), len(b_ops), len(extras), len(alias_inputs), len(outs)
    nk = grid[2]
    dims = {"nn": _NN, "nt": _NT, "tn": _TN}[form]

    def body(*refs):
        a_refs = refs[:na]
        b_refs = refs[na:na + nb]
        e_refs = refs[na + nb:na + nb + ne]
        o_refs = refs[na + nb + ne + nal:na + nb + ne + nal + no]
        acc = refs[-1]
        k = pl.program_id(2)

        @pl.when(k == 0)
        def _():
            acc[...] = jnp.zeros_like(acc)

        for p in range(na):
            acc[acc_ids[p]] += _dg(a_refs[p][...].astype(BF16), b_refs[p][...].astype(BF16), dims)

        @pl.when(k == nk - 1)
        def _():
            res = epilogue([acc[i] for i in range(n_acc)], [e[...] for e in e_refs])
            for o, r in zip(o_refs, res):
                o[...] = r.astype(o.dtype)

    ops = list(a_ops) + list(b_ops) + list(extras)
    in_specs = [pl.BlockSpec(bs, im) for (_, bs, im) in ops] + [pl.BlockSpec(memory_space=pl.ANY)] * nal
    aliases = {}
    if alias_map:
        aliases = {len(ops) + i: o for i, o in alias_map.items()}
    res = pl.pallas_call(
        body, name=name, grid=grid,
        in_specs=in_specs,
        out_specs=[pl.BlockSpec(bs, im) for (_, bs, im) in outs],
        out_shape=[s for (s, _, _) in outs],
        scratch_shapes=[pltpu.VMEM((n_acc, tm, tn), F32)],
        input_output_aliases=aliases,
        compiler_params=_cp(("parallel", "parallel", "arbitrary")),
    )(*[o[0] for o in ops], *alias_inputs)
    return res


def _tiles(M, N, K, fixed_bytes_per_tm_tn, k_cands=(512, 384, 256, 128), n_cands=(2048, 1792, 1408, 1280, 1024, 896, 768, 640, 512, 384, 256, 128),
           m_cands=(1408, 1056, 704, 528, 384, 256, 128, 64, 32, 16, 8), a_bytes=2, b_bytes=2, n_pairs=1):
    tn = _div(N, n_cands)
    tk = _div(K, k_cands)
    for tm in m_cands:
        if tm > M or M % tm:
            continue
        est = tm * tn * fixed_bytes_per_tm_tn + n_pairs * 2 * (tm * tk * a_bytes + tk * tn * b_bytes)
        if est <= VMEM_BUDGET:
            return tm, tn, tk
    return _div(M, (8,)), tn, tk


def _mm(name, form, a, b, out_dtype, epilogue=None, extras=(), extra_kinds=(), n_out=1, out_dtypes=None):
    if form == "nn":
        (M, K), N = a.shape, b.shape[1]
    elif form == "nt":
        (M, K), N = a.shape, b.shape[0]
    else:
        (K, M), N = a.shape, b.shape[1]
    out_dtypes = out_dtypes or [out_dtype] * n_out
    per = 4 + sum(2 * jnp.dtype(d).itemsize for d in out_dtypes)
    per += sum(2 * e.dtype.itemsize for e, kd in zip(extras, extra_kinds) if kd == "tile")
    kc = (1408, 1056, 704, 512, 384, 256, 128) if form == "tn" else (896, 512, 384, 256, 128)
    mc = (1024, 896, 768, 640, 512, 384, 256, 128) if form == "tn" else (1408, 1056, 704, 528, 384, 256, 128, 64, 32, 16, 8)
    tm, tn, tk = _tiles(M, N, K, per, k_cands=kc, m_cands=mc, a_bytes=a.dtype.itemsize, b_bytes=b.dtype.itemsize)
    grid = (M // tm, N // tn, K // tk)
    if form == "nn":
        a_op = (a, (tm, tk), lambda i, j, k: (i, k))
        b_op = (b, (tk, tn), lambda i, j, k: (k, j))
    elif form == "nt":
        a_op = (a, (tm, tk), lambda i, j, k: (i, k))
        b_op = (b, (tn, tk), lambda i, j, k: (j, k))
    else:
        a_op = (a, (tk, tm), lambda i, j, k: (k, i))
        b_op = (b, (tk, tn), lambda i, j, k: (k, j))
    e_ops = []
    for e, kd in zip(extras, extra_kinds):
        if kd == "tile":
            e_ops.append((e, (tm, tn), lambda i, j, k: (i, j)))
        else:
            e_ops.append((e, (1, tn), lambda i, j, k: (0, j)))
    outs = [(jax.ShapeDtypeStruct((M, N), d), (tm, tn), lambda i, j, k: (i, j)) for d in out_dtypes]

    def epi(accs, ex):
        if epilogue is None:
            return [accs[0]]
        r = epilogue(accs[0], *ex)
        return list(r) if isinstance(r, (tuple, list)) else [r]

    res = _matmul(name, form, grid, [a_op], [b_op], [0], 1, e_ops, outs, epi, tm, tn)
    return res[0] if len(res) == 1 else res


def _rowwise(name, fn, ins, outs, tr, rows):
    n_in, n_out = len(ins), len(outs)
    nblk = rows // tr

    def body(*refs):
        i = pl.program_id(0)
        vals = fn(i * tr, *[r[...] for r in refs[:n_in]])
        for o, v, (_, _, kind) in zip(refs[n_in:], vals, outs):
            if kind == "row":
                o[...] = v.astype(o.dtype)
            else:
                @pl.when(i == 0)
                def _(o=o):
                    o[...] = jnp.zeros_like(o)
                o[...] += v.astype(o.dtype)

    in_specs = []
    for arr, w, cb in ins:
        if arr.shape[0] == 1 and rows != 1:
            in_specs.append(pl.BlockSpec((1, w), lambda i, cb=cb: (0, cb)))
        else:
            in_specs.append(pl.BlockSpec((tr, w), lambda i, cb=cb: (i, cb)))
    out_specs, out_shape = [], []
    for w, d, kind in outs:
        if kind == "row":
            out_specs.append(pl.BlockSpec((tr, w), lambda i: (i, 0)))
            out_shape.append(jax.ShapeDtypeStruct((rows, w), d))
        else:
            out_specs.append(pl.BlockSpec((1, w), lambda i: (0, 0)))
            out_shape.append(jax.ShapeDtypeStruct((1, w), d))
    return pl.pallas_call(
        body, name=name, grid=(nblk,), in_specs=in_specs, out_specs=out_specs, out_shape=out_shape,
        compiler_params=_cp(("arbitrary",)),
    )(*[a for a, _, _ in ins])


def _full(arr):
    return (arr, arr.shape[1], 0)


def _rmsnorm_fwd(name, h, w, out_dtype):
    D = h.shape[1]

    def fn(_, hb, wb):
        rstd = lax.rsqrt(jnp.mean(hb * hb, axis=1, keepdims=True) + EPS)
        return [hb * rstd * wb]
    return _rowwise(name, fn, [_full(h), _full(w)], [(D, out_dtype, "row")], ROW_BLOCK, h.shape[0])[0]


def _rmsnorm_bwd(name, h, w, dn, dh_in):
    D = h.shape[1]

    def fn(_, hb, wb, dnb, dhb):
        rstd = lax.rsqrt(jnp.mean(hb * hb, axis=1, keepdims=True) + EPS)
        xhat = hb * rstd
        dxh = dnb.astype(F32) * wb
        dh = rstd * (dxh - xhat * jnp.mean(dxh * xhat, axis=1, keepdims=True))
        return [dhb + dh, jnp.sum(dnb.astype(F32) * xhat, axis=0, keepdims=True)]
    return _rowwise(name, fn, [_full(h), _full(w), _full(dn), _full(dh_in)],
                    [(D, F32, "row"), (D, F32, "acc")], ROW_BLOCK, h.shape[0])


def _headnorm_fn(group, act):
    def f(o, gate, w):
        rstd = lax.rsqrt(jnp.mean(o * o, axis=1, keepdims=True) + EPS)
        return o * rstd * w * act(gate)
    return f


def _headnorm_fwd(name, o, gate_arr, gate_w, gate_cb, w, group, act):
    N = o.shape[1]
    f = _headnorm_fn(group, act)

    def fn(_, ob, gb, wb):
        parts = [f(ob[:, s:s + group], gb[:, s:s + group], wb[:, s:s + group]) for s in range(0, N, group)]
        return [jnp.concatenate(parts, axis=1)]
    return _rowwise(name, fn, [_full(o), (gate_arr, gate_w, gate_cb), _full(w)], [(N, BF16, "row")], ROW_BLOCK, o.shape[0])[0]


def _headnorm_bwd(name, o, gate_arr, gate_w, gate_cb, w, dout, group, act):
    N = o.shape[1]
    f = _headnorm_fn(group, act)

    def fn(_, ob, gb, wb, db):
        dos, dgs, dws = [], [], []
        for s in range(0, N, group):
            _, vjp = jax.vjp(f, ob[:, s:s + group], gb[:, s:s + group], jnp.broadcast_to(wb[:, s:s + group], (ob.shape[0], group)))
            do, dgt, dw = vjp(db[:, s:s + group])
            dos.append(do)
            dgs.append(dgt)
            dws.append(jnp.sum(dw, axis=0, keepdims=True))
        return [jnp.concatenate(dos, axis=1), jnp.concatenate(dgs, axis=1), jnp.concatenate(dws, axis=1)]
    return _rowwise(name, fn, [_full(o), (gate_arr, gate_w, gate_cb), _full(w), _full(dout)],
                    [(N, F32, "row"), (N, F32, "row"), (N, F32, "acc")], ROW_BLOCK, o.shape[0])


def _colsum(name, a):
    def fn(_, ab):
        return [jnp.sum(ab.astype(F32), axis=0, keepdims=True)]
    tr = _div(a.shape[0], (512, 384, 256, 128, 64))
    return _rowwise(name, fn, [_full(a)], [(a.shape[1], F32, "acc")], tr, a.shape[0])[0]


def _loss_head(name, h, w, tgt, n_meta, seq):
    D = h.shape[1]
    tr = ROW_BLOCK

    def fn(row0, hb, wb, tb):
        row = row0 + _iota((tr, 1), 0)
        valid = (row >= n_meta) & (row < n_meta + seq)
        rstd = lax.rsqrt(jnp.mean(hb * hb, axis=1, keepdims=True) + EPS)
        xhat = hb * rstd
        err = jnp.where(valid, xhat * wb - tb, 0.0)
        loss = 0.5 * jnp.sum(jnp.mean(err * err, axis=1, keepdims=True), axis=0, keepdims=True)
        dy = err * (1.0 / D)
        dxh = dy * wb
        dh = rstd * (dxh - xhat * jnp.mean(dxh * xhat, axis=1, keepdims=True))
        return [dh, jnp.sum(dy * xhat, axis=0, keepdims=True), jnp.broadcast_to(loss, (1, LANES))]
    return _rowwise(name, fn, [_full(h), _full(w), _full(tgt)],
                    [(D, F32, "row"), (D, F32, "acc"), (LANES, F32, "acc")], tr, h.shape[0])


def _chunk_valid(ci, n_meta):
    lim = jnp.where(ci == 0, n_meta, CHUNK)
    return _iota((CHUNK, 1), 0) < lim, _iota((1, CHUNK), 1) < lim


def _tri_masks():
    r = _iota((CHUNK, CHUNK), 0)
    c = _iota((CHUNK, CHUNK), 1)
    return r >= c, r > c, r <= c


def _mlstm_chunk(vc, vr, m_st, c_st, n_st, q, k, v, li_c, lf_c, li_r, lf_r):
    tril, _, triu = _tri_masks()
    dk = q.shape[1]
    li_c = jnp.where(vc, li_c, NEG)
    li_r = jnp.where(vr, li_r, NEG)
    lf_c = jnp.where(vc, _log_sigmoid(lf_c), 0.0)
    lf_r = jnp.where(vr, _log_sigmoid(lf_r), 0.0)
    b_c = jnp.sum(jnp.where(tril, lf_r, 0.0), axis=1, keepdims=True)
    b_r = jnp.sum(jnp.where(triu, lf_c, 0.0), axis=0, keepdims=True)
    b_last = jnp.sum(lf_r, axis=1, keepdims=True)
    log_w = jnp.where(tril, b_c - b_r + li_r, NEG)
    log_init = b_c + m_st
    m_t = lax.stop_gradient(jnp.maximum(log_init, jnp.max(log_w, axis=1, keepdims=True)))
    w = jnp.exp(log_w - m_t)
    w_init = jnp.exp(log_init - m_t)
    qs = q * (dk ** -0.5)
    qk = _bnt(qs, k) * w
    num = w_init * _bnn(qs, c_st) + _bnn(qk, v)
    den = w_init * jnp.sum(qs * n_st, axis=1, keepdims=True) + jnp.sum(qk, axis=1, keepdims=True)
    h = num / jnp.maximum(jnp.abs(den), jnp.exp(-m_t))
    log_end_init = b_last + m_st
    log_end_r = b_last - b_r + li_r
    m_new = lax.stop_gradient(jnp.maximum(log_end_init, jnp.max(log_end_r, axis=1, keepdims=True)))
    a_init = jnp.exp(log_end_init - m_new)
    a_c = jnp.exp(b_last - b_c + li_c - m_new)
    ka = k * a_c
    c_new = a_init * c_st + _btn(ka, v)
    n_new = a_init * n_st + jnp.sum(ka, axis=0, keepdims=True)
    return (c_new, n_new, h), m_new


HEADS_PER_STEP = 4


def _head_batch(heads):
    return _div(heads, (HEADS_PER_STEP, 2, 1))


def _mlstm_io_specs(heads, hb, dk, dv, NC, rev):
    ci = (lambda c: NC - 1 - c) if rev else (lambda c: c)
    nb = heads // hb
    q = pl.BlockSpec((CHUNK, hb * dk), lambda h, c: (ci(c), h))
    k = pl.BlockSpec((CHUNK, hb * dk), lambda h, c: (ci(c), nb + h))
    v = pl.BlockSpec((CHUNK, hb * dv), lambda h, c: (ci(c), (2 * heads * dk) // (hb * dv) + h))
    col = pl.BlockSpec((hb, CHUNK, 1), lambda h, c: (h, ci(c), 0))
    row = pl.BlockSpec((hb, 1, 1, CHUNK), lambda h, c: (h, ci(c), 0, 0))
    st = [pl.BlockSpec((hb, 1, dk, dv), lambda h, c: (h, ci(c), 0, 0)),
          pl.BlockSpec((hb, 1, 1, dk), lambda h, c: (h, ci(c), 0, 0)),
          pl.BlockSpec((hb, 1, 1, 1), lambda h, c: (h, ci(c), 0, 0))]
    wide = pl.BlockSpec((CHUNK, hb * dv), lambda h, c: (ci(c), h))
    return q, k, v, col, row, st, wide


def _mlstm_core_fwd(pc, li_c, lf_c, li_r, lf_r, heads, dk, dv, n_meta):
    TC = pc.shape[0]
    NC = TC // CHUNK
    hb = _head_batch(heads)

    def body(q_ref, k_ref, v_ref, lic, lfc, lir, lfr, h_ref, cs_ref, ns_ref, ms_ref, c_s, n_s, m_s):
        ci = pl.program_id(1)

        @pl.when(ci == 0)
        def _():
            c_s[...] = jnp.zeros_like(c_s)
            n_s[...] = jnp.zeros_like(n_s)
            m_s[...] = jnp.zeros_like(m_s)

        vc, vr = _chunk_valid(ci, n_meta)
        for j in range(hb):
            cs_ref[j, 0] = c_s[j]
            ns_ref[j, 0] = n_s[j]
            ms_ref[j, 0] = m_s[j]
            (c_new, n_new, h), m_new = _mlstm_chunk(
                vc, vr, m_s[j], c_s[j], n_s[j], q_ref[:, j * dk:(j + 1) * dk], k_ref[:, j * dk:(j + 1) * dk],
                v_ref[:, j * dv:(j + 1) * dv], lic[j], lfc[j], lir[j, 0], lfr[j, 0])
            h_ref[:, j * dv:(j + 1) * dv] = h
            c_s[j] = c_new
            n_s[j] = n_new
            m_s[j] = m_new

    q, k, v, col, row, st, wide = _mlstm_io_specs(heads, hb, dk, dv, NC, False)
    return pl.pallas_call(
        body, name="mlstm_core_fwd", grid=(heads // hb, NC),
        in_specs=[q, k, v, col, col, row, row],
        out_specs=[wide] + st,
        out_shape=[jax.ShapeDtypeStruct((TC, heads * dv), F32),
                   jax.ShapeDtypeStruct((heads, NC, dk, dv), F32),
                   jax.ShapeDtypeStruct((heads, NC, 1, dk), F32),
                   jax.ShapeDtypeStruct((heads, NC, 1, 1), F32)],
        scratch_shapes=[pltpu.VMEM((hb, dk, dv), F32), pltpu.VMEM((hb, 1, dk), F32), pltpu.VMEM((hb, 1, 1), F32)],
        compiler_params=_cp(("parallel", "arbitrary")),
    )(pc, pc, pc, li_c, lf_c, li_r, lf_r)


def _mlstm_core_bwd(pc, li_c, lf_c, li_r, lf_r, cs, ns, ms, dh, heads, dk, dv, n_meta):
    TC = pc.shape[0]
    NC = TC // CHUNK
    hb = _head_batch(heads)

    def body(q_ref, k_ref, v_ref, lic, lfc, lir, lfr, cs_ref, ns_ref, ms_ref, dh_ref,
             dq_ref, dk_ref, dv_ref, dlic, dlfc, dlir, dlfr, dc_s, dn_s):
        step = pl.program_id(1)
        ci = NC - 1 - step

        @pl.when(step == 0)
        def _():
            dc_s[...] = jnp.zeros_like(dc_s)
            dn_s[...] = jnp.zeros_like(dn_s)

        vc, vr = _chunk_valid(ci, n_meta)
        for j in range(hb):
            ks, vs = slice(j * dk, (j + 1) * dk), slice(j * dv, (j + 1) * dv)
            m_st = ms_ref[j, 0]
            fn = lambda *a, m_st=m_st: _mlstm_chunk(vc, vr, m_st, *a)
            _, vjp, _ = jax.vjp(fn, cs_ref[j, 0], ns_ref[j, 0], q_ref[:, ks], k_ref[:, ks], v_ref[:, vs],
                                lic[j], lfc[j], lir[j, 0], lfr[j, 0], has_aux=True)
            dc, dn, dq, dkk, dvv, g0, g1, g2, g3 = vjp((dc_s[j], dn_s[j], dh_ref[:, vs]))
            dq_ref[:, ks] = dq
            dk_ref[:, ks] = dkk
            dv_ref[:, vs] = dvv
            dlic[j] = g0
            dlfc[j] = g1
            dlir[j, 0] = g2
            dlfr[j, 0] = g3
            dc_s[j] = dc
            dn_s[j] = dn

    q, k, v, col, row, st, wide = _mlstm_io_specs(heads, hb, dk, dv, NC, True)
    return pl.pallas_call(
        body, name="mlstm_core_bwd", grid=(heads // hb, NC),
        in_specs=[q, k, v, col, col, row, row] + st + [wide],
        out_specs=[q, q, wide, col, col, row, row],
        out_shape=[jax.ShapeDtypeStruct((TC, heads * dk), F32), jax.ShapeDtypeStruct((TC, heads * dk), F32),
                   jax.ShapeDtypeStruct((TC, heads * dv), F32),
                   jax.ShapeDtypeStruct(li_c.shape, F32), jax.ShapeDtypeStruct(lf_c.shape, F32),
                   jax.ShapeDtypeStruct(li_r.shape, F32), jax.ShapeDtypeStruct(lf_r.shape, F32)],
        scratch_shapes=[pltpu.VMEM((hb, dk, dv), F32), pltpu.VMEM((hb, 1, dk), F32)],
        compiler_params=_cp(("parallel", "arbitrary")),
    )(pc, pc, pc, li_c, lf_c, li_r, lf_r, cs, ns, ms, dh)


@jax.custom_vjp
def _tri_solve(low, rhs):
    return _tri_solve_fwd(low, rhs)[0]


def _tri_solve_fwd(low, rhs):
    levels = int(math.log2(low.shape[-1]))
    p = -low
    r = p
    for i in range(levels):
        if i > 0:
            r = r + p + _rqnn(p, r)
        if i < levels - 1:
            p = _rqnn(p, p)
    sol = rhs + _rqnn(r, rhs)
    return sol, (r, sol)


def _tri_solve_bwd(res, ct):
    r, sol = res
    d_rhs = ct + _rqtn(r, ct)
    return -_rqnt(d_rhs, sol), d_rhs


_tri_solve.defvjp(_tri_solve_fwd, _tri_solve_bwd)


def _l2norm(x):
    return x * lax.rsqrt(jnp.sum(x * x, axis=-1, keepdims=True) + EPS)


def _lane_heads(ref, n, width):
    return jnp.stack([ref[:, j * width:(j + 1) * width] for j in range(n)], axis=0)


def _gdn_chunk(vc, vr, s_st, q, k, v, g_c, b_c, g_r):
    tril, strict, triu = _tri_masks()
    dk = q.shape[-1]
    pair = lambda t: jnp.concatenate([t[j:j + 1] for j in range(t.shape[0]) for _ in (0, 1)], axis=0)
    qn = _l2norm(q) * (dk ** -0.5)
    kn = _l2norm(k)
    qk = pair(_qnt(qn, kn))
    qn, kn = pair(qn), pair(kn)
    g_c = jnp.where(vc, g_c, 0.0)
    g_r = jnp.where(vr, g_r, 0.0)
    b_c = jnp.where(vc, b_c, 0.0)
    gc_c = jnp.sum(jnp.where(tril, g_r, 0.0), axis=2, keepdims=True)
    gc_r = jnp.sum(jnp.where(triu, g_c, 0.0), axis=1, keepdims=True)
    g_last = jnp.sum(g_r, axis=2, keepdims=True)
    decay = jnp.exp(jnp.where(tril, gc_c - gc_r, NEG))
    kb = kn * b_c
    low = jnp.where(strict, _qnt(kb, kn) * decay, 0.0)
    eg = jnp.exp(gc_c)
    sol = _tri_solve(low, jnp.concatenate([v * b_c, kb * eg], axis=2))
    u_vec, w_vec = sol[:, :, :dk], sol[:, :, dk:]
    v_new = u_vec - _qnn(w_vec, s_st)
    o = _qnn(qn * eg, s_st) + _qnn(qk * decay, v_new)
    s_new = jnp.exp(g_last) * s_st + _qtn(kn * jnp.exp(g_last - gc_c), v_new)
    return s_new, o


def _gdn_core_fwd(qkv, g_c, b_c, g_r, qk_heads, dk, n_meta):
    TC = qkv.shape[0]
    NC = TC // CHUNK
    H = qk_heads

    hb = _head_batch(H)

    def body(q_ref, k_ref, v_ref, gc, bc, gr, o_ref, st_ref, s_s):
        ci = pl.program_id(1)

        @pl.when(ci == 0)
        def _():
            s_s[...] = jnp.zeros_like(s_s)

        vc, vr = _chunk_valid(ci, n_meta)
        s_st = s_s[...]
        st_ref[:, 0] = s_st.reshape(hb, 2, dk, dk)
        s_new, o = _gdn_chunk(vc, vr, s_st, _lane_heads(q_ref, hb, dk), _lane_heads(k_ref, hb, dk),
                              _lane_heads(v_ref, 2 * hb, dk), gc[...], bc[...], gr[:, 0])
        for b in range(2 * hb):
            o_ref[:, b * dk:(b + 1) * dk] = o[b]
        s_s[...] = s_new

    q, k, v, col, row, st = _gdn_io_specs(H, hb, dk, NC, False)
    return pl.pallas_call(
        body, name="gdn_core_fwd", grid=(H // hb, NC),
        in_specs=[q, k, v, col, col, row],
        out_specs=[_gdn_wide_spec(hb, dk, NC, False), st],
        out_shape=[jax.ShapeDtypeStruct((TC, 2 * H * dk), F32), jax.ShapeDtypeStruct((H, NC, 2, dk, dk), F32)],
        scratch_shapes=[pltpu.VMEM((2 * hb, dk, dk), F32)],
        compiler_params=_cp(("parallel", "arbitrary")),
    )(qkv, qkv, qkv, g_c, b_c, g_r)


def _gdn_wide_spec(hb, dk, NC, rev):
    ci = (lambda c: NC - 1 - c) if rev else (lambda c: c)
    return pl.BlockSpec((CHUNK, 2 * hb * dk), lambda h, c: (ci(c), h))


def _gdn_io_specs(H, hb, dk, NC, rev):
    ci = (lambda c: NC - 1 - c) if rev else (lambda c: c)
    nb = H // hb
    q = pl.BlockSpec((CHUNK, hb * dk), lambda h, c: (ci(c), h))
    k = pl.BlockSpec((CHUNK, hb * dk), lambda h, c: (ci(c), nb + h))
    v = pl.BlockSpec((CHUNK, 2 * hb * dk), lambda h, c: (ci(c), nb + h))
    col = pl.BlockSpec((2 * hb, CHUNK, 1), lambda h, c: (h, ci(c), 0))
    row = pl.BlockSpec((2 * hb, 1, 1, CHUNK), lambda h, c: (h, ci(c), 0, 0))
    st = pl.BlockSpec((hb, 1, 2, dk, dk), lambda h, c: (h, ci(c), 0, 0, 0))
    return q, k, v, col, row, st


def _gdn_core_bwd(qkv, g_c, b_c, g_r, st, do, qk_heads, dk, n_meta):
    TC = qkv.shape[0]
    NC = TC // CHUNK
    H = qk_heads
    hb = _head_batch(H)

    def body(q_ref, k_ref, v_ref, gc, bc, gr, st_ref, do_ref, dq_ref, dk_ref, dv_ref, dgc, dbc, dgr, ds_s):
        step = pl.program_id(1)
        ci = NC - 1 - step

        @pl.when(step == 0)
        def _():
            ds_s[...] = jnp.zeros_like(ds_s)

        vc, vr = _chunk_valid(ci, n_meta)
        fn = lambda *a: _gdn_chunk(vc, vr, *a)
        _, vjp = jax.vjp(fn, st_ref[:, 0].reshape(2 * hb, dk, dk), _lane_heads(q_ref, hb, dk), _lane_heads(k_ref, hb, dk),
                         _lane_heads(v_ref, 2 * hb, dk), gc[...], bc[...], gr[:, 0])
        ds, dq, dkk, dvv, d_gc, d_bc, d_gr = vjp((ds_s[...], _lane_heads(do_ref, 2 * hb, dk)))
        ds_s[...] = ds
        for j in range(hb):
            dq_ref[:, j * dk:(j + 1) * dk] = dq[j]
            dk_ref[:, j * dk:(j + 1) * dk] = dkk[j]
        for b in range(2 * hb):
            dv_ref[:, b * dk:(b + 1) * dk] = dvv[b]
        dgc[...] = d_gc
        dbc[...] = d_bc
        dgr[:, 0] = d_gr

    q, k, v, col, row, stspec = _gdn_io_specs(H, hb, dk, NC, True)
    wide = _gdn_wide_spec(hb, dk, NC, True)
    return pl.pallas_call(
        body, name="gdn_core_bwd", grid=(H // hb, NC),
        in_specs=[q, k, v, col, col, row, stspec, wide],
        out_specs=[q, q, wide, col, col, row],
        out_shape=[jax.ShapeDtypeStruct((TC, H * dk), F32), jax.ShapeDtypeStruct((TC, H * dk), F32),
                   jax.ShapeDtypeStruct((TC, 2 * H * dk), F32),
                   jax.ShapeDtypeStruct(g_c.shape, F32), jax.ShapeDtypeStruct(g_c.shape, F32),
                   jax.ShapeDtypeStruct(g_r.shape, F32)],
        scratch_shapes=[pltpu.VMEM((2 * hb, dk, dk), F32)],
        compiler_params=_cp(("parallel", "arbitrary")),
    )(qkv, qkv, qkv, g_c, b_c, g_r, st, do)


def _gdn_gate_fn(a_pre, b_pre, a_log, dt_bias):
    return -jnp.exp(a_log) * _softplus(a_pre + dt_bias), _sigmoid(b_pre)


def _gdn_gates_fwd(a_pre, b_pre, a_log, dt_bias):
    n = a_pre.shape[1]

    def fn(_, ab, bb, al, dt):
        g, beta = _gdn_gate_fn(ab, bb, al, dt)
        return [g, beta]
    return _rowwise("gdn_gates_fwd", fn, [_full(a_pre), _full(b_pre), _full(a_log), _full(dt_bias)],
                    [(n, F32, "row"), (n, F32, "row")], ROW_BLOCK, a_pre.shape[0])


def _gdn_gates_bwd(a_pre, b_pre, a_log, dt_bias, dg, dbeta):
    n = a_pre.shape[1]

    def fn(_, ab, bb, al, dt, dgb, dbb):
        rows = ab.shape[0]
        _, vjp = jax.vjp(_gdn_gate_fn, ab, bb, jnp.broadcast_to(al, (rows, n)), jnp.broadcast_to(dt, (rows, n)))
        da, db, dal, ddt = vjp((dgb, dbb))
        return [da, db, jnp.sum(dal, axis=0, keepdims=True), jnp.sum(ddt, axis=0, keepdims=True)]
    return _rowwise("gdn_gates_bwd", fn, [_full(a_pre), _full(b_pre), _full(a_log), _full(dt_bias), _full(dg), _full(dbeta)],
                    [(n, F32, "row"), (n, F32, "row"), (n, F32, "acc"), (n, F32, "acc")], ROW_BLOCK, a_pre.shape[0])


def _shift_down(cur, prev, j):
    row = _iota(cur.shape, 0)
    return jnp.where(row >= j, pltpu.roll(cur, j, 0), pltpu.roll(prev, j, 0))


def _shift_up(cur, nxt, j):
    n = cur.shape[0]
    row = _iota(cur.shape, 0)
    return jnp.where(row < n - j, pltpu.roll(cur, n - j, 0), pltpu.roll(nxt, n - j, 0))


def _conv_acc(cur, prev, w):
    acc = cur * w[GDN_CONV - 1:GDN_CONV, :]
    for j in range(1, GDN_CONV):
        acc = acc + _shift_down(cur, prev, j) * w[GDN_CONV - 1 - j:GDN_CONV - j, :]
    return acc


def _conv_tiles(width):
    return _div(width, (1024, 512, 256, 128))


def _conv_fwd(p, w, width):
    TP = p.shape[0]
    nb, tn = TP // ROW_BLOCK, _conv_tiles(width)

    def body(cur_ref, prev_ref, w_ref, y_ref):
        i = pl.program_id(1)
        prev = jnp.where(i > 0, prev_ref[...], 0.0)
        y_ref[...] = _silu(_conv_acc(cur_ref[...], prev, w_ref[...]))

    return pl.pallas_call(
        body, name="gdn_conv_fwd", grid=(width // tn, nb),
        in_specs=[pl.BlockSpec((ROW_BLOCK, tn), lambda j, i: (i, j)),
                  pl.BlockSpec((ROW_BLOCK, tn), lambda j, i: (jnp.maximum(i - 1, 0), j)),
                  pl.BlockSpec((GDN_CONV, tn), lambda j, i: (0, j))],
        out_specs=pl.BlockSpec((ROW_BLOCK, tn), lambda j, i: (i, j)),
        out_shape=jax.ShapeDtypeStruct((TP, width), F32),
        compiler_params=_cp(("parallel", "arbitrary")),
    )(p, p, w)


def _conv_bwd_pre(p, w, dy, width):
    TP = p.shape[0]
    nb, tn = TP // ROW_BLOCK, _conv_tiles(width)

    def body(cur_ref, prev_ref, w_ref, dy_ref, da_ref, dw_ref):
        i = pl.program_id(1)
        cur = cur_ref[...]
        prev = jnp.where(i > 0, prev_ref[...], 0.0)
        acc = _conv_acc(cur, prev, w_ref[...])
        s = _sigmoid(acc)
        da = dy_ref[...] * (s * (1.0 + acc * (1.0 - s)))
        da_ref[...] = da

        @pl.when(i == 0)
        def _():
            dw_ref[...] = jnp.zeros_like(dw_ref)

        rows = [jnp.sum(da * (cur if j == 0 else _shift_down(cur, prev, j)), axis=0, keepdims=True)
                for j in range(GDN_CONV - 1, -1, -1)]
        dw_ref[...] += jnp.concatenate(rows, axis=0)

    return pl.pallas_call(
        body, name="gdn_conv_bwd_pre", grid=(width // tn, nb),
        in_specs=[pl.BlockSpec((ROW_BLOCK, tn), lambda j, i: (i, j)),
                  pl.BlockSpec((ROW_BLOCK, tn), lambda j, i: (jnp.maximum(i - 1, 0), j)),
                  pl.BlockSpec((GDN_CONV, tn), lambda j, i: (0, j)),
                  pl.BlockSpec((ROW_BLOCK, tn), lambda j, i: (i, j))],
        out_specs=[pl.BlockSpec((ROW_BLOCK, tn), lambda j, i: (i, j)), pl.BlockSpec((GDN_CONV, tn), lambda j, i: (0, j))],
        out_shape=[jax.ShapeDtypeStruct((TP, width), F32), jax.ShapeDtypeStruct((GDN_CONV, width), F32)],
        compiler_params=_cp(("parallel", "arbitrary")),
    )(p, p, w, dy)


def _conv_bwd_dx(da, w, width):
    TP = da.shape[0]
    nb, tn = TP // ROW_BLOCK, _conv_tiles(width)

    def body(cur_ref, nxt_ref, w_ref, dx_ref):
        i = pl.program_id(1)
        cur = cur_ref[...]
        nxt = jnp.where(i < nb - 1, nxt_ref[...], 0.0)
        w_all = w_ref[...]
        dx = cur * w_all[GDN_CONV - 1:GDN_CONV, :]
        for j in range(1, GDN_CONV):
            dx = dx + _shift_up(cur, nxt, j) * w_all[GDN_CONV - 1 - j:GDN_CONV - j, :]
        dx_ref[...] = dx

    return pl.pallas_call(
        body, name="gdn_conv_bwd_dx", grid=(width // tn, nb),
        in_specs=[pl.BlockSpec((ROW_BLOCK, tn), lambda j, i: (i, j)),
                  pl.BlockSpec((ROW_BLOCK, tn), lambda j, i: (jnp.minimum(i + 1, nb - 1), j)),
                  pl.BlockSpec((GDN_CONV, tn), lambda j, i: (0, j))],
        out_specs=pl.BlockSpec((ROW_BLOCK, tn), lambda j, i: (i, j)),
        out_shape=jax.ShapeDtypeStruct((TP, width), F32),
        compiler_params=_cp(("parallel", "arbitrary")),
    )(da, da, w)


def _pool_bands(i, win):
    n = ROW_BLOCK
    t = _iota((n, n), 0)
    s = _iota((n, n), 1)
    cnt = jnp.minimum(i * n + t + 1, win).astype(F32)
    cur = jnp.where((t - s >= 0) & (t - s < win), 1.0 / cnt, 0.0)
    prev = jnp.where((t + n - s < win) & (i > 0), 1.0 / cnt, 0.0)
    return cur, prev


def _pool_fwd(u):
    TP, D = u.shape
    nb, grp = TP // ROW_BLOCK, D // len(POOL_WINDOWS)

    def body(cur_ref, prev_ref, out_ref):
        i = pl.program_id(0)
        for gi, win in enumerate(POOL_WINDOWS):
            sl = slice(gi * grp, (gi + 1) * grp)
            bc, bp = _pool_bands(i, win)
            cur = cur_ref[:, sl]
            out_ref[:, sl] = (_rhnn(bc, cur) + _rhnn(bp, prev_ref[:, sl]) - cur).astype(out_ref.dtype)

    return pl.pallas_call(
        body, name="pool_fwd", grid=(nb,),
        in_specs=[pl.BlockSpec((ROW_BLOCK, D), lambda i: (i, 0)),
                  pl.BlockSpec((ROW_BLOCK, D), lambda i: (jnp.maximum(i - 1, 0), 0))],
        out_specs=pl.BlockSpec((ROW_BLOCK, D), lambda i: (i, 0)),
        out_shape=jax.ShapeDtypeStruct((TP, D), BF16),
        compiler_params=_cp(("arbitrary",)),
    )(u, u)


def _pool_bwd(dp):
    TP, D = dp.shape
    nb, grp = TP // ROW_BLOCK, D // len(POOL_WINDOWS)

    def body(cur_ref, nxt_ref, out_ref):
        i = pl.program_id(0)
        for gi, win in enumerate(POOL_WINDOWS):
            sl = slice(gi * grp, (gi + 1) * grp)
            bc, _ = _pool_bands(i, win)
            _, bp = _pool_bands(i + 1, win)
            cur = cur_ref[:, sl]
            nxt = jnp.where(i < nb - 1, nxt_ref[:, sl], 0.0)
            out_ref[:, sl] = _rhtn(bc, cur) + _rhtn(bp, nxt) - cur

    return pl.pallas_call(
        body, name="pool_bwd", grid=(nb,),
        in_specs=[pl.BlockSpec((ROW_BLOCK, D), lambda i: (i, 0)),
                  pl.BlockSpec((ROW_BLOCK, D), lambda i: (jnp.minimum(i + 1, nb - 1), 0))],
        out_specs=pl.BlockSpec((ROW_BLOCK, D), lambda i: (i, 0)),
        out_shape=jax.ShapeDtypeStruct((TP, D), F32),
        compiler_params=_cp(("arbitrary",)),
    )(dp, dp)


def _rot_matrix(dh):
    s = _iota((dh, dh), 0)
    t = _iota((dh, dh), 1)
    return jnp.where(s == t + dh // 2, -1.0, 0.0) + jnp.where(s == t - dh // 2, 1.0, 0.0)


def _swa_block(i, t_real, q, k_prev, k_cur, v_prev, v_cur, cos_q, sin_q, cos_p, sin_p, sink):
    n = ROW_BLOCK
    dh = q.shape[1]
    g = q.shape[0] // n
    rot = _rot_matrix(dh)
    rope = lambda x, c, s: x * c + _hnn(x, rot) * s
    qr = rope(q, jnp.concatenate([cos_q] * g, axis=0), jnp.concatenate([sin_q] * g, axis=0))
    kb = jnp.concatenate([rope(k_prev, cos_p, sin_p), rope(k_cur, cos_q, sin_q)], axis=0)
    vb = jnp.concatenate([v_prev, v_cur], axis=0)
    s = _bnt(qr, kb) * (dh ** -0.5)
    qpos = i * n + (_iota((g * n, 2 * n), 0) % n)
    kpos = (i - 1) * n + _iota((g * n, 2 * n), 1)
    mask = (kpos <= qpos) & (qpos - kpos < SWA_WINDOW) & (kpos >= 0) & (kpos < t_real)
    s = jnp.where(mask, s, NEG)
    m = lax.stop_gradient(jnp.maximum(jnp.max(s, axis=1, keepdims=True), sink))
    e = jnp.where(mask, jnp.exp(s - m), 0.0)
    den = jnp.sum(e, axis=1, keepdims=True) + jnp.exp(sink - m)
    return _bnn(e / den, vb)


def _swa_core(q, k, v, cos, sin, sink, t_real, do=None):
    hkv, g, TP, dh = q.shape
    n = ROW_BLOCK
    nb = TP // n
    bwd = do is not None

    def body(*refs):
        q_ref, kp_ref, kc_ref, vp_ref, vc_ref, cq, sq, cpv, spv, sink_ref = refs[:10]
        i = pl.program_id(1)
        fn = lambda *a: _swa_block(i, t_real, *a)
        args = (q_ref[0].reshape(g * n, dh), kp_ref[0], kc_ref[0], vp_ref[0], vc_ref[0],
                cq[...], sq[...], cpv[...], spv[...], sink_ref[0])
        if not bwd:
            refs[10][0] = fn(*args).reshape(g, n, dh)
            return
        do_ref, dq_ref, dkp_ref, dkc_ref, dvp_ref, dvc_ref, dsink_ref = refs[10:17]
        _, vjp = jax.vjp(fn, *args)
        d = vjp(do_ref[0].reshape(g * n, dh))
        dq_ref[0] = d[0].reshape(g, n, dh)
        dkp_ref[0] = d[1]
        dkc_ref[0] = d[2]
        dvp_ref[0] = d[3]
        dvc_ref[0] = d[4]

        @pl.when(i == 0)
        def _():
            dsink_ref[...] = jnp.zeros_like(dsink_ref)
        dsink_ref[0] += d[9]

    qspec = pl.BlockSpec((1, g, n, dh), lambda h, i: (h, 0, i, 0))
    cur = pl.BlockSpec((1, n, dh), lambda h, i: (h, i, 0))
    prev = pl.BlockSpec((1, n, dh), lambda h, i: (h, jnp.maximum(i - 1, 0), 0))
    tcur = pl.BlockSpec((n, dh), lambda h, i: (i, 0))
    tprev = pl.BlockSpec((n, dh), lambda h, i: (jnp.maximum(i - 1, 0), 0))
    sspec = pl.BlockSpec((1, g * n, 1), lambda h, i: (h, 0, 0))
    in_specs = [qspec, prev, cur, prev, cur, tcur, tcur, tprev, tprev, sspec]
    ins = [q, k, k, v, v, cos, sin, cos, sin, sink]
    if not bwd:
        out_specs, out_shape = [qspec], [jax.ShapeDtypeStruct(q.shape, F32)]
    else:
        in_specs.append(qspec)
        ins.append(do)
        kv = jax.ShapeDtypeStruct(k.shape, F32)
        out_specs = [qspec, cur, cur, cur, cur, sspec]
        out_shape = [jax.ShapeDtypeStruct(q.shape, F32), kv, kv, kv, kv, jax.ShapeDtypeStruct(sink.shape, F32)]
    return pl.pallas_call(
        body, name="swa_core_bwd" if bwd else "swa_core_fwd", grid=(hkv, nb),
        in_specs=in_specs, out_specs=out_specs, out_shape=out_shape,
        compiler_params=_cp(("parallel", "arbitrary")),
    )(*ins)


def _coords():
    return lax.axis_index("x"), lax.axis_index("y"), lax.axis_index("c")


def _other_chips(x, y):
    return [(1 - x, y), (x, 1 - y), (1 - x, 1 - y)]


def _rcopy(src, dst, send, recv, dev):
    return pltpu.make_async_remote_copy(src_ref=src, dst_ref=dst, send_sem=send, recv_sem=recv,
                                        device_id=dev, device_id_type=MESH)


ANY = pl.BlockSpec(memory_space=pl.ANY)


def _allgather(name, shards, split):
    n = len(shards)

    def body(*refs):
        ins, outs = refs[:n], refs[n:2 * n]
        send, recv, loc = refs[2 * n:]
        x, y, c = _coords()
        s_me = 2 * x + y
        chips = _other_chips(x, y)
        started = []
        for i in range(n):
            lc = pltpu.make_async_copy(ins[i], outs[i].at[s_me], loc.at[i])
            lc.start()
            started.append(lc)
        if split:
            halves = [shards[i].shape[0] // 2 for i in range(n)]
            mine = [pl.ds(c * h, h) for h in halves]
            other = [pl.ds((1 - c) * h, h) for h in halves]
        sends = []
        for i in range(n):
            for j, (px, py) in enumerate(chips):
                if split:
                    cp = _rcopy(ins[i].at[mine[i]], outs[i].at[s_me, mine[i]], send.at[i, j], recv.at[i, j], (px, py, c))
                else:
                    cp = _rcopy(ins[i], outs[i].at[s_me], send.at[i, j], recv.at[i, j], (px, py, c))
                cp.start()
                sends.append(cp)
        for i in range(n):
            for j, (px, py) in enumerate(chips):
                s_j = 2 * px + py
                if split:
                    land = outs[i].at[s_j, mine[i]]
                    _rcopy(land, land, send.at[i, j], recv.at[i, j], (px, py, c)).wait_recv()
                    fw = _rcopy(land, land, send.at[i, 3 + j], recv.at[i, 3 + j], (x, y, 1 - c))
                    fw.start()
                    sends.append(fw)
                else:
                    land = outs[i].at[s_j]
                    _rcopy(land, land, send.at[i, j], recv.at[i, j], (px, py, c)).wait_recv()
        if split:
            for i in range(n):
                for j, (px, py) in enumerate(chips):
                    land = outs[i].at[2 * px + py, other[i]]
                    _rcopy(land, land, send.at[i, 3 + j], recv.at[i, 3 + j], (x, y, 1 - c)).wait_recv()
        for cp in sends:
            cp.wait_send()
        for lc in started:
            lc.wait()

    nsem = 6 if split else 3
    return pl.pallas_call(
        body, name=name,
        in_specs=[ANY] * n, out_specs=[ANY] * n,
        out_shape=[jax.ShapeDtypeStruct((N_CHIPS,) + s.shape, s.dtype) for s in shards],
        scratch_shapes=[pltpu.SemaphoreType.DMA((n, nsem)), pltpu.SemaphoreType.DMA((n, nsem)), pltpu.SemaphoreType.DMA((n,))],
        compiler_params=pltpu.CompilerParams(has_side_effects=True),
    )(*shards)


def _rs_pair_exchange(name, grads):
    n = len(grads)

    def body(*refs):
        ins, bufs = refs[:n], refs[n:2 * n]
        send, recv = refs[2 * n:]
        x, y, c = _coords()
        cps = []
        for i in range(n):
            half = grads[i].shape[1] // 2
            cp = _rcopy(ins[i].at[pl.ds(0, N_CHIPS), pl.ds((1 - c) * half, half)], bufs[i], send.at[i], recv.at[i], (x, y, 1 - c))
            cp.start()
            cps.append(cp)
        for cp in cps:
            cp.wait_recv()
        for cp in cps:
            cp.wait_send()

    return pl.pallas_call(
        body, name=name, in_specs=[ANY] * n, out_specs=[ANY] * n,
        out_shape=[jax.ShapeDtypeStruct((N_CHIPS, g.shape[1] // 2, g.shape[2]), g.dtype) for g in grads],
        scratch_shapes=[pltpu.SemaphoreType.DMA((n,)), pltpu.SemaphoreType.DMA((n,))],
        compiler_params=pltpu.CompilerParams(has_side_effects=True),
    )(*grads)


def _row_tile(rows, cols, itemsize, n_bufs):
    for tr in (2048, 1024, 512, 256, 128, 64, 32, 16, 8):
        if rows % tr == 0 and 2 * n_bufs * tr * cols * itemsize <= VMEM_BUDGET // 2:
            return tr
    return rows


def _rs_pair_sum(name, g, buf, core):
    _, R, C = g.shape
    half = R // 2
    tr = _row_tile(half, C, 4, 3)
    nhb = half // tr

    def body(c_ref, g_ref, b_ref, o_ref):
        o_ref[...] = (g_ref[...].astype(F32) + b_ref[...].astype(F32)).astype(o_ref.dtype)

    return pl.pallas_call(
        body, name=name,
        grid_spec=pltpu.PrefetchScalarGridSpec(
            num_scalar_prefetch=1, grid=(N_CHIPS, nhb),
            in_specs=[pl.BlockSpec((1, tr, C), lambda s, r, c_ref: (s, c_ref[0] * nhb + r, 0)),
                      pl.BlockSpec((1, tr, C), lambda s, r, c_ref: (s, r, 0))],
            out_specs=pl.BlockSpec((1, tr, C), lambda s, r, c_ref: (s, r, 0))),
        out_shape=jax.ShapeDtypeStruct((N_CHIPS, half, C), BF16),
        compiler_params=_cp(("arbitrary", "arbitrary")),
    )(core, g, buf)


def _rs_all_to_all(name, parts):
    n = len(parts)

    def body(*refs):
        ins, bufs = refs[:n], refs[n:2 * n]
        send, recv, loc = refs[2 * n:]
        x, y, c = _coords()
        s_me = 2 * x + y
        chips = _other_chips(x, y)
        cps, lcs = [], []
        for i in range(n):
            lc = pltpu.make_async_copy(ins[i].at[s_me], bufs[i].at[s_me], loc.at[i])
            lc.start()
            lcs.append(lc)
            for j, (px, py) in enumerate(chips):
                cp = _rcopy(ins[i].at[2 * px + py], bufs[i].at[s_me], send.at[i, j], recv.at[i, j], (px, py, c))
                cp.start()
                cps.append(cp)
        for i in range(n):
            for j, (px, py) in enumerate(chips):
                land = bufs[i].at[2 * px + py]
                _rcopy(land, land, send.at[i, j], recv.at[i, j], (px, py, c)).wait_recv()
        for cp in cps:
            cp.wait_send()
        for lc in lcs:
            lc.wait()

    return pl.pallas_call(
        body, name=name, in_specs=[ANY] * n, out_specs=[ANY] * n,
        out_shape=[jax.ShapeDtypeStruct(p.shape, p.dtype) for p in parts],
        scratch_shapes=[pltpu.SemaphoreType.DMA((n, 3)), pltpu.SemaphoreType.DMA((n, 3)), pltpu.SemaphoreType.DMA((n,))],
        compiler_params=pltpu.CompilerParams(has_side_effects=True),
    )(*parts)


def _rs_sum4(name, buf, core):
    _, half, C = buf.shape
    tr = _row_tile(half, C, 4, 3)
    nhb = half // tr

    def body(c_ref, b_ref, o_ref):
        acc = b_ref[0].astype(F32)
        for s in range(1, N_CHIPS):
            acc = acc + b_ref[s].astype(F32)
        o_ref[...] = acc

    return pl.pallas_call(
        body, name=name,
        grid_spec=pltpu.PrefetchScalarGridSpec(
            num_scalar_prefetch=1, grid=(nhb,),
            in_specs=[pl.BlockSpec((N_CHIPS, tr, C), lambda r, c_ref: (0, r, 0))],
            out_specs=pl.BlockSpec((tr, C), lambda r, c_ref: (c_ref[0] * nhb + r, 0))),
        out_shape=jax.ShapeDtypeStruct((2 * half, C), F32),
        compiler_params=_cp(("arbitrary",)),
    )(core, buf)


def _rs_share_halves(name, outs):
    n = len(outs)

    def body(*refs):
        o = refs[n:2 * n]
        send, recv = refs[2 * n:]
        x, y, c = _coords()
        cps = []
        for i in range(n):
            half = outs[i].shape[0] // 2
            mine = o[i].at[pl.ds(c * half, half)]
            cp = _rcopy(mine, mine, send.at[i], recv.at[i], (x, y, 1 - c))
            cp.start()
            cps.append(cp)
        for i in range(n):
            half = outs[i].shape[0] // 2
            land = o[i].at[pl.ds((1 - c) * half, half)]
            _rcopy(land, land, send.at[i], recv.at[i], (x, y, 1 - c)).wait_recv()
        for cp in cps:
            cp.wait_send()

    return pl.pallas_call(
        body, name=name, in_specs=[ANY] * n, out_specs=[ANY] * n,
        out_shape=[jax.ShapeDtypeStruct(a.shape, a.dtype) for a in outs],
        scratch_shapes=[pltpu.SemaphoreType.DMA((n,)), pltpu.SemaphoreType.DMA((n,))],
        input_output_aliases={i: i for i in range(n)},
        compiler_params=pltpu.CompilerParams(has_side_effects=True),
    )(*outs)


def _reduce_scatter(tag, grads, core):
    bufs = _rs_pair_exchange(f"rs_pair_{tag}", grads)
    parts = [_rs_pair_sum(f"rs_pairsum_{tag}_{i}", g, b, core) for i, (g, b) in enumerate(zip(grads, bufs))]
    recv = _rs_all_to_all(f"rs_a2a_{tag}", parts)
    outs = [_rs_sum4(f"rs_sum4_{tag}_{i}", r, core) for i, r in enumerate(recv)]
    return _rs_share_halves(f"rs_share_{tag}", outs)


def _allreduce_small(name, v):
    rows = v.shape[0]

    def body(v_ref, o_ref, gath, send, recv):
        x, y, c = _coords()
        me = 4 * x + 2 * y + c

        def peer(kk):
            return (1 - x if kk & 4 else x, 1 - y if kk & 2 else y, 1 - c if kk & 1 else c)

        cps = []
        for kk in range(1, N_DEV):
            cp = _rcopy(v_ref, gath.at[me], send.at[kk - 1], recv.at[kk - 1], peer(kk))
            cp.start()
            cps.append(cp)
        gath[me] = v_ref[...]
        for kk in range(1, N_DEV):
            px, py, pc = peer(kk)
            land = gath.at[4 * px + 2 * py + pc]
            _rcopy(land, land, send.at[kk - 1], recv.at[kk - 1], (px, py, pc)).wait_recv()
        for cp in cps:
            cp.wait_send()
        acc = gath[0]
        for d in range(1, N_DEV):
            acc = acc + gath[d]
        o_ref[...] = acc

    return pl.pallas_call(
        body, name=name,
        in_specs=[pl.BlockSpec(memory_space=pltpu.VMEM)], out_specs=pl.BlockSpec(memory_space=pltpu.VMEM),
        out_shape=jax.ShapeDtypeStruct(v.shape, F32),
        scratch_shapes=[pltpu.VMEM((N_DEV, rows, LANES), F32), pltpu.SemaphoreType.DMA((N_DEV - 1,)), pltpu.SemaphoreType.DMA((N_DEV - 1,))],
        compiler_params=pltpu.CompilerParams(has_side_effects=True, vmem_limit_bytes=VMEM_LIMIT),
    )(v)


def _adamw(name, w, g, m, v):
    rows, cols = w.shape
    tr = _row_tile(rows, cols, 4, 7)

    def body(w_ref, g_ref, m_ref, v_ref, d_ref, nm_ref, nv_ref):
        gg = g_ref[...]
        nm = ADAM_B1 * m_ref[...] + (1.0 - ADAM_B1) * gg
        nv = ADAM_B2 * v_ref[...] + (1.0 - ADAM_B2) * (gg * gg)
        m_hat = nm / (1.0 - ADAM_B1 ** ADAM_STEP)
        v_hat = nv / (1.0 - ADAM_B2 ** ADAM_STEP)
        d_ref[...] = -ADAM_LR * (m_hat / (jnp.sqrt(v_hat) + ADAM_EPS) + ADAM_WD * w_ref[...])
        nm_ref[...] = nm
        nv_ref[...] = nv

    spec = pl.BlockSpec((tr, cols), lambda i: (i, 0))
    shp = jax.ShapeDtypeStruct((rows, cols), F32)
    return pl.pallas_call(
        body, name=name, grid=(rows // tr,), in_specs=[spec] * 4, out_specs=[spec] * 3, out_shape=[shp] * 3,
        compiler_params=_cp(("parallel",)),
    )(w, g, m, v)


def _as2d(a):
    if a.ndim == 1:
        return a.reshape(1, a.shape[0])
    return a.reshape(-1, a.shape[-1])


_WEIGHTS = ['meta_tokens', 'norm_w', 'ffn_w_gate', 'ffn_w_up', 'ffn_w_down', 'mlstm_w_in', 'mlstm_b_if', 'mlstm_norm_w',
            'mlstm_w_out', 'pool_w', 'pool_scale', 'gdn_w_in', 'gdn_conv_w', 'gdn_a_log', 'gdn_dt_bias', 'gdn_norm_w',
            'gdn_w_out', 'swa_w_qkv', 'swa_b_qkv', 'swa_sinks', 'swa_w_out', 'swa_b_out', 'final_norm_w']
_SMALL = [('meta_tokens', True), ('norm_w', True), ('pool_scale', True), ('gdn_conv_w', True), ('swa_b_qkv', True),
          ('swa_b_out', True), ('mlstm_b_if', False), ('mlstm_norm_w', False), ('gdn_a_log', False),
          ('gdn_dt_bias', False), ('gdn_norm_w', False), ('swa_sinks', False), ('final_norm_w', False)]


def _pack(vals):
    flat = jnp.concatenate([v.reshape(-1).astype(F32) for v in vals])
    n = _round_up(flat.shape[0], 8 * LANES)
    return jnp.pad(flat, (0, n - flat.shape[0])).reshape(n // LANES, LANES)


def _unpack(packed, shapes):
    flat = packed.reshape(-1)
    out, off = [], 0
    for s in shapes:
        n = int(np.prod(s))
        out.append(flat[off:off + n].reshape(s))
        off += n
    return out


def _to_chunks(a, n_meta, seq):
    pad = jnp.zeros((CHUNK - n_meta,) + a.shape[1:], a.dtype)
    return jnp.concatenate([a[:n_meta], pad, a[n_meta:n_meta + seq]], axis=0)


def _from_chunks(a, n_meta, seq, tp):
    pad = jnp.zeros((tp - n_meta - seq,) + a.shape[1:], a.dtype)
    return jnp.concatenate([a[:n_meta], a[CHUNK:CHUNK + seq], pad], axis=0)


def _col_row(g, heads):
    t = g.T
    return t[:, :, None], t.reshape(heads, -1, 1, CHUNK)


def _from_col_row(dc, dr):
    heads = dc.shape[0]
    return (dc[:, :, 0] + dr.reshape(heads, -1)).T


def kernel(x, meta_tokens, norm_w, ffn_w_gate, ffn_w_up, ffn_w_down, mlstm_w_in, mlstm_b_if, mlstm_norm_w, mlstm_w_out, pool_w, pool_scale, gdn_w_in, gdn_conv_w, gdn_a_log, gdn_dt_bias, gdn_norm_w, gdn_w_out, swa_w_qkv, swa_b_qkv, swa_sinks, swa_w_out, swa_b_out, final_norm_w, loss_target, m_meta_tokens, m_norm_w, m_ffn_w_gate, m_ffn_w_up, m_ffn_w_down, m_mlstm_w_in, m_mlstm_b_if, m_mlstm_norm_w, m_mlstm_w_out, m_pool_w, m_pool_scale, m_gdn_w_in, m_gdn_conv_w, m_gdn_a_log, m_gdn_dt_bias, m_gdn_norm_w, m_gdn_w_out, m_swa_w_qkv, m_swa_b_qkv, m_swa_sinks, m_swa_w_out, m_swa_b_out, m_final_norm_w, v_meta_tokens, v_norm_w, v_ffn_w_gate, v_ffn_w_up, v_ffn_w_down, v_mlstm_w_in, v_mlstm_b_if, v_mlstm_norm_w, v_mlstm_w_out, v_pool_w, v_pool_scale, v_gdn_w_in, v_gdn_conv_w, v_gdn_a_log, v_gdn_dt_bias, v_gdn_norm_w, v_gdn_w_out, v_swa_w_qkv, v_swa_b_qkv, v_swa_sinks, v_swa_w_out, v_swa_b_out, v_final_norm_w):
    args = locals()
    W = {n: args[n] for n in _WEIGHTS}
    M1 = {n: args["m_" + n] for n in _WEIGHTS}
    V2 = {n: args["v_" + n] for n in _WEIGHTS}

    SEQ, D = x.shape[1], x.shape[2]
    NM = meta_tokens.shape[0]
    T = NM + SEQ
    TP = _round_up(T, ROW_BLOCK)
    DEPTH = ffn_w_gate.shape[0]
    FFS = ffn_w_gate.shape[3]
    ML_H = mlstm_b_if.shape[1] // 2
    ML_DV = D // ML_H
    ML_DK = ML_DV // 2
    ML_IN = 2 * ML_H * ML_DK + 2 * D + 2 * ML_H
    ML_INP = _pad_cols(ML_IN)
    GD_DK = gdn_norm_w.shape[1]
    GD_VH = gdn_a_log.shape[1]
    GD_QH = GD_VH // 2
    GD_QKW = GD_QH * GD_DK
    GD_VW = GD_VH * GD_DK
    GD_CC = 2 * GD_QKW + GD_VW
    GD_IN = GD_CC + GD_VW + 2 * GD_VH
    GD_INP = _pad_cols(GD_IN)
    SW_HQ = swa_sinks.shape[1]
    SW_DH = D // SW_HQ
    SW_HKV = SW_HQ // SWA_GROUP
    SW_KVW = SW_HKV * SW_DH
    SW_IN = D + 2 * SW_KVW
    n_pool = len(POOL_WINDOWS)
    PG = D // n_pool

    cx, cy, cc = _coords()
    s_me = 2 * cx + cy
    core = cc.astype(jnp.int32).reshape(1)

    def my_cols(full, width):
        return lax.dynamic_slice_in_dim(full, s_me * width, width, axis=full.ndim - 1)

    big_names = ['ffn_w_gate', 'ffn_w_up', 'ffn_w_down', 'mlstm_w_in', 'mlstm_w_out', 'pool_w', 'gdn_w_in', 'gdn_w_out',
                 'swa_w_qkv', 'swa_w_out']
    big_shards = {
        'ffn_w_gate': ffn_w_gate, 'ffn_w_up': ffn_w_up, 'ffn_w_down': ffn_w_down,
        'mlstm_w_in': mlstm_w_in[0], 'mlstm_w_out': mlstm_w_out[0], 'pool_w': pool_w[0].reshape(n_pool * (PG // N_CHIPS), PG),
        'gdn_w_in': gdn_w_in[0], 'gdn_w_out': gdn_w_out[0], 'swa_w_qkv': swa_w_qkv[0], 'swa_w_out': swa_w_out[0]}
    gathered = _allgather("allgather_weights", [big_shards[n].astype(BF16) for n in big_names], True)
    G = dict(zip(big_names, gathered))
    wg_full, wu_full, wd_full = G['ffn_w_gate'], G['ffn_w_up'], G['ffn_w_down']

    def cols_full(g, pad_to=None):
        k = g.shape[1]
        full = jnp.transpose(g, (1, 0, 2)).reshape(k, -1)
        if pad_to is not None and pad_to > full.shape[1]:
            full = jnp.pad(full, ((0, 0), (0, pad_to - full.shape[1])))
        return full

    def rows_full(g):
        return g.reshape(-1, g.shape[2])

    ml_win = cols_full(G['mlstm_w_in'], ML_INP)
    ml_wout = rows_full(G['mlstm_w_out'])
    pool_full = jnp.transpose(G['pool_w'].reshape(N_CHIPS, n_pool, PG // N_CHIPS, PG), (1, 0, 2, 3)).reshape(n_pool, PG, PG)
    gd_win = cols_full(G['gdn_w_in'], GD_INP)
    gd_wout = rows_full(G['gdn_w_out'])
    sw_wqkv = cols_full(G['swa_w_qkv'])
    sw_wout = rows_full(G['swa_w_out'])

    small_sharded = [n for n, sh in _SMALL if sh]
    sm_shapes = [W[n].shape for n in small_sharded]
    sm_gath = _allgather("allgather_small", [_pack([W[n] for n in small_sharded])], False)[0]
    sm_parts = [_unpack(sm_gath[s], sm_shapes) for s in range(N_CHIPS)]
    SF = {n: jnp.concatenate([sm_parts[s][i] for s in range(N_CHIPS)], axis=-1) for i, n in enumerate(small_sharded)}
    meta_full, normw_full = SF['meta_tokens'], SF['norm_w']
    pool_scale_full, conv_full = SF['pool_scale'], SF['gdn_conv_w'][0]
    bqkv_full, bout_full = SF['swa_b_qkv'], SF['swa_b_out']

    nps = 1
    ff_tm = _tiles(TP, FFS, D, 2 * 4 + 3 * 2 * 2, n_pairs=2)[0]

    def ffn_fwd(h, li, wi, nw):
        n = _rmsnorm_fwd(f"ffn_norm_{li}_{wi}", h, nw, BF16)
        tm, tn, tk = ff_tm, FFS, _div(D, (512, 256, 128))
        bspec = lambda j, k: (j, li, wi, k, 0)
        g, u, a = _matmul(
            f"ffn_gateup_{li}_{wi}", "nn", (TP // tm, N_CHIPS, D // tk),
            [(n, (tm, tk), lambda i, j, k: (i, k))] * 2,
            [(wg_full, (None, None, None, tk, tn), lambda i, j, k: bspec(j, k)),
             (wu_full, (None, None, None, tk, tn), lambda i, j, k: bspec(j, k))],
            [0, 1], 2, [],
            [(jax.ShapeDtypeStruct((TP, N_CHIPS * FFS), BF16), (tm, tn), lambda i, j, k: (i, j))] * 3,
            lambda accs, ex: [accs[0], accs[1], _silu(accs[0]) * accs[1]], tm, tn)
        tm2, tn2, tk2 = _tiles(TP, D, FFS, 4 + 2 * 4 + 2 * 4, k_cands=(FFS,), n_cands=(1024, 512, 256, 128))
        kps = FFS // tk2
        h2 = _matmul(
            f"ffn_down_{li}_{wi}", "nn", (TP // tm2, D // tn2, N_CHIPS * kps),
            [(a, (tm2, tk2), lambda i, j, k: (i, k))],
            [(wd_full, (None, None, None, tk2, tn2), lambda i, j, k: (k // kps, li, wi, k % kps, j))],
            [0], 1, [(h, (tm2, tn2), lambda i, j, k: (i, j))],
            [(jax.ShapeDtypeStruct((TP, D), F32), (tm2, tn2), lambda i, j, k: (i, j))],
            lambda accs, ex: [ex[0] + 0.5 * accs[0]], tm2, tn2)[0]
        return h2, (h, n, g, u, a)

    def ffn_bwd(dh2, saved, li, wi, nw, gbufs):
        h, n, g, u, a = saved
        gg, gu, gd = gbufs
        slot = li * 2 + wi
        tm, tn = ff_tm, FFS
        tk = _div(D, (512, 256, 128))

        def epi(accs, ex):
            gb, ub = ex[0].astype(F32), ex[1].astype(F32)
            da = 0.5 * accs[0]
            s = _sigmoid(gb)
            return [da * ub * (s * (1.0 + gb * (1.0 - s))), da * (gb * s)]
        dg, du = _matmul(
            f"ffn_dact_{li}_{wi}", "nt", (TP // tm, N_CHIPS, D // tk),
            [(dh2, (tm, tk), lambda i, j, k: (i, k))],
            [(wd_full, (None, None, None, tn, tk), lambda i, j, k: (j, li, wi, 0, k))],
            [0], 1, [(g, (tm, tn), lambda i, j, k: (i, j)), (u, (tm, tn), lambda i, j, k: (i, j))],
            [(jax.ShapeDtypeStruct((TP, N_CHIPS * FFS), BF16), (tm, tn), lambda i, j, k: (i, j))] * 2, epi, tm, tn)
        tkr = _div(TP, (1408, 1056, 704, 384, 256, 128))
        tnd = _div(D, (1024, 512, 256, 128))
        gd = _matmul(
            f"ffn_dwd_{li}_{wi}", "tn", (N_CHIPS, D // tnd, TP // tkr),
            [(a, (tkr, FFS), lambda i, j, k: (k, i))], [(dh2, (tkr, tnd), lambda i, j, k: (k, j))],
            [0], 1, [], [(jax.ShapeDtypeStruct(gd.shape, BF16), (None, None, FFS, tnd), lambda i, j, k: (i, slot, 0, j))],
            lambda accs, ex: [0.5 * accs[0]], FFS, tnd, alias_inputs=[gd], alias_map={0: 0})[0]
        tmw = _div(D, (512, 256, 128))
        gg, gu = _matmul(
            f"ffn_dwgu_{li}_{wi}", "tn", (D // tmw, N_CHIPS, TP // tkr),
            [(n, (tkr, tmw), lambda i, j, k: (k, i))] * 2,
            [(dg, (tkr, FFS), lambda i, j, k: (k, j)), (du, (tkr, FFS), lambda i, j, k: (k, j))],
            [0, 1], 2, [],
            [(jax.ShapeDtypeStruct(gg.shape, BF16), (None, None, tmw, FFS), lambda i, j, k: (j, slot, i, 0))] * 2,
            lambda accs, ex: [accs[0], accs[1]], tmw, FFS, alias_inputs=[gg, gu], alias_map={0: 0, 1: 1})
        tm3 = _div(TP, (704, 528, 384, 256, 128))
        tn3 = _div(D, (1024, 512, 256, 128))
        dn = _matmul(
            f"ffn_dn_{li}_{wi}", "nt", (TP // tm3, D // tn3, N_CHIPS),
            [(dg, (tm3, FFS), lambda i, j, k: (i, k)), (du, (tm3, FFS), lambda i, j, k: (i, k))],
            [(wg_full, (None, None, None, tn3, FFS), lambda i, j, k: (k, li, wi, j, 0)),
             (wu_full, (None, None, None, tn3, FFS), lambda i, j, k: (k, li, wi, j, 0))],
            [0, 0], 1, [], [(jax.ShapeDtypeStruct((TP, D), F32), (tm3, tn3), lambda i, j, k: (i, j))],
            lambda accs, ex: [accs[0]], tm3, tn3)[0]
        dh, dnw = _rmsnorm_bwd(f"ffn_norm_bwd_{li}_{wi}", h, nw, dn, dh2)
        return dh, dnw, (gg, gu, gd)

    def mlstm_fwd(h, nw):
        u = _rmsnorm_fwd("mlstm_norm", h, nw, BF16)
        p = _mm("mlstm_in", "nn", u, ml_win, F32)
        pc = _to_chunks(p, NM, SEQ)
        qkw = ML_H * ML_DK
        gates = pc[:, 2 * qkw + 2 * D:2 * qkw + 2 * D + 2 * ML_H] + mlstm_b_if
        li_c, li_r = _col_row(gates[:, :ML_H], ML_H)
        lf_c, lf_r = _col_row(gates[:, ML_H:], ML_H)
        hh, cs, ns, ms = _mlstm_core_fwd(pc, li_c, lf_c, li_r, lf_r, ML_H, ML_DK, ML_DV, NM)
        hh_s = _from_chunks(hh, NM, SEQ, TP)
        og_cb = (2 * qkw + D) // D
        out = _headnorm_fwd("mlstm_post", hh_s, p, D, og_cb, mlstm_norm_w, ML_DV, _sigmoid)
        h2 = _mm("mlstm_out", "nn", out, ml_wout, F32, lambda acc, hb: hb + acc, [h], ["tile"])
        return h2, (h, u, p, pc, (li_c, lf_c, li_r, lf_r), (cs, ns, ms), hh_s, out, og_cb)

    def mlstm_bwd(dh2, saved, nw):
        h, u, p, pc, gts, sts, hh_s, out, og_cb = saved
        dout = _mm("mlstm_out_dx", "nt", dh2, ml_wout, F32)
        d_wout = _mm("mlstm_out_dw", "tn", out, dh2, BF16)
        dhh, dog, dnormw = _headnorm_bwd("mlstm_post_bwd", hh_s, p, D, og_cb, mlstm_norm_w, dout, ML_DV, _sigmoid)
        dq, dkk, dvv, d0, d1, d2, d3 = _mlstm_core_bwd(pc, *gts, *sts, _to_chunks(dhh, NM, SEQ), ML_H, ML_DK, ML_DV, NM)
        dgates = jnp.concatenate([_from_col_row(d0, d2), _from_col_row(d1, d3)], axis=1)
        dqkvg = _from_chunks(jnp.concatenate([dq, dkk, dvv], axis=1), NM, SEQ, TP)
        dgs = _from_chunks(dgates, NM, SEQ, TP)
        pad = jnp.zeros((TP, ML_INP - ML_IN), F32)
        dp = jnp.concatenate([dqkvg, dog, dgs, pad], axis=1)
        d_bif = _colsum("mlstm_dbias", jnp.pad(dgs, ((0, 0), (0, LANES - 2 * ML_H))))[:, :2 * ML_H]
        d_win = _mm("mlstm_in_dw", "tn", u, dp, BF16)[:, :ML_IN]
        du = _mm("mlstm_in_dx", "nt", dp, ml_win, F32)
        dh, dnw = _rmsnorm_bwd("mlstm_norm_bwd", h, nw, du, dh2)
        return dh, dnw, {'mlstm_w_in': d_win, 'mlstm_w_out': d_wout, 'mlstm_b_if': d_bif, 'mlstm_norm_w': dnormw}

    def pool_fwd_layer(h, nw):
        u = _rmsnorm_fwd("pool_norm", h, nw, F32)
        pooled = _pool_fwd(u)
        tm = _div(TP, (704, 528, 384, 256, 128))
        tk = _div(PG, (512, 256, 128))
        kpg = PG // tk
        h2, ypre = _matmul(
            "pool_mix", "nn", (TP // tm, n_pool, kpg),
            [(pooled, (tm, tk), lambda i, j, k: (i, j * kpg + k))],
            [(pool_full, (None, tk, PG), lambda i, j, k: (j, k, 0))],
            [0], 1, [(h, (tm, PG), lambda i, j, k: (i, j)), (pool_scale_full, (1, PG), lambda i, j, k: (0, j))],
            [(jax.ShapeDtypeStruct((TP, D), F32), (tm, PG), lambda i, j, k: (i, j))] * 2,
            lambda accs, ex: [ex[0] + accs[0] * ex[1], accs[0]], tm, PG)
        return h2, (h, pooled, ypre)

    def pool_bwd_layer(dh2, saved, nw):
        h, pooled, ypre = saved

        def fn(_, dyb, ypb, sb):
            return [dyb * sb, jnp.sum(dyb * ypb, axis=0, keepdims=True)]
        dys, dscale = _rowwise("pool_scale_bwd", fn, [_full(dh2), _full(ypre), _full(pool_scale_full)],
                               [(D, BF16, "row"), (D, F32, "acc")], ROW_BLOCK, TP)
        tm = _div(TP, (704, 528, 384, 256, 128))
        tk = _div(PG, (512, 256, 128))
        kpg = PG // tk
        dpooled = _matmul(
            "pool_mix_dx", "nt", (TP // tm, n_pool, kpg),
            [(dys, (tm, tk), lambda i, j, k: (i, j * kpg + k))],
            [(pool_full, (None, PG, tk), lambda i, j, k: (j, 0, k))],
            [0], 1, [], [(jax.ShapeDtypeStruct((TP, D), F32), (tm, PG), lambda i, j, k: (i, j))],
            lambda accs, ex: [accs[0]], tm, PG)[0]
        tkr = _div(TP, (1408, 1056, 704, 384, 256, 128))
        d_pw = _matmul(
            "pool_mix_dw", "tn", (1, n_pool, TP // tkr),
            [(pooled, (tkr, PG), lambda i, j, k: (k, j))], [(dys, (tkr, PG), lambda i, j, k: (k, j))],
            [0], 1, [], [(jax.ShapeDtypeStruct((n_pool, PG, PG), BF16), (None, PG, PG), lambda i, j, k: (j, 0, 0))],
            lambda accs, ex: [accs[0]], PG, PG)[0]
        du = _pool_bwd(dpooled)
        dh, dnw = _rmsnorm_bwd("pool_norm_bwd", h, nw, du, dh2)
        return dh, dnw, {'pool_w': d_pw, 'pool_scale': dscale}

    def gdn_fwd(h, nw):
        u = _rmsnorm_fwd("gdn_norm", h, nw, BF16)
        p = _mm("gdn_in", "nn", u, gd_win, F32)
        qkv_act = _conv_fwd(p, conv_full, GD_CC)
        b_pre = p[:, GD_CC + GD_VW:GD_CC + GD_VW + GD_VH]
        a_pre = p[:, GD_CC + GD_VW + GD_VH:GD_IN]
        g, beta = _gdn_gates_fwd(a_pre, b_pre, gdn_a_log, gdn_dt_bias)
        qkv_c = _to_chunks(qkv_act, NM, SEQ)
        g_c, g_r = _col_row(_to_chunks(g, NM, SEQ), GD_VH)
        b_c, _ = _col_row(_to_chunks(beta, NM, SEQ), GD_VH)
        o, st = _gdn_core_fwd(qkv_c, g_c, b_c, g_r, GD_QH, GD_DK, NM)
        o_s = _from_chunks(o, NM, SEQ, TP)
        nw_t = jnp.tile(gdn_norm_w, (1, GD_VH))
        z_cb = GD_CC // GD_VW
        out = _headnorm_fwd("gdn_post", o_s, p, GD_VW, z_cb, nw_t, GD_DK, _silu)
        h2 = _mm("gdn_out", "nn", out, gd_wout, F32, lambda acc, hb: hb + acc, [h], ["tile"])
        return h2, (h, u, p, qkv_c, (g_c, b_c, g_r), st, o_s, out, nw_t, z_cb, a_pre, b_pre)

    def gdn_bwd(dh2, saved, nw):
        h, u, p, qkv_c, gts, st, o_s, out, nw_t, z_cb, a_pre, b_pre = saved
        dout = _mm("gdn_out_dx", "nt", dh2, gd_wout, F32)
        d_wout = _mm("gdn_out_dw", "tn", out, dh2, BF16)
        do, dz, dnw_t = _headnorm_bwd("gdn_post_bwd", o_s, p, GD_VW, z_cb, nw_t, dout, GD_DK, _silu)
        dnormw = jnp.sum(dnw_t.reshape(GD_VH, GD_DK), axis=0, keepdims=True)
        res = _gdn_core_bwd(qkv_c, *gts, st, _to_chunks(do, NM, SEQ), GD_QH, GD_DK, NM)
        dq, dkk, dvv = res[0], res[1], res[2]
        dgc, dbc, dgr = res[3], res[4], res[5]
        dg = _from_chunks(_from_col_row(dgc, dgr), NM, SEQ, TP)
        dbeta = _from_chunks(dbc[:, :, 0].T, NM, SEQ, TP)
        da_pre, db_pre, d_alog, d_dt = _gdn_gates_bwd(a_pre, b_pre, gdn_a_log, gdn_dt_bias, dg, dbeta)
        dact = _from_chunks(jnp.concatenate([dq, dkk, dvv], axis=1), NM, SEQ, TP)
        dacc, d_conv = _conv_bwd_pre(p, conv_full, dact, GD_CC)
        dqkv_pre = _conv_bwd_dx(dacc, conv_full, GD_CC)
        pad = jnp.zeros((TP, GD_INP - GD_IN), F32)
        dp = jnp.concatenate([dqkv_pre, dz, db_pre, da_pre, pad], axis=1)
        d_win = _mm("gdn_in_dw", "tn", u, dp, BF16)[:, :GD_IN]
        du = _mm("gdn_in_dx", "nt", dp, gd_win, F32)
        dh, dnw = _rmsnorm_bwd("gdn_norm_bwd", h, nw, du, dh2)
        return dh, dnw, {'gdn_w_in': d_win, 'gdn_w_out': d_wout, 'gdn_conv_w': d_conv, 'gdn_a_log': d_alog,
                         'gdn_dt_bias': d_dt, 'gdn_norm_w': dnormw}

    inv = ROPE_THETA ** (-jnp.arange(0, SW_DH, 2, dtype=F32) / SW_DH)
    ang = jnp.arange(TP, dtype=F32)[:, None] * inv[None, :]
    ang = jnp.concatenate([ang, ang], axis=-1)
    rope_cos, rope_sin = jnp.cos(ang), jnp.sin(ang)

    def swa_split(p):
        q = jnp.transpose(p[:, :D].reshape(TP, SW_HKV, SWA_GROUP, SW_DH), (1, 2, 0, 3))
        k = jnp.transpose(p[:, D:D + SW_KVW].reshape(TP, SW_HKV, SW_DH), (1, 0, 2))
        v = jnp.transpose(p[:, D + SW_KVW:].reshape(TP, SW_HKV, SW_DH), (1, 0, 2))
        return q, k, v

    def swa_fwd(h, nw):
        u = _rmsnorm_fwd("swa_norm", h, nw, BF16)
        p = _mm("swa_in", "nn", u, sw_wqkv, F32, lambda acc, bb: acc + bb, [bqkv_full], ["row"])
        q, k, v = swa_split(p)
        sink = jnp.repeat(swa_sinks.reshape(SW_HKV, SWA_GROUP), ROW_BLOCK, axis=1)[:, :, None]
        o = _swa_core(q, k, v, rope_cos, rope_sin, sink, T)[0]
        o2 = jnp.transpose(o, (2, 0, 1, 3)).reshape(TP, D).astype(BF16)
        h2 = _mm("swa_out", "nn", o2, sw_wout, F32, lambda acc, hb, bb: hb + acc + bb, [h, bout_full], ["tile", "row"])
        return h2, (h, u, q, k, v, sink, o2)

    def swa_bwd(dh2, saved, nw):
        h, u, q, k, v, sink, o2 = saved
        do = _mm("swa_out_dx", "nt", dh2, sw_wout, F32)
        d_wout = _mm("swa_out_dw", "tn", o2, dh2, BF16)
        d_bout = _colsum("swa_dbout", dh2)
        do4 = jnp.transpose(do.reshape(TP, SW_HKV, SWA_GROUP, SW_DH), (1, 2, 0, 3))
        dq, dkp, dkc, dvp, dvc, dsink = _swa_core(q, k, v, rope_cos, rope_sin, sink, T, do=do4)
        shift = lambda a: jnp.concatenate([a[:, ROW_BLOCK:], jnp.zeros_like(a[:, :ROW_BLOCK])], axis=1)
        dk = dkc + shift(dkp)
        dv = dvc + shift(dvp)
        dp = jnp.concatenate([jnp.transpose(dq, (2, 0, 1, 3)).reshape(TP, D),
                              jnp.transpose(dk, (1, 0, 2)).reshape(TP, SW_KVW),
                              jnp.transpose(dv, (1, 0, 2)).reshape(TP, SW_KVW)], axis=1)
        d_sinks = jnp.sum(dsink.reshape(SW_HKV, SWA_GROUP, ROW_BLOCK), axis=2).reshape(1, SW_HQ)
        d_bqkv = _colsum("swa_dbqkv", dp)
        d_wqkv = _mm("swa_in_dw", "tn", u, dp, BF16)
        du = _mm("swa_in_dx", "nt", dp, sw_wqkv, F32)
        dh, dnw = _rmsnorm_bwd("swa_norm_bwd", h, nw, du, dh2)
        return dh, dnw, {'swa_w_qkv': d_wqkv, 'swa_w_out': d_wout, 'swa_b_qkv': d_bqkv, 'swa_b_out': d_bout,
                         'swa_sinks': d_sinks}

    mixers_fwd = [mlstm_fwd, pool_fwd_layer, gdn_fwd, swa_fwd]
    mixers_bwd = [mlstm_bwd, pool_bwd_layer, gdn_bwd, swa_bwd]

    h = jnp.concatenate([meta_full, x[0], jnp.zeros((TP - T, D), F32)], axis=0)
    saved = []
    for li in range(DEPTH):
        nws = [normw_full[li, t].reshape(1, D) for t in range(3)]
        h, s0 = ffn_fwd(h, li, 0, nws[0])
        h, s1 = mixers_fwd[li % 4](h, nws[1])
        h, s2 = ffn_fwd(h, li, 1, nws[2])
        saved.append((s0, s1, s2, nws))

    tgt = jnp.concatenate([jnp.zeros((NM, D), F32), loss_target[0], jnp.zeros((TP - T, D), F32)], axis=0)
    dh, d_final_w, loss_vec = _loss_head("loss_head", h, final_norm_w.reshape(1, D), tgt, NM, SEQ)

    slots = DEPTH * 2
    gbufs = (lax.empty((N_CHIPS, slots, D, FFS), BF16), lax.empty((N_CHIPS, slots, D, FFS), BF16),
             lax.empty((N_CHIPS, slots, FFS, D), BF16))
    d_normw = [[None] * 3 for _ in range(DEPTH)]
    GR = {}
    for li in reversed(range(DEPTH)):
        s0, s1, s2, nws = saved[li]
        dh, d_normw[li][2], gbufs = ffn_bwd(dh, s2, li, 1, nws[2], gbufs)
        dh, d_normw[li][1], gm = mixers_bwd[li % 4](dh, s1, nws[1])
        GR.update(gm)
        dh, d_normw[li][0], gbufs = ffn_bwd(dh, s0, li, 0, nws[0], gbufs)
    grad_x = dh[NM:NM + SEQ][None]
    GR['meta_tokens'] = dh[:NM]
    GR['norm_w'] = jnp.stack([jnp.concatenate(r, axis=0) for r in d_normw], axis=0)
    GR['final_norm_w'] = d_final_w

    def col_shards(g, w):
        return jnp.transpose(g.reshape(g.shape[0], N_CHIPS, w), (1, 0, 2))

    def row_shards(g):
        return g.reshape(N_CHIPS, -1, g.shape[1])

    big_grads = [
        gbufs[0].reshape(N_CHIPS, slots * D, FFS), gbufs[1].reshape(N_CHIPS, slots * D, FFS),
        gbufs[2].reshape(N_CHIPS, slots * FFS, D),
        col_shards(GR['mlstm_w_in'], ML_IN // N_CHIPS), row_shards(GR['mlstm_w_out']),
        jnp.transpose(GR['pool_w'].reshape(n_pool, N_CHIPS, PG // N_CHIPS, PG), (1, 0, 2, 3)).reshape(N_CHIPS, -1, PG),
        col_shards(GR['gdn_w_in'], GD_IN // N_CHIPS), row_shards(GR['gdn_w_out']),
        col_shards(GR['swa_w_qkv'], SW_IN // N_CHIPS), row_shards(GR['swa_w_out'])]
    summed = _reduce_scatter("w", big_grads, core)
    grads = {n: s.reshape(W[n].shape) for n, s in zip(big_names, summed)}

    small_names = [n for n, _ in _SMALL]
    small_full_shapes = [GR[n].shape for n in small_names]
    packed = _pack([GR[n] for n in small_names] + [loss_vec[:, :1]])
    red = _allreduce_small("allreduce_small", packed)
    parts = _unpack(red, small_full_shapes + [(1, 1)])
    loss = parts[-1].reshape(())
    for (n, sharded), full in zip(_SMALL, parts[:-1]):
        full = full.reshape(W[n].shape[:-1] + (-1,))
        grads[n] = my_cols(full, W[n].shape[-1]) if sharded else full

    delta, new_m, new_v = {}, {}, {}
    for n in _WEIGHTS:
        shp = W[n].shape
        d, nm, nv = _adamw(f"adamw_{n}", _as2d(W[n]), _as2d(grads[n]), _as2d(M1[n]), _as2d(V2[n]))
        delta[n], new_m[n], new_v[n] = d.reshape(shp), nm.reshape(shp), nv.reshape(shp)

    return (loss, grad_x, *[grads[n] for n in _WEIGHTS], *[delta[n] for n in _WEIGHTS],
            *[new_m[n] for n in _WEIGHTS], *[new_v[n] for n in _WEIGHTS])
```

```python
import functools
import math

import jax
import jax.numpy as jnp
import numpy as np
from jax import lax
from jax.experimental import pallas as pl
from jax.experimental.pallas import tpu as pltpu

F32 = jnp.float32
BF16 = jnp.bfloat16
MESH = pl.DeviceIdType.MESH

EPS = 1e-6
CHUNK = 64
ROW_BLOCK = 128
SWA_WINDOW = 128
SWA_GROUP = 8
POOL_WINDOWS = (2, 4, 8, 16)
GDN_CONV = 4
ROPE_THETA = 10000.0
NEG = -1e30
N_CHIPS = 4
N_DEV = 8
LANES = 128
WIDE_TILE = 896
VMEM_LIMIT = 56 * 1024 * 1024
VMEM_BUDGET = 36 * 1024 * 1024

ADAM_LR = 0.001
ADAM_B1 = 0.9
ADAM_B2 = 0.999
ADAM_EPS = 1e-08
ADAM_WD = 0.01
ADAM_STEP = 10

_NN = ((1,), (0,))
_NT = ((1,), (1,))
_TN = ((0,), (0,))


def _cp(dims=None):
    return pltpu.CompilerParams(dimension_semantics=dims, vmem_limit_bytes=VMEM_LIMIT)


def _round_up(n, m):
    return -(-n // m) * m


def _div(n, cands):
    for c in cands:
        if c <= n and n % c == 0:
            return c
    return n


def _pad_cols(n):
    return _round_up(n, WIDE_TILE) if n > 2048 else _round_up(n, LANES)


def _dg(a, b, dims, prec=None, batched=False):
    if batched:
        dims = (((dims[0][0] + 1,), (dims[1][0] + 1,)), ((0,), (0,)))
    else:
        dims = (dims, ((), ()))
    return lax.dot_general(a, b, dims, precision=prec, preferred_element_type=F32)


def _make_dots(cast, batched=False):
    prec = None if cast is not None else lax.Precision.HIGHEST
    c = (lambda t: t.astype(cast)) if cast is not None else (lambda t: t)
    rnn = lambda a, b: _dg(c(a), c(b), _NN, prec, batched)
    rnt = lambda a, b: _dg(c(a), c(b), _NT, prec, batched)
    rtn = lambda a, b: _dg(c(a), c(b), _TN, prec, batched)

    @jax.custom_vjp
    def nn(a, b):
        return rnn(a, b)
    nn.defvjp(lambda a, b: (rnn(a, b), (a, b)), lambda r, ct: (rnt(ct, r[1]), rtn(r[0], ct)))

    @jax.custom_vjp
    def nt(a, b):
        return rnt(a, b)
    nt.defvjp(lambda a, b: (rnt(a, b), (a, b)), lambda r, ct: (rnn(ct, r[1]), rtn(ct, r[0])))

    @jax.custom_vjp
    def tn(a, b):
        return rtn(a, b)
    tn.defvjp(lambda a, b: (rtn(a, b), (a, b)), lambda r, ct: (rnt(r[1], ct), rnn(r[0], ct)))
    return nn, nt, tn, rnn, rnt, rtn


_bnn, _bnt, _btn, _rbnn, _rbnt, _rbtn = _make_dots(BF16)
_hnn, _hnt, _htn, _rhnn, _rhnt, _rhtn = _make_dots(None)
_qnn, _qnt, _qtn, _rqnn, _rqnt, _rqtn = _make_dots(BF16, batched=True)


def _sigmoid(x):
    return 1.0 / (1.0 + jnp.exp(-x))


def _silu(x):
    return x * _sigmoid(x)


def _softplus(x):
    return jnp.maximum(x, 0.0) + jnp.log(1.0 + jnp.exp(-jnp.abs(x)))


def _log_sigmoid(x):
    return -_softplus(-x)


def _iota(shape, dim):
    return lax.broadcasted_iota(jnp.int32, shape, dim)


def _matmul(name, form, grid, a_ops, b_ops, acc_ids, n_acc, extras, outs, epilogue, tm, tn,
            alias_inputs=(), alias_map=None):
    na, nb, ne, nal, no = len(a_ops), len(b_ops), len(extras), len(alias_inputs), len(outs)
    nk = grid[2]
    dims = {"nn": _NN, "nt": _NT, "tn": _TN}[form]

    def body(*refs):
        a_refs = refs[:na]
        b_refs = refs[na:na + nb]
        e_refs = refs[na + nb:na + nb + ne]
        o_refs = refs[na + nb + ne + nal:na + nb + ne + nal + no]
        acc = refs[-1]
        k = pl.program_id(2)

        @pl.when(k == 0)
        def _():
            acc[...] = jnp.zeros_like(acc)

        for p in range(na):
            acc[acc_ids[p]] += _dg(a_refs[p][...].astype(BF16), b_refs[p][...].astype(BF16), dims)

        @pl.when(k == nk - 1)
        def _():
            res = epilogue([acc[i] for i in range(n_acc)], [e[...] for e in e_refs])
            for o, r in zip(o_refs, res):
                o[...] = r.astype(o.dtype)

    ops = list(a_ops) + list(b_ops) + list(extras)
    in_specs = [pl.BlockSpec(bs, im) for (_, bs, im) in ops] + [pl.BlockSpec(memory_space=pl.ANY)] * nal
    aliases = {}
    if alias_map:
        aliases = {len(ops) + i: o for i, o in alias_map.items()}
    res = pl.pallas_call(
        body, name=name, grid=grid,
        in_specs=in_specs,
        out_specs=[pl.BlockSpec(bs, im) for (_, bs, im) in outs],
        out_shape=[s for (s, _, _) in outs],
        scratch_shapes=[pltpu.VMEM((n_acc, tm, tn), F32)],
        input_output_aliases=aliases,
        compiler_params=_cp(("parallel", "parallel", "arbitrary")),
    )(*[o[0] for o in ops], *alias_inputs)
    return res


def _tiles(M, N, K, fixed_bytes_per_tm_tn, k_cands=(512, 384, 256, 128), n_cands=(2048, 1792, 1408, 1280, 1024, 896, 768, 640, 512, 384, 256, 128),
           m_cands=(1408, 1056, 704, 528, 384, 256, 128, 64, 32, 16, 8), a_bytes=2, b_bytes=2, n_pairs=1):
    tn = _div(N, n_cands)
    tk = _div(K, k_cands)
    for tm in m_cands:
        if tm > M or M % tm:
            continue
        est = tm * tn * fixed_bytes_per_tm_tn + n_pairs * 2 * (tm * tk * a_bytes + tk * tn * b_bytes)
        if est <= VMEM_BUDGET:
            return tm, tn, tk
    return _div(M, (8,)), tn, tk


def _mm(name, form, a, b, out_dtype, epilogue=None, extras=(), extra_kinds=(), n_out=1, out_dtypes=None):
    if form == "nn":
        (M, K), N = a.shape, b.shape[1]
    elif form == "nt":
        (M, K), N = a.shape, b.shape[0]
    else:
        (K, M), N = a.shape, b.shape[1]
    out_dtypes = out_dtypes or [out_dtype] * n_out
    per = 4 + sum(2 * jnp.dtype(d).itemsize for d in out_dtypes)
    per += sum(2 * e.dtype.itemsize for e, kd in zip(extras, extra_kinds) if kd == "tile")
    kc = (1408, 1056, 704, 512, 384, 256, 128) if form == "tn" else (896, 512, 384, 256, 128)
    mc = (1024, 896, 768, 640, 512, 384, 256, 128) if form == "tn" else (1408, 1056, 704, 528, 384, 256, 128, 64, 32, 16, 8)
    tm, tn, tk = _tiles(M, N, K, per, k_cands=kc, m_cands=mc, a_bytes=a.dtype.itemsize, b_bytes=b.dtype.itemsize)
    grid = (M // tm, N // tn, K // tk)
    if form == "nn":
        a_op = (a, (tm, tk), lambda i, j, k: (i, k))
        b_op = (b, (tk, tn), lambda i, j, k: (k, j))
    elif form == "nt":
        a_op = (a, (tm, tk), lambda i, j, k: (i, k))
        b_op = (b, (tn, tk), lambda i, j, k: (j, k))
    else:
        a_op = (a, (tk, tm), lambda i, j, k: (k, i))
        b_op = (b, (tk, tn), lambda i, j, k: (k, j))
    e_ops = []
    for e, kd in zip(extras, extra_kinds):
        if kd == "tile":
            e_ops.append((e, (tm, tn), lambda i, j, k: (i, j)))
        else:
            e_ops.append((e, (1, tn), lambda i, j, k: (0, j)))
    outs = [(jax.ShapeDtypeStruct((M, N), d), (tm, tn), lambda i, j, k: (i, j)) for d in out_dtypes]

    def epi(accs, ex):
        if epilogue is None:
            return [accs[0]]
        r = epilogue(accs[0], *ex)
        return list(r) if isinstance(r, (tuple, list)) else [r]

    res = _matmul(name, form, grid, [a_op], [b_op], [0], 1, e_ops, outs, epi, tm, tn)
    return res[0] if len(res) == 1 else res


def _rowwise(name, fn, ins, outs, tr, rows):
    n_in, n_out = len(ins), len(outs)
    nblk = rows // tr

    def body(*refs):
        i = pl.program_id(0)
        vals = fn(i * tr, *[r[...] for r in refs[:n_in]])
        for o, v, (_, _, kind) in zip(refs[n_in:], vals, outs):
            if kind == "row":
                o[...] = v.astype(o.dtype)
            else:
                @pl.when(i == 0)
                def _(o=o):
                    o[...] = jnp.zeros_like(o)
                o[...] += v.astype(o.dtype)

    in_specs = []
    for arr, w, cb in ins:
        if arr.shape[0] == 1 and rows != 1:
            in_specs.append(pl.BlockSpec((1, w), lambda i, cb=cb: (0, cb)))
        else:
            in_specs.append(pl.BlockSpec((tr, w), lambda i, cb=cb: (i, cb)))
    out_specs, out_shape = [], []
    for w, d, kind in outs:
        if kind == "row":
            out_specs.append(pl.BlockSpec((tr, w), lambda i: (i, 0)))
            out_shape.append(jax.ShapeDtypeStruct((rows, w), d))
        else:
            out_specs.append(pl.BlockSpec((1, w), lambda i: (0, 0)))
            out_shape.append(jax.ShapeDtypeStruct((1, w), d))
    return pl.pallas_call(
        body, name=name, grid=(nblk,), in_specs=in_specs, out_specs=out_specs, out_shape=out_shape,
        compiler_params=_cp(("arbitrary",)),
    )(*[a for a, _, _ in ins])


def _full(arr):
    return (arr, arr.shape[1], 0)


def _rmsnorm_fwd(name, h, w, out_dtype):
    D = h.shape[1]

    def fn(_, hb, wb):
        rstd = lax.rsqrt(jnp.mean(hb * hb, axis=1, keepdims=True) + EPS)
        return [hb * rstd * wb]
    return _rowwise(name, fn, [_full(h), _full(w)], [(D, out_dtype, "row")], ROW_BLOCK, h.shape[0])[0]


def _rmsnorm_bwd(name, h, w, dn, dh_in):
    D = h.shape[1]

    def fn(_, hb, wb, dnb, dhb):
        rstd = lax.rsqrt(jnp.mean(hb * hb, axis=1, keepdims=True) + EPS)
        xhat = hb * rstd
        dxh = dnb.astype(F32) * wb
        dh = rstd * (dxh - xhat * jnp.mean(dxh * xhat, axis=1, keepdims=True))
        return [dhb + dh, jnp.sum(dnb.astype(F32) * xhat, axis=0, keepdims=True)]
    return _rowwise(name, fn, [_full(h), _full(w), _full(dn), _full(dh_in)],
                    [(D, F32, "row"), (D, F32, "acc")], ROW_BLOCK, h.shape[0])


def _headnorm_fn(group, act):
    def f(o, gate, w):
        rstd = lax.rsqrt(jnp.mean(o * o, axis=1, keepdims=True) + EPS)
        return o * rstd * w * act(gate)
    return f


def _headnorm_fwd(name, o, gate_arr, gate_w, gate_cb, w, group, act):
    N = o.shape[1]
    f = _headnorm_fn(group, act)

    def fn(_, ob, gb, wb):
        parts = [f(ob[:, s:s + group], gb[:, s:s + group], wb[:, s:s + group]) for s in range(0, N, group)]
        return [jnp.concatenate(parts, axis=1)]
    return _rowwise(name, fn, [_full(o), (gate_arr, gate_w, gate_cb), _full(w)], [(N, BF16, "row")], ROW_BLOCK, o.shape[0])[0]


def _headnorm_bwd(name, o, gate_arr, gate_w, gate_cb, w, dout, group, act):
    N = o.shape[1]
    f = _headnorm_fn(group, act)

    def fn(_, ob, gb, wb, db):
        dos, dgs, dws = [], [], []
        for s in range(0, N, group):
            _, vjp = jax.vjp(f, ob[:, s:s + group], gb[:, s:s + group], jnp.broadcast_to(wb[:, s:s + group], (ob.shape[0], group)))
            do, dgt, dw = vjp(db[:, s:s + group])
            dos.append(do)
            dgs.append(dgt)
            dws.append(jnp.sum(dw, axis=0, keepdims=True))
        return [jnp.concatenate(dos, axis=1), jnp.concatenate(dgs, axis=1), jnp.concatenate(dws, axis=1)]
    return _rowwise(name, fn, [_full(o), (gate_arr, gate_w, gate_cb), _full(w), _full(dout)],
                    [(N, F32, "row"), (N, F32, "row"), (N, F32, "acc")], ROW_BLOCK, o.shape[0])


def _colsum(name, a):
    def fn(_, ab):
        return [jnp.sum(ab.astype(F32), axis=0, keepdims=True)]
    tr = _div(a.shape[0], (512, 384, 256, 128, 64))
    return _rowwise(name, fn, [_full(a)], [(a.shape[1], F32, "acc")], tr, a.shape[0])[0]


def _loss_head(name, h, w, tgt, n_meta, seq):
    D = h.shape[1]
    tr = ROW_BLOCK

    def fn(row0, hb, wb, tb):
        row = row0 + _iota((tr, 1), 0)
        valid = (row >= n_meta) & (row < n_meta + seq)
        rstd = lax.rsqrt(jnp.mean(hb * hb, axis=1, keepdims=True) + EPS)
        xhat = hb * rstd
        err = jnp.where(valid, xhat * wb - tb, 0.0)
        loss = 0.5 * jnp.sum(jnp.mean(err * err, axis=1, keepdims=True), axis=0, keepdims=True)
        dy = err * (1.0 / D)
        dxh = dy * wb
        dh = rstd * (dxh - xhat * jnp.mean(dxh * xhat, axis=1, keepdims=True))
        return [dh, jnp.sum(dy * xhat, axis=0, keepdims=True), jnp.broadcast_to(loss, (1, LANES))]
    return _rowwise(name, fn, [_full(h), _full(w), _full(tgt)],
                    [(D, F32, "row"), (D, F32, "acc"), (LANES, F32, "acc")], tr, h.shape[0])


def _chunk_valid(ci, n_meta):
    lim = jnp.where(ci == 0, n_meta, CHUNK)
    return _iota((CHUNK, 1), 0) < lim, _iota((1, CHUNK), 1) < lim


def _tri_masks():
    r = _iota((CHUNK, CHUNK), 0)
    c = _iota((CHUNK, CHUNK), 1)
    return r >= c, r > c, r <= c


def _mlstm_chunk(vc, vr, m_st, c_st, n_st, q, k, v, li_c, lf_c, li_r, lf_r):
    tril, _, triu = _tri_masks()
    dk = q.shape[1]
    li_c = jnp.where(vc, li_c, NEG)
    li_r = jnp.where(vr, li_r, NEG)
    lf_c = jnp.where(vc, _log_sigmoid(lf_c), 0.0)
    lf_r = jnp.where(vr, _log_sigmoid(lf_r), 0.0)
    b_c = jnp.sum(jnp.where(tril, lf_r, 0.0), axis=1, keepdims=True)
    b_r = jnp.sum(jnp.where(triu, lf_c, 0.0), axis=0, keepdims=True)
    b_last = jnp.sum(lf_r, axis=1, keepdims=True)
    log_w = jnp.where(tril, b_c - b_r + li_r, NEG)
    log_init = b_c + m_st
    m_t = lax.stop_gradient(jnp.maximum(log_init, jnp.max(log_w, axis=1, keepdims=True)))
    w = jnp.exp(log_w - m_t)
    w_init = jnp.exp(log_init - m_t)
    qs = q * (dk ** -0.5)
    qk = _bnt(qs, k) * w
    num = w_init * _bnn(qs, c_st) + _bnn(qk, v)
    den = w_init * jnp.sum(qs * n_st, axis=1, keepdims=True) + jnp.sum(qk, axis=1, keepdims=True)
    h = num / jnp.maximum(jnp.abs(den), jnp.exp(-m_t))
    log_end_init = b_last + m_st
    log_end_r = b_last - b_r + li_r
    m_new = lax.stop_gradient(jnp.maximum(log_end_init, jnp.max(log_end_r, axis=1, keepdims=True)))
    a_init = jnp.exp(log_end_init - m_new)
    a_c = jnp.exp(b_last - b_c + li_c - m_new)
    ka = k * a_c
    c_new = a_init * c_st + _btn(ka, v)
    n_new = a_init * n_st + jnp.sum(ka, axis=0, keepdims=True)
    return (c_new, n_new, h), m_new


HEADS_PER_STEP = 4


def _head_batch(heads):
    return _div(heads, (HEADS_PER_STEP, 2, 1))


def _mlstm_io_specs(heads, hb, dk, dv, NC, rev):
    ci = (lambda c: NC - 1 - c) if rev else (lambda c: c)
    nb = heads // hb
    q = pl.BlockSpec((CHUNK, hb * dk), lambda h, c: (ci(c), h))
    k = pl.BlockSpec((CHUNK, hb * dk), lambda h, c: (ci(c), nb + h))
    v = pl.BlockSpec((CHUNK, hb * dv), lambda h, c: (ci(c), (2 * heads * dk) // (hb * dv) + h))
    col = pl.BlockSpec((hb, CHUNK, 1), lambda h, c: (h, ci(c), 0))
    row = pl.BlockSpec((hb, 1, 1, CHUNK), lambda h, c: (h, ci(c), 0, 0))
    st = [pl.BlockSpec((hb, 1, dk, dv), lambda h, c: (h, ci(c), 0, 0)),
          pl.BlockSpec((hb, 1, 1, dk), lambda h, c: (h, ci(c), 0, 0)),
          pl.BlockSpec((hb, 1, 1, 1), lambda h, c: (h, ci(c), 0, 0))]
    wide = pl.BlockSpec((CHUNK, hb * dv), lambda h, c: (ci(c), h))
    return q, k, v, col, row, st, wide


def _mlstm_core_fwd(pc, li_c, lf_c, li_r, lf_r, heads, dk, dv, n_meta):
    TC = pc.shape[0]
    NC = TC // CHUNK
    hb = _head_batch(heads)

    def body(q_ref, k_ref, v_ref, lic, lfc, lir, lfr, h_ref, cs_ref, ns_ref, ms_ref, c_s, n_s, m_s):
        ci = pl.program_id(1)

        @pl.when(ci == 0)
        def _():
            c_s[...] = jnp.zeros_like(c_s)
            n_s[...] = jnp.zeros_like(n_s)
            m_s[...] = jnp.zeros_like(m_s)

        vc, vr = _chunk_valid(ci, n_meta)
        for j in range(hb):
            cs_ref[j, 0] = c_s[j]
            ns_ref[j, 0] = n_s[j]
            ms_ref[j, 0] = m_s[j]
            (c_new, n_new, h), m_new = _mlstm_chunk(
                vc, vr, m_s[j], c_s[j], n_s[j], q_ref[:, j * dk:(j + 1) * dk], k_ref[:, j * dk:(j + 1) * dk],
                v_ref[:, j * dv:(j + 1) * dv], lic[j], lfc[j], lir[j, 0], lfr[j, 0])
            h_ref[:, j * dv:(j + 1) * dv] = h
            c_s[j] = c_new
            n_s[j] = n_new
            m_s[j] = m_new

    q, k, v, col, row, st, wide = _mlstm_io_specs(heads, hb, dk, dv, NC, False)
    return pl.pallas_call(
        body, name="mlstm_core_fwd", grid=(heads // hb, NC),
        in_specs=[q, k, v, col, col, row, row],
        out_specs=[wide] + st,
        out_shape=[jax.ShapeDtypeStruct((TC, heads * dv), F32),
                   jax.ShapeDtypeStruct((heads, NC, dk, dv), F32),
                   jax.ShapeDtypeStruct((heads, NC, 1, dk), F32),
                   jax.ShapeDtypeStruct((heads, NC, 1, 1), F32)],
        scratch_shapes=[pltpu.VMEM((hb, dk, dv), F32), pltpu.VMEM((hb, 1, dk), F32), pltpu.VMEM((hb, 1, 1), F32)],
        compiler_params=_cp(("parallel", "arbitrary")),
    )(pc, pc, pc, li_c, lf_c, li_r, lf_r)


def _mlstm_core_bwd(pc, li_c, lf_c, li_r, lf_r, cs, ns, ms, dh, heads, dk, dv, n_meta):
    TC = pc.shape[0]
    NC = TC // CHUNK
    hb = _head_batch(heads)

    def body(q_ref, k_ref, v_ref, lic, lfc, lir, lfr, cs_ref, ns_ref, ms_ref, dh_ref,
             dq_ref, dk_ref, dv_ref, dlic, dlfc, dlir, dlfr, dc_s, dn_s):
        step = pl.program_id(1)
        ci = NC - 1 - step

        @pl.when(step == 0)
        def _():
            dc_s[...] = jnp.zeros_like(dc_s)
            dn_s[...] = jnp.zeros_like(dn_s)

        vc, vr = _chunk_valid(ci, n_meta)
        for j in range(hb):
            ks, vs = slice(j * dk, (j + 1) * dk), slice(j * dv, (j + 1) * dv)
            m_st = ms_ref[j, 0]
            fn = lambda *a, m_st=m_st: _mlstm_chunk(vc, vr, m_st, *a)
            _, vjp, _ = jax.vjp(fn, cs_ref[j, 0], ns_ref[j, 0], q_ref[:, ks], k_ref[:, ks], v_ref[:, vs],
                                lic[j], lfc[j], lir[j, 0], lfr[j, 0], has_aux=True)
            dc, dn, dq, dkk, dvv, g0, g1, g2, g3 = vjp((dc_s[j], dn_s[j], dh_ref[:, vs]))
            dq_ref[:, ks] = dq
            dk_ref[:, ks] = dkk
            dv_ref[:, vs] = dvv
            dlic[j] = g0
            dlfc[j] = g1
            dlir[j, 0] = g2
            dlfr[j, 0] = g3
            dc_s[j] = dc
            dn_s[j] = dn

    q, k, v, col, row, st, wide = _mlstm_io_specs(heads, hb, dk, dv, NC, True)
    return pl.pallas_call(
        body, name="mlstm_core_bwd", grid=(heads // hb, NC),
        in_specs=[q, k, v, col, col, row, row] + st + [wide],
        out_specs=[q, q, wide, col, col, row, row],
        out_shape=[jax.ShapeDtypeStruct((TC, heads * dk), F32), jax.ShapeDtypeStruct((TC, heads * dk), F32),
                   jax.ShapeDtypeStruct((TC, heads * dv), F32),
                   jax.ShapeDtypeStruct(li_c.shape, F32), jax.ShapeDtypeStruct(lf_c.shape, F32),
                   jax.ShapeDtypeStruct(li_r.shape, F32), jax.ShapeDtypeStruct(lf_r.shape, F32)],
        scratch_shapes=[pltpu.VMEM((hb, dk, dv), F32), pltpu.VMEM((hb, 1, dk), F32)],
        compiler_params=_cp(("parallel", "arbitrary")),
    )(pc, pc, pc, li_c, lf_c, li_r, lf_r, cs, ns, ms, dh)


@jax.custom_vjp
def _tri_solve(low, rhs):
    return _tri_solve_fwd(low, rhs)[0]


def _tri_solve_fwd(low, rhs):
    levels = int(math.log2(low.shape[-1]))
    p = -low
    r = p
    for i in range(levels):
        if i > 0:
            r = r + p + _rqnn(p, r)
        if i < levels - 1:
            p = _rqnn(p, p)
    sol = rhs + _rqnn(r, rhs)
    return sol, (r, sol)


def _tri_solve_bwd(res, ct):
    r, sol = res
    d_rhs = ct + _rqtn(r, ct)
    return -_rqnt(d_rhs, sol), d_rhs


_tri_solve.defvjp(_tri_solve_fwd, _tri_solve_bwd)


def _l2norm(x):
    return x * lax.rsqrt(jnp.sum(x * x, axis=-1, keepdims=True) + EPS)


def _lane_heads(ref, n, width):
    return jnp.stack([ref[:, j * width:(j + 1) * width] for j in range(n)], axis=0)


def _gdn_chunk(vc, vr, s_st, q, k, v, g_c, b_c, g_r):
    tril, strict, triu = _tri_masks()
    dk = q.shape[-1]
    pair = lambda t: jnp.concatenate([t[j:j + 1] for j in range(t.shape[0]) for _ in (0, 1)], axis=0)
    qn = _l2norm(q) * (dk ** -0.5)
    kn = _l2norm(k)
    qk = pair(_qnt(qn, kn))
    qn, kn = pair(qn), pair(kn)
    g_c = jnp.where(vc, g_c, 0.0)
    g_r = jnp.where(vr, g_r, 0.0)
    b_c = jnp.where(vc, b_c, 0.0)
    gc_c = jnp.sum(jnp.where(tril, g_r, 0.0), axis=2, keepdims=True)
    gc_r = jnp.sum(jnp.where(triu, g_c, 0.0), axis=1, keepdims=True)
    g_last = jnp.sum(g_r, axis=2, keepdims=True)
    decay = jnp.exp(jnp.where(tril, gc_c - gc_r, NEG))
    kb = kn * b_c
    low = jnp.where(strict, _qnt(kb, kn) * decay, 0.0)
    eg = jnp.exp(gc_c)
    sol = _tri_solve(low, jnp.concatenate([v * b_c, kb * eg], axis=2))
    u_vec, w_vec = sol[:, :, :dk], sol[:, :, dk:]
    v_new = u_vec - _qnn(w_vec, s_st)
    o = _qnn(qn * eg, s_st) + _qnn(qk * decay, v_new)
    s_new = jnp.exp(g_last) * s_st + _qtn(kn * jnp.exp(g_last - gc_c), v_new)
    return s_new, o


def _gdn_core_fwd(qkv, g_c, b_c, g_r, qk_heads, dk, n_meta):
    TC = qkv.shape[0]
    NC = TC // CHUNK
    H = qk_heads

    hb = _head_batch(H)

    def body(q_ref, k_ref, v_ref, gc, bc, gr, o_ref, st_ref, s_s):
        ci = pl.program_id(1)

        @pl.when(ci == 0)
        def _():
            s_s[...] = jnp.zeros_like(s_s)

        vc, vr = _chunk_valid(ci, n_meta)
        s_st = s_s[...]
        st_ref[:, 0] = s_st.reshape(hb, 2, dk, dk)
        s_new, o = _gdn_chunk(vc, vr, s_st, _lane_heads(q_ref, hb, dk), _lane_heads(k_ref, hb, dk),
                              _lane_heads(v_ref, 2 * hb, dk), gc[...], bc[...], gr[:, 0])
        for b in range(2 * hb):
            o_ref[:, b * dk:(b + 1) * dk] = o[b]
        s_s[...] = s_new

    q, k, v, col, row, st = _gdn_io_specs(H, hb, dk, NC, False)
    return pl.pallas_call(
        body, name="gdn_core_fwd", grid=(H // hb, NC),
        in_specs=[q, k, v, col, col, row],
        out_specs=[_gdn_wide_spec(hb, dk, NC, False), st],
        out_shape=[jax.ShapeDtypeStruct((TC, 2 * H * dk), F32), jax.ShapeDtypeStruct((H, NC, 2, dk, dk), F32)],
        scratch_shapes=[pltpu.VMEM((2 * hb, dk, dk), F32)],
        compiler_params=_cp(("parallel", "arbitrary")),
    )(qkv, qkv, qkv, g_c, b_c, g_r)


def _gdn_wide_spec(hb, dk, NC, rev):
    ci = (lambda c: NC - 1 - c) if rev else (lambda c: c)
    return pl.BlockSpec((CHUNK, 2 * hb * dk), lambda h, c: (ci(c), h))


def _gdn_io_specs(H, hb, dk, NC, rev):
    ci = (lambda c: NC - 1 - c) if rev else (lambda c: c)
    nb = H // hb
    q = pl.BlockSpec((CHUNK, hb * dk), lambda h, c: (ci(c), h))
    k = pl.BlockSpec((CHUNK, hb * dk), lambda h, c: (ci(c), nb + h))
    v = pl.BlockSpec((CHUNK, 2 * hb * dk), lambda h, c: (ci(c), nb + h))
    col = pl.BlockSpec((2 * hb, CHUNK, 1), lambda h, c: (h, ci(c), 0))
    row = pl.BlockSpec((2 * hb, 1, 1, CHUNK), lambda h, c: (h, ci(c), 0, 0))
    st = pl.BlockSpec((hb, 1, 2, dk, dk), lambda h, c: (h, ci(c), 0, 0, 0))
    return q, k, v, col, row, st


def _gdn_core_bwd(qkv, g_c, b_c, g_r, st, do, qk_heads, dk, n_meta):
    TC = qkv.shape[0]
    NC = TC // CHUNK
    H = qk_heads
    hb = _head_batch(H)

    def body(q_ref, k_ref, v_ref, gc, bc, gr, st_ref, do_ref, dq_ref, dk_ref, dv_ref, dgc, dbc, dgr, ds_s):
        step = pl.program_id(1)
        ci = NC - 1 - step

        @pl.when(step == 0)
        def _():
            ds_s[...] = jnp.zeros_like(ds_s)

        vc, vr = _chunk_valid(ci, n_meta)
        fn = lambda *a: _gdn_chunk(vc, vr, *a)
        _, vjp = jax.vjp(fn, st_ref[:, 0].reshape(2 * hb, dk, dk), _lane_heads(q_ref, hb, dk), _lane_heads(k_ref, hb, dk),
                         _lane_heads(v_ref, 2 * hb, dk), gc[...], bc[...], gr[:, 0])
        ds, dq, dkk, dvv, d_gc, d_bc, d_gr = vjp((ds_s[...], _lane_heads(do_ref, 2 * hb, dk)))
        ds_s[...] = ds
        for j in range(hb):
            dq_ref[:, j * dk:(j + 1) * dk] = dq[j]
            dk_ref[:, j * dk:(j + 1) * dk] = dkk[j]
        for b in range(2 * hb):
            dv_ref[:, b * dk:(b + 1) * dk] = dvv[b]
        dgc[...] = d_gc
        dbc[...] = d_bc
        dgr[:, 0] = d_gr

    q, k, v, col, row, stspec = _gdn_io_specs(H, hb, dk, NC, True)
    wide = _gdn_wide_spec(hb, dk, NC, True)
    return pl.pallas_call(
        body, name="gdn_core_bwd", grid=(H // hb, NC),
        in_specs=[q, k, v, col, col, row, stspec, wide],
        out_specs=[q, q, wide, col, col, row],
        out_shape=[jax.ShapeDtypeStruct((TC, H * dk), F32), jax.ShapeDtypeStruct((TC, H * dk), F32),
                   jax.ShapeDtypeStruct((TC, 2 * H * dk), F32),
                   jax.ShapeDtypeStruct(g_c.shape, F32), jax.ShapeDtypeStruct(g_c.shape, F32),
                   jax.ShapeDtypeStruct(g_r.shape, F32)],
        scratch_shapes=[pltpu.VMEM((2 * hb, dk, dk), F32)],
        compiler_params=_cp(("parallel", "arbitrary")),
    )(qkv, qkv, qkv, g_c, b_c, g_r, st, do)


def _gdn_gate_fn(a_pre, b_pre, a_log, dt_bias):
    return -jnp.exp(a_log) * _softplus(a_pre + dt_bias), _sigmoid(b_pre)


def _gdn_gates_fwd(a_pre, b_pre, a_log, dt_bias):
    n = a_pre.shape[1]

    def fn(_, ab, bb, al, dt):
        g, beta = _gdn_gate_fn(ab, bb, al, dt)
        return [g, beta]
    return _rowwise("gdn_gates_fwd", fn, [_full(a_pre), _full(b_pre), _full(a_log), _full(dt_bias)],
                    [(n, F32, "row"), (n, F32, "row")], ROW_BLOCK, a_pre.shape[0])


def _gdn_gates_bwd(a_pre, b_pre, a_log, dt_bias, dg, dbeta):
    n = a_pre.shape[1]

    def fn(_, ab, bb, al, dt, dgb, dbb):
        rows = ab.shape[0]
        _, vjp = jax.vjp(_gdn_gate_fn, ab, bb, jnp.broadcast_to(al, (rows, n)), jnp.broadcast_to(dt, (rows, n)))
        da, db, dal, ddt = vjp((dgb, dbb))
        return [da, db, jnp.sum(dal, axis=0, keepdims=True), jnp.sum(ddt, axis=0, keepdims=True)]
    return _rowwise("gdn_gates_bwd", fn, [_full(a_pre), _full(b_pre), _full(a_log), _full(dt_bias), _full(dg), _full(dbeta)],
                    [(n, F32, "row"), (n, F32, "row"), (n, F32, "acc"), (n, F32, "acc")], ROW_BLOCK, a_pre.shape[0])


def _shift_down(cur, prev, j):
    row = _iota(cur.shape, 0)
    return jnp.where(row >= j, pltpu.roll(cur, j, 0), pltpu.roll(prev, j, 0))


def _shift_up(cur, nxt, j):
    n = cur.shape[0]
    row = _iota(cur.shape, 0)
    return jnp.where(row < n - j, pltpu.roll(cur, n - j, 0), pltpu.roll(nxt, n - j, 0))


def _conv_acc(cur, prev, w):
    acc = cur * w[GDN_CONV - 1:GDN_CONV, :]
    for j in range(1, GDN_CONV):
        acc = acc + _shift_down(cur, prev, j) * w[GDN_CONV - 1 - j:GDN_CONV - j, :]
    return acc


def _conv_tiles(width):
    return _div(width, (1024, 512, 256, 128))


def _conv_fwd(p, w, width):
    TP = p.shape[0]
    nb, tn = TP // ROW_BLOCK, _conv_tiles(width)

    def body(cur_ref, prev_ref, w_ref, y_ref):
        i = pl.program_id(1)
        prev = jnp.where(i > 0, prev_ref[...], 0.0)
        y_ref[...] = _silu(_conv_acc(cur_ref[...], prev, w_ref[...]))

    return pl.pallas_call(
        body, name="gdn_conv_fwd", grid=(width // tn, nb),
        in_specs=[pl.BlockSpec((ROW_BLOCK, tn), lambda j, i: (i, j)),
                  pl.BlockSpec((ROW_BLOCK, tn), lambda j, i: (jnp.maximum(i - 1, 0), j)),
                  pl.BlockSpec((GDN_CONV, tn), lambda j, i: (0, j))],
        out_specs=pl.BlockSpec((ROW_BLOCK, tn), lambda j, i: (i, j)),
        out_shape=jax.ShapeDtypeStruct((TP, width), F32),
        compiler_params=_cp(("parallel", "arbitrary")),
    )(p, p, w)


def _conv_bwd_pre(p, w, dy, width):
    TP = p.shape[0]
    nb, tn = TP // ROW_BLOCK, _conv_tiles(width)

    def body(cur_ref, prev_ref, w_ref, dy_ref, da_ref, dw_ref):
        i = pl.program_id(1)
        cur = cur_ref[...]
        prev = jnp.where(i > 0, prev_ref[...], 0.0)
        acc = _conv_acc(cur, prev, w_ref[...])
        s = _sigmoid(acc)
        da = dy_ref[...] * (s * (1.0 + acc * (1.0 - s)))
        da_ref[...] = da

        @pl.when(i == 0)
        def _():
            dw_ref[...] = jnp.zeros_like(dw_ref)

        rows = [jnp.sum(da * (cur if j == 0 else _shift_down(cur, prev, j)), axis=0, keepdims=True)
                for j in range(GDN_CONV - 1, -1, -1)]
        dw_ref[...] += jnp.concatenate(rows, axis=0)

    return pl.pallas_call(
        body, name="gdn_conv_bwd_pre", grid=(width // tn, nb),
        in_specs=[pl.BlockSpec((ROW_BLOCK, tn), lambda j, i: (i, j)),
                  pl.BlockSpec((ROW_BLOCK, tn), lambda j, i: (jnp.maximum(i - 1, 0), j)),
                  pl.BlockSpec((GDN_CONV, tn), lambda j, i: (0, j)),
                  pl.BlockSpec((ROW_BLOCK, tn), lambda j, i: (i, j))],
        out_specs=[pl.BlockSpec((ROW_BLOCK, tn), lambda j, i: (i, j)), pl.BlockSpec((GDN_CONV, tn), lambda j, i: (0, j))],
        out_shape=[jax.ShapeDtypeStruct((TP, width), F32), jax.ShapeDtypeStruct((GDN_CONV, width), F32)],
        compiler_params=_cp(("parallel", "arbitrary")),
    )(p, p, w, dy)


def _conv_bwd_dx(da, w, width):
    TP = da.shape[0]
    nb, tn = TP // ROW_BLOCK, _conv_tiles(width)

    def body(cur_ref, nxt_ref, w_ref, dx_ref):
        i = pl.program_id(1)
        cur = cur_ref[...]
        nxt = jnp.where(i < nb - 1, nxt_ref[...], 0.0)
        w_all = w_ref[...]
        dx = cur * w_all[GDN_CONV - 1:GDN_CONV, :]
        for j in range(1, GDN_CONV):
            dx = dx + _shift_up(cur, nxt, j) * w_all[GDN_CONV - 1 - j:GDN_CONV - j, :]
        dx_ref[...] = dx

    return pl.pallas_call(
        body, name="gdn_conv_bwd_dx", grid=(width // tn, nb),
        in_specs=[pl.BlockSpec((ROW_BLOCK, tn), lambda j, i: (i, j)),
                  pl.BlockSpec((ROW_BLOCK, tn), lambda j, i: (jnp.minimum(i + 1, nb - 1), j)),
                  pl.BlockSpec((GDN_CONV, tn), lambda j, i: (0, j))],
        out_specs=pl.BlockSpec((ROW_BLOCK, tn), lambda j, i: (i, j)),
        out_shape=jax.ShapeDtypeStruct((TP, width), F32),
        compiler_params=_cp(("parallel", "arbitrary")),
    )(da, da, w)


def _pool_bands(i, win):
    n = ROW_BLOCK
    t = _iota((n, n), 0)
    s = _iota((n, n), 1)
    cnt = jnp.minimum(i * n + t + 1, win).astype(F32)
    cur = jnp.where((t - s >= 0) & (t - s < win), 1.0 / cnt, 0.0)
    prev = jnp.where((t + n - s < win) & (i > 0), 1.0 / cnt, 0.0)
    return cur, prev


def _pool_fwd(u):
    TP, D = u.shape
    nb, grp = TP // ROW_BLOCK, D // len(POOL_WINDOWS)

    def body(cur_ref, prev_ref, out_ref):
        i = pl.program_id(0)
        for gi, win in enumerate(POOL_WINDOWS):
            sl = slice(gi * grp, (gi + 1) * grp)
            bc, bp = _pool_bands(i, win)
            cur = cur_ref[:, sl]
            out_ref[:, sl] = (_rhnn(bc, cur) + _rhnn(bp, prev_ref[:, sl]) - cur).astype(out_ref.dtype)

    return pl.pallas_call(
        body, name="pool_fwd", grid=(nb,),
        in_specs=[pl.BlockSpec((ROW_BLOCK, D), lambda i: (i, 0)),
                  pl.BlockSpec((ROW_BLOCK, D), lambda i: (jnp.maximum(i - 1, 0), 0))],
        out_specs=pl.BlockSpec((ROW_BLOCK, D), lambda i: (i, 0)),
        out_shape=jax.ShapeDtypeStruct((TP, D), BF16),
        compiler_params=_cp(("arbitrary",)),
    )(u, u)


def _pool_bwd(dp):
    TP, D = dp.shape
    nb, grp = TP // ROW_BLOCK, D // len(POOL_WINDOWS)

    def body(cur_ref, nxt_ref, out_ref):
        i = pl.program_id(0)
        for gi, win in enumerate(POOL_WINDOWS):
            sl = slice(gi * grp, (gi + 1) * grp)
            bc, _ = _pool_bands(i, win)
            _, bp = _pool_bands(i + 1, win)
            cur = cur_ref[:, sl]
            nxt = jnp.where(i < nb - 1, nxt_ref[:, sl], 0.0)
            out_ref[:, sl] = _rhtn(bc, cur) + _rhtn(bp, nxt) - cur

    return pl.pallas_call(
        body, name="pool_bwd", grid=(nb,),
        in_specs=[pl.BlockSpec((ROW_BLOCK, D), lambda i: (i, 0)),
                  pl.BlockSpec((ROW_BLOCK, D), lambda i: (jnp.minimum(i + 1, nb - 1), 0))],
        out_specs=pl.BlockSpec((ROW_BLOCK, D), lambda i: (i, 0)),
        out_shape=jax.ShapeDtypeStruct((TP, D), F32),
        compiler_params=_cp(("arbitrary",)),
    )(dp, dp)


def _rot_matrix(dh):
    s = _iota((dh, dh), 0)
    t = _iota((dh, dh), 1)
    return jnp.where(s == t + dh // 2, -1.0, 0.0) + jnp.where(s == t - dh // 2, 1.0, 0.0)


def _swa_block(i, t_real, q, k_prev, k_cur, v_prev, v_cur, cos_q, sin_q, cos_p, sin_p, sink):
    n = ROW_BLOCK
    dh = q.shape[1]
    g = q.shape[0] // n
    rot = _rot_matrix(dh)
    rope = lambda x, c, s: x * c + _hnn(x, rot) * s
    qr = rope(q, jnp.concatenate([cos_q] * g, axis=0), jnp.concatenate([sin_q] * g, axis=0))
    kb = jnp.concatenate([rope(k_prev, cos_p, sin_p), rope(k_cur, cos_q, sin_q)], axis=0)
    vb = jnp.concatenate([v_prev, v_cur], axis=0)
    s = _bnt(qr, kb) * (dh ** -0.5)
    qpos = i * n + (_iota((g * n, 2 * n), 0) % n)
    kpos = (i - 1) * n + _iota((g * n, 2 * n), 1)
    mask = (kpos <= qpos) & (qpos - kpos < SWA_WINDOW) & (kpos >= 0) & (kpos < t_real)
    s = jnp.where(mask, s, NEG)
    m = lax.stop_gradient(jnp.maximum(jnp.max(s, axis=1, keepdims=True), sink))
    e = jnp.where(mask, jnp.exp(s - m), 0.0)
    den = jnp.sum(e, axis=1, keepdims=True) + jnp.exp(sink - m)
    return _bnn(e / den, vb)


def _swa_core(q, k, v, cos, sin, sink, t_real, do=None):
    hkv, g, TP, dh = q.shape
    n = ROW_BLOCK
    nb = TP // n
    bwd = do is not None

    def body(*refs):
        q_ref, kp_ref, kc_ref, vp_ref, vc_ref, cq, sq, cpv, spv, sink_ref = refs[:10]
        i = pl.program_id(1)
        fn = lambda *a: _swa_block(i, t_real, *a)
        args = (q_ref[0].reshape(g * n, dh), kp_ref[0], kc_ref[0], vp_ref[0], vc_ref[0],
                cq[...], sq[...], cpv[...], spv[...], sink_ref[0])
        if not bwd:
            refs[10][0] = fn(*args).reshape(g, n, dh)
            return
        do_ref, dq_ref, dkp_ref, dkc_ref, dvp_ref, dvc_ref, dsink_ref = refs[10:17]
        _, vjp = jax.vjp(fn, *args)
        d = vjp(do_ref[0].reshape(g * n, dh))
        dq_ref[0] = d[0].reshape(g, n, dh)
        dkp_ref[0] = d[1]
        dkc_ref[0] = d[2]
        dvp_ref[0] = d[3]
        dvc_ref[0] = d[4]

        @pl.when(i == 0)
        def _():
            dsink_ref[...] = jnp.zeros_like(dsink_ref)
        dsink_ref[0] += d[9]

    qspec = pl.BlockSpec((1, g, n, dh), lambda h, i: (h, 0, i, 0))
    cur = pl.BlockSpec((1, n, dh), lambda h, i: (h, i, 0))
    prev = pl.BlockSpec((1, n, dh), lambda h, i: (h, jnp.maximum(i - 1, 0), 0))
    tcur = pl.BlockSpec((n, dh), lambda h, i: (i, 0))
    tprev = pl.BlockSpec((n, dh), lambda h, i: (jnp.maximum(i - 1, 0), 0))
    sspec = pl.BlockSpec((1, g * n, 1), lambda h, i: (h, 0, 0))
    in_specs = [qspec, prev, cur, prev, cur, tcur, tcur, tprev, tprev, sspec]
    ins = [q, k, k, v, v, cos, sin, cos, sin, sink]
    if not bwd:
        out_specs, out_shape = [qspec], [jax.ShapeDtypeStruct(q.shape, F32)]
    else:
        in_specs.append(qspec)
        ins.append(do)
        kv = jax.ShapeDtypeStruct(k.shape, F32)
        out_specs = [qspec, cur, cur, cur, cur, sspec]
        out_shape = [jax.ShapeDtypeStruct(q.shape, F32), kv, kv, kv, kv, jax.ShapeDtypeStruct(sink.shape, F32)]
    return pl.pallas_call(
        body, name="swa_core_bwd" if bwd else "swa_core_fwd", grid=(hkv, nb),
        in_specs=in_specs, out_specs=out_specs, out_shape=out_shape,
        compiler_params=_cp(("parallel", "arbitrary")),
    )(*ins)


def _coords():
    return lax.axis_index("x"), lax.axis_index("y"), lax.axis_index("c")


def _other_chips(x, y):
    return [(1 - x, y), (x, 1 - y), (1 - x, 1 - y)]


def _rcopy(src, dst, send, recv, dev):
    return pltpu.make_async_remote_copy(src_ref=src, dst_ref=dst, send_sem=send, recv_sem=recv,
                                        device_id=dev, device_id_type=MESH)


ANY = pl.BlockSpec(memory_space=pl.ANY)


def _allgather(name, shards, split):
    n = len(shards)

    def body(*refs):
        ins, outs = refs[:n], refs[n:2 * n]
        send, recv, loc = refs[2 * n:]
        x, y, c = _coords()
        s_me = 2 * x + y
        chips = _other_chips(x, y)
        started = []
        for i in range(n):
            lc = pltpu.make_async_copy(ins[i], outs[i].at[s_me], loc.at[i])
            lc.start()
            started.append(lc)
        if split:
            halves = [shards[i].shape[0] // 2 for i in range(n)]
            mine = [pl.ds(c * h, h) for h in halves]
            other = [pl.ds((1 - c) * h, h) for h in halves]
        sends = []
        for i in range(n):
            for j, (px, py) in enumerate(chips):
                if split:
                    cp = _rcopy(ins[i].at[mine[i]], outs[i].at[s_me, mine[i]], send.at[i, j], recv.at[i, j], (px, py, c))
                else:
                    cp = _rcopy(ins[i], outs[i].at[s_me], send.at[i, j], recv.at[i, j], (px, py, c))
                cp.start()
                sends.append(cp)
        for i in range(n):
            for j, (px, py) in enumerate(chips):
                s_j = 2 * px + py
                if split:
                    land = outs[i].at[s_j, mine[i]]
                    _rcopy(land, land, send.at[i, j], recv.at[i, j], (px, py, c)).wait_recv()
                    fw = _rcopy(land, land, send.at[i, 3 + j], recv.at[i, 3 + j], (x, y, 1 - c))
                    fw.start()
                    sends.append(fw)
                else:
                    land = outs[i].at[s_j]
                    _rcopy(land, land, send.at[i, j], recv.at[i, j], (px, py, c)).wait_recv()
        if split:
            for i in range(n):
                for j, (px, py) in enumerate(chips):
                    land = outs[i].at[2 * px + py, other[i]]
                    _rcopy(land, land, send.at[i, 3 + j], recv.at[i, 3 + j], (x, y, 1 - c)).wait_recv()
        for cp in sends:
            cp.wait_send()
        for lc in started:
            lc.wait()

    nsem = 6 if split else 3
    return pl.pallas_call(
        body, name=name,
        in_specs=[ANY] * n, out_specs=[ANY] * n,
        out_shape=[jax.ShapeDtypeStruct((N_CHIPS,) + s.shape, s.dtype) for s in shards],
        scratch_shapes=[pltpu.SemaphoreType.DMA((n, nsem)), pltpu.SemaphoreType.DMA((n, nsem)), pltpu.SemaphoreType.DMA((n,))],
        compiler_params=pltpu.CompilerParams(has_side_effects=True),
    )(*shards)


_HBM = pl.BlockSpec(memory_space=pltpu.HBM)
_SEM = pl.BlockSpec(memory_space=pltpu.SEMAPHORE)
_DATAFLOW = pltpu.SideEffectType.DATAFLOW_SIDE_EFFECTING


def _in_hbm(a):
    return pltpu.with_memory_space_constraint(a, pltpu.HBM)


def _ici_copy(gather, src, land, rows, c, s_from, s_to, send, recv, dev):
    if gather:
        mine = pl.ds(c * (rows // 2), rows // 2)
        return _rcopy(src.at[mine], land.at[s_from, mine], send, recv, dev)
    return _rcopy(src.at[s_to], land.at[s_from], send, recv, dev)


def _own_slot(name, src, where):
    shape = src.shape
    a = shape[0] if src.ndim >= 3 else 1
    c = shape[-1]
    r = int(np.prod(shape)) // (a * c)
    tr = _row_tile(r, c, src.dtype.itemsize, 2)

    def body(x_ref, y_ref, s_ref, o_ref):
        o_ref[...] = s_ref[...]

    out = pl.pallas_call(
        body, name=name,
        grid_spec=pltpu.PrefetchScalarGridSpec(
            num_scalar_prefetch=2, grid=(a, r // tr),
            in_specs=[pl.BlockSpec((1, tr, c), lambda i, j, xr, yr: (i, j, 0))],
            out_specs=pl.BlockSpec((1, tr, c), lambda i, j, xr, yr: ((2 * xr[0] + yr[0]) * a + i, j, 0))),
        out_shape=jax.ShapeDtypeStruct((N_CHIPS * a, r, c), src.dtype),
        compiler_params=_cp(("arbitrary", "arbitrary")),
    )(where[1], where[2], src.reshape(a, r, c))
    return out.reshape((N_CHIPS,) + shape)


def _ici_start(name, arrays, after, gather, lands=None):
    n = len(arrays)
    land_shapes = [((N_CHIPS,) + a.shape) if gather else a.shape for a in arrays]
    if lands is None:
        lands = [lax.empty(s, a.dtype) for s, a in zip(land_shapes, arrays)]

    def body(*refs):
        srcs, lands = refs[:n], refs[n:2 * n]
        send, recv = refs[2 * n + 1], refs[2 * n + 2]
        token = refs[-1]
        x, y, c = _coords()
        s_me = 2 * x + y
        for i in range(n):
            for j, (px, py) in enumerate(_other_chips(x, y)):
                _ici_copy(gather, srcs[i], lands[i], arrays[i].shape[0], c, s_me, 2 * px + py,
                          send.at[3 * i + j], recv.at[3 * i + j], (px, py, c)).start()
        token[...] = jnp.zeros_like(token)

    out_shape = [pltpu.SemaphoreType.DMA((3 * n,)), pltpu.SemaphoreType.DMA((3 * n,))]
    out_shape += [pltpu.HBM(a.shape, a.dtype) for a in arrays]
    out_shape += [pltpu.HBM(s, a.dtype) for s, a in zip(land_shapes, arrays)]
    out_shape.append(jax.ShapeDtypeStruct((8, LANES), F32))
    res = pl.pallas_call(
        body, name=name, out_shape=out_shape,
        in_specs=[_HBM] * (2 * n) + [ANY],
        out_specs=[_SEM, _SEM] + [_HBM] * (2 * n) + [pl.BlockSpec(memory_space=pltpu.VMEM)],
        input_output_aliases={i: 2 + i for i in range(2 * n)},
        compiler_params=pltpu.CompilerParams(has_side_effects=_DATAFLOW),
    )(*[_in_hbm(a) for a in arrays], *[_in_hbm(b) for b in lands], after)
    return (res[0], res[1]), list(res[2:2 + n]), list(res[2 + n:2 + 2 * n]), res[-1]


def _ici_wait(name, srcs, lands, send, recv, after, gather):
    n = len(srcs)

    def body(*refs):
        s_refs, l_refs = refs[:n], refs[n:2 * n]
        send_ref, recv_ref = refs[2 * n], refs[2 * n + 1]
        x, y, c = _coords()
        for i in range(n):
            for j, (px, py) in enumerate(_other_chips(x, y)):
                s_j = 2 * px + py
                cp = _ici_copy(gather, s_refs[i], l_refs[i], srcs[i].shape[0], c, s_j, s_j,
                               send_ref.at[3 * i + j], recv_ref.at[3 * i + j], (px, py, c))
                cp.wait_send()
                cp.wait_recv()

    res = pl.pallas_call(
        body, name=name,
        out_shape=[pltpu.HBM(a.shape, a.dtype) for a in srcs] + [pltpu.HBM(a.shape, a.dtype) for a in lands],
        in_specs=[_HBM] * (2 * n) + [_SEM, _SEM, ANY], out_specs=[_HBM] * (2 * n),
        input_output_aliases={i: i for i in range(2 * n)},
        compiler_params=pltpu.CompilerParams(has_side_effects=_DATAFLOW),
    )(*srcs, *lands, send, recv, after)
    return list(res[:n]), list(res[n:])


def _ag_finish(name, lands):
    n = len(lands)

    def body(*refs):
        outs = refs[n:2 * n]
        send, recv = refs[2 * n:]
        x, y, c = _coords()
        chips = _other_chips(x, y)
        cps = []
        for i in range(n):
            half = lands[i].shape[1] // 2
            for j, (px, py) in enumerate(chips):
                land = outs[i].at[2 * px + py, pl.ds(c * half, half)]
                cp = _rcopy(land, land, send.at[i, j], recv.at[i, j], (x, y, 1 - c))
                cp.start()
                cps.append(cp)
        for i in range(n):
            half = lands[i].shape[1] // 2
            for j, (px, py) in enumerate(chips):
                land = outs[i].at[2 * px + py, pl.ds((1 - c) * half, half)]
                _rcopy(land, land, send.at[i, j], recv.at[i, j], (x, y, 1 - c)).wait_recv()
        for cp in cps:
            cp.wait_send()

    return pl.pallas_call(
        body, name=name, in_specs=[ANY] * n, out_specs=[ANY] * n,
        out_shape=[jax.ShapeDtypeStruct(a.shape, a.dtype) for a in lands],
        scratch_shapes=[pltpu.SemaphoreType.DMA((n, 3)), pltpu.SemaphoreType.DMA((n, 3))],
        input_output_aliases={i: i for i in range(n)},
        compiler_params=pltpu.CompilerParams(has_side_effects=True),
    )(*lands)


def _rs_pair_exchange(name, grads):
    n = len(grads)

    def body(*refs):
        ins, bufs = refs[:n], refs[n:2 * n]
        send, recv = refs[2 * n:]
        x, y, c = _coords()
        cps = []
        for i in range(n):
            half = grads[i].shape[1] // 2
            cp = _rcopy(ins[i].at[pl.ds(0, N_CHIPS), pl.ds((1 - c) * half, half)], bufs[i], send.at[i], recv.at[i], (x, y, 1 - c))
            cp.start()
            cps.append(cp)
        for cp in cps:
            cp.wait_recv()
        for cp in cps:
            cp.wait_send()

    return pl.pallas_call(
        body, name=name, in_specs=[ANY] * n, out_specs=[ANY] * n,
        out_shape=[jax.ShapeDtypeStruct((N_CHIPS, g.shape[1] // 2, g.shape[2]), g.dtype) for g in grads],
        scratch_shapes=[pltpu.SemaphoreType.DMA((n,)), pltpu.SemaphoreType.DMA((n,))],
        compiler_params=pltpu.CompilerParams(has_side_effects=True),
    )(*grads)


def _row_tile(rows, cols, itemsize, n_bufs):
    for tr in (2048, 1024, 512, 256, 128, 64, 32, 16, 8):
        if rows % tr == 0 and 2 * n_bufs * tr * cols * itemsize <= VMEM_BUDGET // 2:
            return tr
    return rows


def _rs_pair_sum(name, g, buf, core):
    _, R, C = g.shape
    half = R // 2
    tr = _row_tile(half, C, 4, 3)
    nhb = half // tr

    def body(c_ref, g_ref, b_ref, o_ref):
        o_ref[...] = (g_ref[...].astype(F32) + b_ref[...].astype(F32)).astype(o_ref.dtype)

    return pl.pallas_call(
        body, name=name,
        grid_spec=pltpu.PrefetchScalarGridSpec(
            num_scalar_prefetch=1, grid=(N_CHIPS, nhb),
            in_specs=[pl.BlockSpec((1, tr, C), lambda s, r, c_ref: (s, c_ref[0] * nhb + r, 0)),
                      pl.BlockSpec((1, tr, C), lambda s, r, c_ref: (s, r, 0))],
            out_specs=pl.BlockSpec((1, tr, C), lambda s, r, c_ref: (s, r, 0))),
        out_shape=jax.ShapeDtypeStruct((N_CHIPS, half, C), BF16),
        compiler_params=_cp(("arbitrary", "arbitrary")),
    )(core, g, buf)


def _rs_sum4(name, buf, part, where, out=None, row_base=0):
    _, half, C = buf.shape
    tr = _row_tile(half, C, 4, 4)
    nhb = half // tr
    base = row_base // tr

    def body(c_ref, x_ref, y_ref, b_ref, p_ref, *rest):
        o_ref = rest[-1]
        s_me = 2 * x_ref[0] + y_ref[0]
        acc = None
        for s in range(N_CHIPS):
            term = jnp.where(s_me == s, p_ref[0], b_ref[s]).astype(F32)
            acc = term if acc is None else acc + term
        o_ref[...] = acc

    in_specs = [pl.BlockSpec((N_CHIPS, tr, C), lambda r, cr, xr, yr: (0, r, 0)),
                pl.BlockSpec((1, tr, C), lambda r, cr, xr, yr: (2 * xr[0] + yr[0], r, 0))]
    args = [*where, buf, part]
    aliases = {}
    if out is not None:
        in_specs.append(ANY)
        args.append(out)
        aliases = {5: 0}
    return pl.pallas_call(
        body, name=name,
        grid_spec=pltpu.PrefetchScalarGridSpec(
            num_scalar_prefetch=3, grid=(nhb,), in_specs=in_specs,
            out_specs=pl.BlockSpec((tr, C), lambda r, cr, xr, yr: (base + cr[0] * nhb + r, 0))),
        out_shape=jax.ShapeDtypeStruct((2 * half, C) if out is None else out.shape, F32),
        input_output_aliases=aliases,
        compiler_params=_cp(("arbitrary",)),
    )(*args)


def _rs_share_halves(name, outs, spans):
    n = len(outs)

    def body(*refs):
        o = refs[n:2 * n]
        send, recv = refs[2 * n:]
        x, y, c = _coords()
        cps = []
        for i in range(n):
            base, half = spans[i][0], spans[i][1] // 2
            mine = o[i].at[pl.ds(base + c * half, half)]
            cp = _rcopy(mine, mine, send.at[i], recv.at[i], (x, y, 1 - c))
            cp.start()
            cps.append(cp)
        for i in range(n):
            base, half = spans[i][0], spans[i][1] // 2
            land = o[i].at[pl.ds(base + (1 - c) * half, half)]
            _rcopy(land, land, send.at[i], recv.at[i], (x, y, 1 - c)).wait_recv()
        for cp in cps:
            cp.wait_send()

    return pl.pallas_call(
        body, name=name, in_specs=[ANY] * n, out_specs=[ANY] * n,
        out_shape=[jax.ShapeDtypeStruct(a.shape, a.dtype) for a in outs],
        scratch_shapes=[pltpu.SemaphoreType.DMA((n,)), pltpu.SemaphoreType.DMA((n,))],
        input_output_aliases={i: i for i in range(n)},
        compiler_params=pltpu.CompilerParams(has_side_effects=True),
    )(*outs)


def _allreduce_small(name, v):
    rows = v.shape[0]

    def body(v_ref, o_ref, gath, send, recv):
        x, y, c = _coords()
        me = 4 * x + 2 * y + c

        def peer(kk):
            return (1 - x if kk & 4 else x, 1 - y if kk & 2 else y, 1 - c if kk & 1 else c)

        cps = []
        for kk in range(1, N_DEV):
            cp = _rcopy(v_ref, gath.at[me], send.at[kk - 1], recv.at[kk - 1], peer(kk))
            cp.start()
            cps.append(cp)
        gath[me] = v_ref[...]
        for kk in range(1, N_DEV):
            px, py, pc = peer(kk)
            land = gath.at[4 * px + 2 * py + pc]
            _rcopy(land, land, send.at[kk - 1], recv.at[kk - 1], (px, py, pc)).wait_recv()
        for cp in cps:
            cp.wait_send()
        acc = gath[0]
        for d in range(1, N_DEV):
            acc = acc + gath[d]
        o_ref[...] = acc

    return pl.pallas_call(
        body, name=name,
        in_specs=[pl.BlockSpec(memory_space=pltpu.VMEM)], out_specs=pl.BlockSpec(memory_space=pltpu.VMEM),
        out_shape=jax.ShapeDtypeStruct(v.shape, F32),
        scratch_shapes=[pltpu.VMEM((N_DEV, rows, LANES), F32), pltpu.SemaphoreType.DMA((N_DEV - 1,)), pltpu.SemaphoreType.DMA((N_DEV - 1,))],
        compiler_params=pltpu.CompilerParams(has_side_effects=True, vmem_limit_bytes=VMEM_LIMIT),
    )(v)


def _adamw(name, w, g, m, v):
    rows, cols = w.shape
    tr = _row_tile(rows, cols, 4, 7)

    def body(w_ref, g_ref, m_ref, v_ref, d_ref, nm_ref, nv_ref):
        gg = g_ref[...]
        nm = ADAM_B1 * m_ref[...] + (1.0 - ADAM_B1) * gg
        nv = ADAM_B2 * v_ref[...] + (1.0 - ADAM_B2) * (gg * gg)
        m_hat = nm / (1.0 - ADAM_B1 ** ADAM_STEP)
        v_hat = nv / (1.0 - ADAM_B2 ** ADAM_STEP)
        d_ref[...] = -ADAM_LR * (m_hat / (jnp.sqrt(v_hat) + ADAM_EPS) + ADAM_WD * w_ref[...])
        nm_ref[...] = nm
        nv_ref[...] = nv

    spec = pl.BlockSpec((tr, cols), lambda i: (i, 0))
    shp = jax.ShapeDtypeStruct((rows, cols), F32)
    return pl.pallas_call(
        body, name=name, grid=(rows // tr,), in_specs=[spec] * 4, out_specs=[spec] * 3, out_shape=[shp] * 3,
        compiler_params=_cp(("parallel",)),
    )(w, g, m, v)


def _as2d(a):
    if a.ndim == 1:
        return a.reshape(1, a.shape[0])
    return a.reshape(-1, a.shape[-1])


_WEIGHTS = ['meta_tokens', 'norm_w', 'ffn_w_gate', 'ffn_w_up', 'ffn_w_down', 'mlstm_w_in', 'mlstm_b_if', 'mlstm_norm_w',
            'mlstm_w_out', 'pool_w', 'pool_scale', 'gdn_w_in', 'gdn_conv_w', 'gdn_a_log', 'gdn_dt_bias', 'gdn_norm_w',
            'gdn_w_out', 'swa_w_qkv', 'swa_b_qkv', 'swa_sinks', 'swa_w_out', 'swa_b_out', 'final_norm_w']
_SMALL = [('meta_tokens', True), ('norm_w', True), ('pool_scale', True), ('gdn_conv_w', True), ('swa_b_qkv', True),
          ('swa_b_out', True), ('mlstm_b_if', False), ('mlstm_norm_w', False), ('gdn_a_log', False),
          ('gdn_dt_bias', False), ('gdn_norm_w', False), ('swa_sinks', False), ('final_norm_w', False)]


def _pack(vals):
    flat = jnp.concatenate([v.reshape(-1).astype(F32) for v in vals])
    n = _round_up(flat.shape[0], 8 * LANES)
    return jnp.pad(flat, (0, n - flat.shape[0])).reshape(n // LANES, LANES)


def _unpack(packed, shapes):
    flat = packed.reshape(-1)
    out, off = [], 0
    for s in shapes:
        n = int(np.prod(s))
        out.append(flat[off:off + n].reshape(s))
        off += n
    return out


def _to_chunks(a, n_meta, seq):
    pad = jnp.zeros((CHUNK - n_meta,) + a.shape[1:], a.dtype)
    return jnp.concatenate([a[:n_meta], pad, a[n_meta:n_meta + seq]], axis=0)


def _from_chunks(a, n_meta, seq, tp):
    pad = jnp.zeros((tp - n_meta - seq,) + a.shape[1:], a.dtype)
    return jnp.concatenate([a[:n_meta], a[CHUNK:CHUNK + seq], pad], axis=0)


def _col_row(g, heads):
    t = g.T
    return t[:, :, None], t.reshape(heads, -1, 1, CHUNK)


def _from_col_row(dc, dr):
    heads = dc.shape[0]
    return (dc[:, :, 0] + dr.reshape(heads, -1)).T


def kernel(x, meta_tokens, norm_w, ffn_w_gate, ffn_w_up, ffn_w_down, mlstm_w_in, mlstm_b_if, mlstm_norm_w, mlstm_w_out, pool_w, pool_scale, gdn_w_in, gdn_conv_w, gdn_a_log, gdn_dt_bias, gdn_norm_w, gdn_w_out, swa_w_qkv, swa_b_qkv, swa_sinks, swa_w_out, swa_b_out, final_norm_w, loss_target, m_meta_tokens, m_norm_w, m_ffn_w_gate, m_ffn_w_up, m_ffn_w_down, m_mlstm_w_in, m_mlstm_b_if, m_mlstm_norm_w, m_mlstm_w_out, m_pool_w, m_pool_scale, m_gdn_w_in, m_gdn_conv_w, m_gdn_a_log, m_gdn_dt_bias, m_gdn_norm_w, m_gdn_w_out, m_swa_w_qkv, m_swa_b_qkv, m_swa_sinks, m_swa_w_out, m_swa_b_out, m_final_norm_w, v_meta_tokens, v_norm_w, v_ffn_w_gate, v_ffn_w_up, v_ffn_w_down, v_mlstm_w_in, v_mlstm_b_if, v_mlstm_norm_w, v_mlstm_w_out, v_pool_w, v_pool_scale, v_gdn_w_in, v_gdn_conv_w, v_gdn_a_log, v_gdn_dt_bias, v_gdn_norm_w, v_gdn_w_out, v_swa_w_qkv, v_swa_b_qkv, v_swa_sinks, v_swa_w_out, v_swa_b_out, v_final_norm_w):
    args = locals()
    W = {n: args[n] for n in _WEIGHTS}
    M1 = {n: args["m_" + n] for n in _WEIGHTS}
    V2 = {n: args["v_" + n] for n in _WEIGHTS}

    SEQ, D = x.shape[1], x.shape[2]
    NM = meta_tokens.shape[0]
    T = NM + SEQ
    TP = _round_up(T, ROW_BLOCK)
    DEPTH = ffn_w_gate.shape[0]
    FFS = ffn_w_gate.shape[3]
    ML_H = mlstm_b_if.shape[1] // 2
    ML_DV = D // ML_H
    ML_DK = ML_DV // 2
    ML_IN = 2 * ML_H * ML_DK + 2 * D + 2 * ML_H
    ML_INP = _pad_cols(ML_IN)
    GD_DK = gdn_norm_w.shape[1]
    GD_VH = gdn_a_log.shape[1]
    GD_QH = GD_VH // 2
    GD_QKW = GD_QH * GD_DK
    GD_VW = GD_VH * GD_DK
    GD_CC = 2 * GD_QKW + GD_VW
    GD_IN = GD_CC + GD_VW + 2 * GD_VH
    GD_INP = _pad_cols(GD_IN)
    SW_HQ = swa_sinks.shape[1]
    SW_DH = D // SW_HQ
    SW_HKV = SW_HQ // SWA_GROUP
    SW_KVW = SW_HKV * SW_DH
    SW_IN = D + 2 * SW_KVW
    n_pool = len(POOL_WINDOWS)
    PG = D // n_pool

    cx, cy, cc = _coords()
    s_me = 2 * cx + cy
    core = cc.astype(jnp.int32).reshape(1)
    where = (core, cx.astype(jnp.int32).reshape(1), cy.astype(jnp.int32).reshape(1))

    def my_cols(full, width):
        return lax.dynamic_slice_in_dim(full, s_me * width, width, axis=full.ndim - 1)

    big_names = ['ffn_w_gate', 'ffn_w_up', 'ffn_w_down', 'mlstm_w_in', 'mlstm_w_out', 'pool_w', 'gdn_w_in', 'gdn_w_out',
                 'swa_w_qkv', 'swa_w_out']
    mixer_shards = [[mlstm_w_in[0], mlstm_w_out[0]], [pool_w[0].reshape(n_pool * (PG // N_CHIPS), PG)],
                    [gdn_w_in[0], gdn_w_out[0]], [swa_w_qkv[0], swa_w_out[0]]]
    wgb, wub, wdb = ffn_w_gate.astype(BF16), ffn_w_up.astype(BF16), ffn_w_down.astype(BF16)
    ag_groups = [[wgb[li], wub[li], wdb[li]] + [m.astype(BF16) for m in mixer_shards[li % 4]] for li in range(DEPTH)]
    ag_state = {}
    WF = {}

    def cols_full(g, pad_to=None):
        k = g.shape[1]
        full = jnp.transpose(g, (1, 0, 2)).reshape(k, -1)
        if pad_to is not None and pad_to > full.shape[1]:
            full = jnp.pad(full, ((0, 0), (0, pad_to - full.shape[1])))
        return full

    def rows_full(g):
        return g.reshape(-1, g.shape[2])

    def start_gather(li, after):
        lands = [_own_slot(f"ag_own_{li}_{i}", a, where) for i, a in enumerate(ag_groups[li])]
        sems, srcs, lands, token = _ici_start(f"ag_start_{li}", ag_groups[li], after, True, lands)
        ag_state[li] = (sems, srcs, lands)
        return token

    def gather_layer(li, after):
        sems, srcs, lands = ag_state.pop(li)
        _, lands = _ici_wait(f"ag_wait_{li}", srcs, lands, *sems, after, True)
        full = _ag_finish(f"ag_finish_{li}", lands)
        WF['wg', li], WF['wu', li], WF['wd', li] = full[0], full[1], full[2]
        m = full[3:]
        if li % 4 == 0:
            WF['ml_win'], WF['ml_wout'] = cols_full(m[0], ML_INP), rows_full(m[1])
        elif li % 4 == 1:
            WF['pool'] = jnp.transpose(m[0].reshape(N_CHIPS, n_pool, PG // N_CHIPS, PG), (1, 0, 2, 3)).reshape(n_pool, PG, PG)
        elif li % 4 == 2:
            WF['gd_win'], WF['gd_wout'] = cols_full(m[0], GD_INP), rows_full(m[1])
        else:
            WF['sw_wqkv'], WF['sw_wout'] = cols_full(m[0]), rows_full(m[1])

    small_sharded = [n for n, sh in _SMALL if sh]
    sm_shapes = [W[n].shape for n in small_sharded]
    sm_gath = _allgather("allgather_small", [_pack([W[n] for n in small_sharded])], False)[0]
    sm_parts = [_unpack(sm_gath[s], sm_shapes) for s in range(N_CHIPS)]
    SF = {n: jnp.concatenate([sm_parts[s][i] for s in range(N_CHIPS)], axis=-1) for i, n in enumerate(small_sharded)}
    meta_full, normw_full = SF['meta_tokens'], SF['norm_w']
    pool_scale_full, conv_full = SF['pool_scale'], SF['gdn_conv_w'][0]
    bqkv_full, bout_full = SF['swa_b_qkv'], SF['swa_b_out']

    nps = 1
    ff_tm = _tiles(TP, FFS, D, 2 * 4 + 3 * 2 * 2, n_pairs=2)[0]

    def ffn_fwd(h, li, wi, nw):
        n = _rmsnorm_fwd(f"ffn_norm_{li}_{wi}", h, nw, BF16)
        tm, tn, tk = ff_tm, FFS, _div(D, (512, 256, 128))
        bspec = lambda j, k: (j, wi, k, 0)
        g, u, a = _matmul(
            f"ffn_gateup_{li}_{wi}", "nn", (TP // tm, N_CHIPS, D // tk),
            [(n, (tm, tk), lambda i, j, k: (i, k))] * 2,
            [(WF['wg', li], (None, None, tk, tn), lambda i, j, k: bspec(j, k)),
             (WF['wu', li], (None, None, tk, tn), lambda i, j, k: bspec(j, k))],
            [0, 1], 2, [],
            [(jax.ShapeDtypeStruct((TP, N_CHIPS * FFS), BF16), (tm, tn), lambda i, j, k: (i, j))] * 3,
            lambda accs, ex: [accs[0], accs[1], _silu(accs[0]) * accs[1]], tm, tn)
        tm2, tn2, tk2 = _tiles(TP, D, FFS, 4 + 2 * 4 + 2 * 4, k_cands=(FFS,), n_cands=(1024, 512, 256, 128))
        kps = FFS // tk2
        h2 = _matmul(
            f"ffn_down_{li}_{wi}", "nn", (TP // tm2, D // tn2, N_CHIPS * kps),
            [(a, (tm2, tk2), lambda i, j, k: (i, k))],
            [(WF['wd', li], (None, None, tk2, tn2), lambda i, j, k: (k // kps, wi, k % kps, j))],
            [0], 1, [(h, (tm2, tn2), lambda i, j, k: (i, j))],
            [(jax.ShapeDtypeStruct((TP, D), F32), (tm2, tn2), lambda i, j, k: (i, j))],
            lambda accs, ex: [ex[0] + 0.5 * accs[0]], tm2, tn2)[0]
        return h2, (h, n, g, u, a)

    def ffn_bwd(dh2, saved, li, wi, nw, gbufs):
        h, n, g, u, a = saved
        gg, gu, gd = gbufs
        slot = wi
        tm, tn = ff_tm, FFS
        tk = _div(D, (512, 256, 128))

        def epi(accs, ex):
            gb, ub = ex[0].astype(F32), ex[1].astype(F32)
            da = 0.5 * accs[0]
            s = _sigmoid(gb)
            return [da * ub * (s * (1.0 + gb * (1.0 - s))), da * (gb * s)]
        dg, du = _matmul(
            f"ffn_dact_{li}_{wi}", "nt", (TP // tm, N_CHIPS, D // tk),
            [(dh2, (tm, tk), lambda i, j, k: (i, k))],
            [(WF['wd', li], (None, None, tn, tk), lambda i, j, k: (j, wi, 0, k))],
            [0], 1, [(g, (tm, tn), lambda i, j, k: (i, j)), (u, (tm, tn), lambda i, j, k: (i, j))],
            [(jax.ShapeDtypeStruct((TP, N_CHIPS * FFS), BF16), (tm, tn), lambda i, j, k: (i, j))] * 2, epi, tm, tn)
        tkr = _div(TP, (1408, 1056, 704, 384, 256, 128))
        tnd = _div(D, (1024, 512, 256, 128))
        gd = _matmul(
            f"ffn_dwd_{li}_{wi}", "tn", (N_CHIPS, D // tnd, TP // tkr),
            [(a, (tkr, FFS), lambda i, j, k: (k, i))], [(dh2, (tkr, tnd), lambda i, j, k: (k, j))],
            [0], 1, [], [(jax.ShapeDtypeStruct(gd.shape, BF16), (None, None, FFS, tnd), lambda i, j, k: (i, slot, 0, j))],
            lambda accs, ex: [0.5 * accs[0]], FFS, tnd, alias_inputs=[gd], alias_map={0: 0})[0]
        tmw = _div(D, (512, 256, 128))
        gg, gu = _matmul(
            f"ffn_dwgu_{li}_{wi}", "tn", (D // tmw, N_CHIPS, TP // tkr),
            [(n, (tkr, tmw), lambda i, j, k: (k, i))] * 2,
            [(dg, (tkr, FFS), lambda i, j, k: (k, j)), (du, (tkr, FFS), lambda i, j, k: (k, j))],
            [0, 1], 2, [],
            [(jax.ShapeDtypeStruct(gg.shape, BF16), (None, None, tmw, FFS), lambda i, j, k: (j, slot, i, 0))] * 2,
            lambda accs, ex: [accs[0], accs[1]], tmw, FFS, alias_inputs=[gg, gu], alias_map={0: 0, 1: 1})
        tm3 = _div(TP, (704, 528, 384, 256, 128))
        tn3 = _div(D, (1024, 512, 256, 128))
        dn = _matmul(
            f"ffn_dn_{li}_{wi}", "nt", (TP // tm3, D // tn3, N_CHIPS),
            [(dg, (tm3, FFS), lambda i, j, k: (i, k)), (du, (tm3, FFS), lambda i, j, k: (i, k))],
            [(WF['wg', li], (None, None, tn3, FFS), lambda i, j, k: (k, wi, j, 0)),
             (WF['wu', li], (None, None, tn3, FFS), lambda i, j, k: (k, wi, j, 0))],
            [0, 0], 1, [], [(jax.ShapeDtypeStruct((TP, D), F32), (tm3, tn3), lambda i, j, k: (i, j))],
            lambda accs, ex: [accs[0]], tm3, tn3)[0]
        dh, dnw = _rmsnorm_bwd(f"ffn_norm_bwd_{li}_{wi}", h, nw, dn, dh2)
        return dh, dnw, (gg, gu, gd)

    def mlstm_fwd(h, nw):
        u = _rmsnorm_fwd("mlstm_norm", h, nw, BF16)
        p = _mm("mlstm_in", "nn", u, WF['ml_win'], F32)
        pc = _to_chunks(p, NM, SEQ)
        qkw = ML_H * ML_DK
        gates = pc[:, 2 * qkw + 2 * D:2 * qkw + 2 * D + 2 * ML_H] + mlstm_b_if
        li_c, li_r = _col_row(gates[:, :ML_H], ML_H)
        lf_c, lf_r = _col_row(gates[:, ML_H:], ML_H)
        hh, cs, ns, ms = _mlstm_core_fwd(pc, li_c, lf_c, li_r, lf_r, ML_H, ML_DK, ML_DV, NM)
        hh_s = _from_chunks(hh, NM, SEQ, TP)
        og_cb = (2 * qkw + D) // D
        out = _headnorm_fwd("mlstm_post", hh_s, p, D, og_cb, mlstm_norm_w, ML_DV, _sigmoid)
        h2 = _mm("mlstm_out", "nn", out, WF['ml_wout'], F32, lambda acc, hb: hb + acc, [h], ["tile"])
        return h2, (h, u, p, pc, (li_c, lf_c, li_r, lf_r), (cs, ns, ms), hh_s, out, og_cb)

    def mlstm_bwd(dh2, saved, nw):
        h, u, p, pc, gts, sts, hh_s, out, og_cb = saved
        dout = _mm("mlstm_out_dx", "nt", dh2, WF['ml_wout'], F32)
        d_wout = _mm("mlstm_out_dw", "tn", out, dh2, BF16)
        dhh, dog, dnormw = _headnorm_bwd("mlstm_post_bwd", hh_s, p, D, og_cb, mlstm_norm_w, dout, ML_DV, _sigmoid)
        dq, dkk, dvv, d0, d1, d2, d3 = _mlstm_core_bwd(pc, *gts, *sts, _to_chunks(dhh, NM, SEQ), ML_H, ML_DK, ML_DV, NM)
        dgates = jnp.concatenate([_from_col_row(d0, d2), _from_col_row(d1, d3)], axis=1)
        dqkvg = _from_chunks(jnp.concatenate([dq, dkk, dvv], axis=1), NM, SEQ, TP)
        dgs = _from_chunks(dgates, NM, SEQ, TP)
        pad = jnp.zeros((TP, ML_INP - ML_IN), F32)
        dp = jnp.concatenate([dqkvg, dog, dgs, pad], axis=1)
        d_bif = _colsum("mlstm_dbias", jnp.pad(dgs, ((0, 0), (0, LANES - 2 * ML_H))))[:, :2 * ML_H]
        d_win = _mm("mlstm_in_dw", "tn", u, dp, BF16)[:, :ML_IN]
        du = _mm("mlstm_in_dx", "nt", dp, WF['ml_win'], F32)
        dh, dnw = _rmsnorm_bwd("mlstm_norm_bwd", h, nw, du, dh2)
        return dh, dnw, {'mlstm_w_in': d_win, 'mlstm_w_out': d_wout, 'mlstm_b_if': d_bif, 'mlstm_norm_w': dnormw}

    def pool_fwd_layer(h, nw):
        u = _rmsnorm_fwd("pool_norm", h, nw, F32)
        pooled = _pool_fwd(u)
        tm = _div(TP, (704, 528, 384, 256, 128))
        tk = _div(PG, (512, 256, 128))
        kpg = PG // tk
        h2, ypre = _matmul(
            "pool_mix", "nn", (TP // tm, n_pool, kpg),
            [(pooled, (tm, tk), lambda i, j, k: (i, j * kpg + k))],
            [(WF['pool'],(None, tk, PG), lambda i, j, k: (j, k, 0))],
            [0], 1, [(h, (tm, PG), lambda i, j, k: (i, j)), (pool_scale_full, (1, PG), lambda i, j, k: (0, j))],
            [(jax.ShapeDtypeStruct((TP, D), F32), (tm, PG), lambda i, j, k: (i, j))] * 2,
            lambda accs, ex: [ex[0] + accs[0] * ex[1], accs[0]], tm, PG)
        return h2, (h, pooled, ypre)

    def pool_bwd_layer(dh2, saved, nw):
        h, pooled, ypre = saved

        def fn(_, dyb, ypb, sb):
            return [dyb * sb, jnp.sum(dyb * ypb, axis=0, keepdims=True)]
        dys, dscale = _rowwise("pool_scale_bwd", fn, [_full(dh2), _full(ypre), _full(pool_scale_full)],
                               [(D, BF16, "row"), (D, F32, "acc")], ROW_BLOCK, TP)
        tm = _div(TP, (704, 528, 384, 256, 128))
        tk = _div(PG, (512, 256, 128))
        kpg = PG // tk
        dpooled = _matmul(
            "pool_mix_dx", "nt", (TP // tm, n_pool, kpg),
            [(dys, (tm, tk), lambda i, j, k: (i, j * kpg + k))],
            [(WF['pool'],(None, PG, tk), lambda i, j, k: (j, 0, k))],
            [0], 1, [], [(jax.ShapeDtypeStruct((TP, D), F32), (tm, PG), lambda i, j, k: (i, j))],
            lambda accs, ex: [accs[0]], tm, PG)[0]
        tkr = _div(TP, (1408, 1056, 704, 384, 256, 128))
        d_pw = _matmul(
            "pool_mix_dw", "tn", (1, n_pool, TP // tkr),
            [(pooled, (tkr, PG), lambda i, j, k: (k, j))], [(dys, (tkr, PG), lambda i, j, k: (k, j))],
            [0], 1, [], [(jax.ShapeDtypeStruct((n_pool, PG, PG), BF16), (None, PG, PG), lambda i, j, k: (j, 0, 0))],
            lambda accs, ex: [accs[0]], PG, PG)[0]
        du = _pool_bwd(dpooled)
        dh, dnw = _rmsnorm_bwd("pool_norm_bwd", h, nw, du, dh2)
        return dh, dnw, {'pool_w': d_pw, 'pool_scale': dscale}

    def gdn_fwd(h, nw):
        u = _rmsnorm_fwd("gdn_norm", h, nw, BF16)
        p = _mm("gdn_in", "nn", u, WF['gd_win'], F32)
        qkv_act = _conv_fwd(p, conv_full, GD_CC)
        b_pre = p[:, GD_CC + GD_VW:GD_CC + GD_VW + GD_VH]
        a_pre = p[:, GD_CC + GD_VW + GD_VH:GD_IN]
        g, beta = _gdn_gates_fwd(a_pre, b_pre, gdn_a_log, gdn_dt_bias)
        qkv_c = _to_chunks(qkv_act, NM, SEQ)
        g_c, g_r = _col_row(_to_chunks(g, NM, SEQ), GD_VH)
        b_c, _ = _col_row(_to_chunks(beta, NM, SEQ), GD_VH)
        o, st = _gdn_core_fwd(qkv_c, g_c, b_c, g_r, GD_QH, GD_DK, NM)
        o_s = _from_chunks(o, NM, SEQ, TP)
        nw_t = jnp.tile(gdn_norm_w, (1, GD_VH))
        z_cb = GD_CC // GD_VW
        out = _headnorm_fwd("gdn_post", o_s, p, GD_VW, z_cb, nw_t, GD_DK, _silu)
        h2 = _mm("gdn_out", "nn", out, WF['gd_wout'], F32, lambda acc, hb: hb + acc, [h], ["tile"])
        return h2, (h, u, p, qkv_c, (g_c, b_c, g_r), st, o_s, out, nw_t, z_cb, a_pre, b_pre)

    def gdn_bwd(dh2, saved, nw):
        h, u, p, qkv_c, gts, st, o_s, out, nw_t, z_cb, a_pre, b_pre = saved
        dout = _mm("gdn_out_dx", "nt", dh2, WF['gd_wout'], F32)
        d_wout = _mm("gdn_out_dw", "tn", out, dh2, BF16)
        do, dz, dnw_t = _headnorm_bwd("gdn_post_bwd", o_s, p, GD_VW, z_cb, nw_t, dout, GD_DK, _silu)
        dnormw = jnp.sum(dnw_t.reshape(GD_VH, GD_DK), axis=0, keepdims=True)
        res = _gdn_core_bwd(qkv_c, *gts, st, _to_chunks(do, NM, SEQ), GD_QH, GD_DK, NM)
        dq, dkk, dvv = res[0], res[1], res[2]
        dgc, dbc, dgr = res[3], res[4], res[5]
        dg = _from_chunks(_from_col_row(dgc, dgr), NM, SEQ, TP)
        dbeta = _from_chunks(dbc[:, :, 0].T, NM, SEQ, TP)
        da_pre, db_pre, d_alog, d_dt = _gdn_gates_bwd(a_pre, b_pre, gdn_a_log, gdn_dt_bias, dg, dbeta)
        dact = _from_chunks(jnp.concatenate([dq, dkk, dvv], axis=1), NM, SEQ, TP)
        dacc, d_conv = _conv_bwd_pre(p, conv_full, dact, GD_CC)
        dqkv_pre = _conv_bwd_dx(dacc, conv_full, GD_CC)
        pad = jnp.zeros((TP, GD_INP - GD_IN), F32)
        dp = jnp.concatenate([dqkv_pre, dz, db_pre, da_pre, pad], axis=1)
        d_win = _mm("gdn_in_dw", "tn", u, dp, BF16)[:, :GD_IN]
        du = _mm("gdn_in_dx", "nt", dp, WF['gd_win'], F32)
        dh, dnw = _rmsnorm_bwd("gdn_norm_bwd", h, nw, du, dh2)
        return dh, dnw, {'gdn_w_in': d_win, 'gdn_w_out': d_wout, 'gdn_conv_w': d_conv, 'gdn_a_log': d_alog,
                         'gdn_dt_bias': d_dt, 'gdn_norm_w': dnormw}

    inv = ROPE_THETA ** (-jnp.arange(0, SW_DH, 2, dtype=F32) / SW_DH)
    ang = jnp.arange(TP, dtype=F32)[:, None] * inv[None, :]
    ang = jnp.concatenate([ang, ang], axis=-1)
    rope_cos, rope_sin = jnp.cos(ang), jnp.sin(ang)

    def swa_split(p):
        q = jnp.transpose(p[:, :D].reshape(TP, SW_HKV, SWA_GROUP, SW_DH), (1, 2, 0, 3))
        k = jnp.transpose(p[:, D:D + SW_KVW].reshape(TP, SW_HKV, SW_DH), (1, 0, 2))
        v = jnp.transpose(p[:, D + SW_KVW:].reshape(TP, SW_HKV, SW_DH), (1, 0, 2))
        return q, k, v

    def swa_fwd(h, nw):
        u = _rmsnorm_fwd("swa_norm", h, nw, BF16)
        p = _mm("swa_in", "nn", u, WF['sw_wqkv'], F32, lambda acc, bb: acc + bb, [bqkv_full], ["row"])
        q, k, v = swa_split(p)
        sink = jnp.repeat(swa_sinks.reshape(SW_HKV, SWA_GROUP), ROW_BLOCK, axis=1)[:, :, None]
        o = _swa_core(q, k, v, rope_cos, rope_sin, sink, T)[0]
        o2 = jnp.transpose(o, (2, 0, 1, 3)).reshape(TP, D).astype(BF16)
        h2 = _mm("swa_out", "nn", o2, WF['sw_wout'], F32, lambda acc, hb, bb: hb + acc + bb, [h, bout_full], ["tile", "row"])
        return h2, (h, u, q, k, v, sink, o2)

    def swa_bwd(dh2, saved, nw):
        h, u, q, k, v, sink, o2 = saved
        do = _mm("swa_out_dx", "nt", dh2, WF['sw_wout'], F32)
        d_wout = _mm("swa_out_dw", "tn", o2, dh2, BF16)
        d_bout = _colsum("swa_dbout", dh2)
        do4 = jnp.transpose(do.reshape(TP, SW_HKV, SWA_GROUP, SW_DH), (1, 2, 0, 3))
        dq, dkp, dkc, dvp, dvc, dsink = _swa_core(q, k, v, rope_cos, rope_sin, sink, T, do=do4)
        shift = lambda a: jnp.concatenate([a[:, ROW_BLOCK:], jnp.zeros_like(a[:, :ROW_BLOCK])], axis=1)
        dk = dkc + shift(dkp)
        dv = dvc + shift(dvp)
        dp = jnp.concatenate([jnp.transpose(dq, (2, 0, 1, 3)).reshape(TP, D),
                              jnp.transpose(dk, (1, 0, 2)).reshape(TP, SW_KVW),
                              jnp.transpose(dv, (1, 0, 2)).reshape(TP, SW_KVW)], axis=1)
        d_sinks = jnp.sum(dsink.reshape(SW_HKV, SWA_GROUP, ROW_BLOCK), axis=2).reshape(1, SW_HQ)
        d_bqkv = _colsum("swa_dbqkv", dp)
        d_wqkv = _mm("swa_in_dw", "tn", u, dp, BF16)
        du = _mm("swa_in_dx", "nt", dp, WF['sw_wqkv'], F32)
        dh, dnw = _rmsnorm_bwd("swa_norm_bwd", h, nw, du, dh2)
        return dh, dnw, {'swa_w_qkv': d_wqkv, 'swa_w_out': d_wout, 'swa_b_qkv': d_bqkv, 'swa_b_out': d_bout,
                         'swa_sinks': d_sinks}

    mixers_fwd = [mlstm_fwd, pool_fwd_layer, gdn_fwd, swa_fwd]
    mixers_bwd = [mlstm_bwd, pool_bwd_layer, gdn_bwd, swa_bwd]

    h = jnp.concatenate([meta_full, x[0], jnp.zeros((TP - T, D), F32)], axis=0)
    saved = []
    token = start_gather(0, sm_gath)
    for li in range(DEPTH):
        nws = [normw_full[li, t].reshape(1, D) for t in range(3)]
        gather_layer(li, token if li == 0 else h)
        if li + 1 < DEPTH:
            nws[0] = nws[0] + start_gather(li + 1, WF['wg', li])[:1, :1]
        h, s0 = ffn_fwd(h, li, 0, nws[0])
        h, s1 = mixers_fwd[li % 4](h, nws[1])
        h, s2 = ffn_fwd(h, li, 1, nws[2])
        saved.append((s0, s1, s2, nws))

    tgt = jnp.concatenate([jnp.zeros((NM, D), F32), loss_target[0], jnp.zeros((TP - T, D), F32)], axis=0)
    dh, d_final_w, loss_vec = _loss_head("loss_head", h, final_norm_w.reshape(1, D), tgt, NM, SEQ)

    def col_shards(g, w):
        return jnp.transpose(g.reshape(g.shape[0], N_CHIPS, w), (1, 0, 2))

    def row_shards(g):
        return g.reshape(N_CHIPS, -1, g.shape[1])

    def mixer_grads(li, gm):
        if li % 4 == 0:
            return ['mlstm_w_in', 'mlstm_w_out'], [col_shards(gm['mlstm_w_in'], ML_IN // N_CHIPS), row_shards(gm['mlstm_w_out'])]
        if li % 4 == 1:
            pw = jnp.transpose(gm['pool_w'].reshape(n_pool, N_CHIPS, PG // N_CHIPS, PG), (1, 0, 2, 3))
            return ['pool_w'], [pw.reshape(N_CHIPS, -1, PG)]
        if li % 4 == 2:
            return ['gdn_w_in', 'gdn_w_out'], [col_shards(gm['gdn_w_in'], GD_IN // N_CHIPS), row_shards(gm['gdn_w_out'])]
        return ['swa_w_qkv', 'swa_w_out'], [col_shards(gm['swa_w_qkv'], SW_IN // N_CHIPS), row_shards(gm['swa_w_out'])]

    ffn_rows = (2 * D, 2 * D, 2 * FFS)
    ffn_out = [lax.empty((DEPTH * r, c_), F32) for r, c_ in zip(ffn_rows, (FFS, FFS, D))]
    grads = {}

    def rs_begin(li, names, glist):
        bufs = _rs_pair_exchange(f"rs_pair_{li}", glist)
        parts = [_rs_pair_sum(f"rs_pairsum_{li}_{i}", g, b, core) for i, (g, b) in enumerate(zip(glist, bufs))]
        sems, parts, lands, token = _ici_start(f"rs_start_{li}", parts, core, False)
        return (li, names, sems, parts, lands), token

    def rs_finish(state, after):
        li, names, sems, parts, lands = state
        parts, lands = _ici_wait(f"rs_wait_{li}", parts, lands, *sems, after, False)
        outs, spans = [], []
        for i, (p, b) in enumerate(zip(parts, lands)):
            if i < 3:
                outs.append(_rs_sum4(f"rs_sum4_{li}_{i}", b, p, where, out=ffn_out[i], row_base=li * ffn_rows[i]))
                spans.append((li * ffn_rows[i], ffn_rows[i]))
            else:
                outs.append(_rs_sum4(f"rs_sum4_{li}_{i}", b, p, where))
                spans.append((0, outs[-1].shape[0]))
        outs = _rs_share_halves(f"rs_share_{li}", outs, spans)
        ffn_out[:] = outs[:3]
        for n, o in zip(names, outs[3:]):
            grads[n] = o.reshape(W[n].shape)

    d_normw = [[None] * 3 for _ in range(DEPTH)]
    GR = {}
    pending = None
    for li in reversed(range(DEPTH)):
        s0, s1, s2, nws = saved[li]
        gbufs = (lax.empty((N_CHIPS, 2, D, FFS), BF16), lax.empty((N_CHIPS, 2, D, FFS), BF16),
                 lax.empty((N_CHIPS, 2, FFS, D), BF16))
        dh, d_normw[li][2], gbufs = ffn_bwd(dh, s2, li, 1, nws[2], gbufs)
        dh, d_normw[li][1], gm = mixers_bwd[li % 4](dh, s1, nws[1])
        GR.update(gm)
        dh, d_normw[li][0], gbufs = ffn_bwd(dh, s0, li, 0, nws[0], gbufs)
        if pending is not None:
            rs_finish(pending, dh)
        names, mg = mixer_grads(li, gm)
        glist = [gbufs[0].reshape(N_CHIPS, 2 * D, FFS), gbufs[1].reshape(N_CHIPS, 2 * D, FFS),
                 gbufs[2].reshape(N_CHIPS, 2 * FFS, D)] + mg
        pending, token = rs_begin(li, names, glist)
        dh = dh + token[:1, :1]
    rs_finish(pending, dh)
    for n, o in zip(['ffn_w_gate', 'ffn_w_up', 'ffn_w_down'], ffn_out):
        grads[n] = o.reshape(W[n].shape)
    grad_x = dh[NM:NM + SEQ][None]
    GR['meta_tokens'] = dh[:NM]
    GR['norm_w'] = jnp.stack([jnp.concatenate(r, axis=0) for r in d_normw], axis=0)
    GR['final_norm_w'] = d_final_w

    small_names = [n for n, _ in _SMALL]
    small_full_shapes = [GR[n].shape for n in small_names]
    packed = _pack([GR[n] for n in small_names] + [loss_vec[:, :1]])
    red = _allreduce_small("allreduce_small", packed)
    parts = _unpack(red, small_full_shapes + [(1, 1)])
    loss = parts[-1].reshape(())
    for (n, sharded), full in zip(_SMALL, parts[:-1]):
        full = full.reshape(W[n].shape[:-1] + (-1,))
        grads[n] = my_cols(full, W[n].shape[-1]) if sharded else full

    delta, new_m, new_v = {}, {}, {}
    for n in _WEIGHTS:
        shp = W[n].shape
        d, nm, nv = _adamw(f"adamw_{n}", _as2d(W[n]), _as2d(grads[n]), _as2d(M1[n]), _as2d(V2[n]))
        delta[n], new_m[n], new_v[n] = d.reshape(shp), nm.reshape(shp), nv.reshape(shp)

    return (loss, grad_x, *[grads[n] for n in _WEIGHTS], *[delta[n] for n in _WEIGHTS],
            *[new_m[n] for n in _WEIGHTS], *[new_v[n] for n in _WEIGHTS])
```

```python
import functools
import math

import jax
import jax.numpy as jnp
import numpy as np
from jax import lax
from jax.experimental import pallas as pl
from jax.experimental.pallas import tpu as pltpu

F32 = jnp.float32
BF16 = jnp.bfloat16
MESH = pl.DeviceIdType.MESH

EPS = 1e-6
CHUNK = 64
ROW_BLOCK = 128
SWA_WINDOW = 128
SWA_GROUP = 8
POOL_WINDOWS = (2, 4, 8, 16)
GDN_CONV = 4
ROPE_THETA = 10000.0
NEG = -1e30
N_CHIPS = 4
N_DEV = 8
LANES = 128
WIDE_TILE = 896
VMEM_LIMIT = 56 * 1024 * 1024
VMEM_BUDGET = 36 * 1024 * 1024

ADAM_LR = 0.001
ADAM_B1 = 0.9
ADAM_B2 = 0.999
ADAM_EPS = 1e-08
ADAM_WD = 0.01
ADAM_STEP = 10

_NN = ((1,), (0,))
_NT = ((1,), (1,))
_TN = ((0,), (0,))


def _cp(dims=None):
    return pltpu.CompilerParams(dimension_semantics=dims, vmem_limit_bytes=VMEM_LIMIT)


def _round_up(n, m):
    return -(-n // m) * m


def _div(n, cands):
    for c in cands:
        if c <= n and n % c == 0:
            return c
    return n


def _pad_cols(n):
    return _round_up(n, WIDE_TILE) if n > 2048 else _round_up(n, LANES)


def _dg(a, b, dims, prec=None, batched=False):
    if batched:
        dims = (((dims[0][0] + 1,), (dims[1][0] + 1,)), ((0,), (0,)))
    else:
        dims = (dims, ((), ()))
    return lax.dot_general(a, b, dims, precision=prec, preferred_element_type=F32)


def _make_dots(cast, batched=False):
    prec = None if cast is not None else lax.Precision.HIGHEST
    c = (lambda t: t.astype(cast)) if cast is not None else (lambda t: t)
    rnn = lambda a, b: _dg(c(a), c(b), _NN, prec, batched)
    rnt = lambda a, b: _dg(c(a), c(b), _NT, prec, batched)
    rtn = lambda a, b: _dg(c(a), c(b), _TN, prec, batched)

    @jax.custom_vjp
    def nn(a, b):
        return rnn(a, b)
    nn.defvjp(lambda a, b: (rnn(a, b), (a, b)), lambda r, ct: (rnt(ct, r[1]), rtn(r[0], ct)))

    @jax.custom_vjp
    def nt(a, b):
        return rnt(a, b)
    nt.defvjp(lambda a, b: (rnt(a, b), (a, b)), lambda r, ct: (rnn(ct, r[1]), rtn(ct, r[0])))

    @jax.custom_vjp
    def tn(a, b):
        return rtn(a, b)
    tn.defvjp(lambda a, b: (rtn(a, b), (a, b)), lambda r, ct: (rnt(r[1], ct), rnn(r[0], ct)))
    return nn, nt, tn, rnn, rnt, rtn


_bnn, _bnt, _btn, _rbnn, _rbnt, _rbtn = _make_dots(BF16)
_hnn, _hnt, _htn, _rhnn, _rhnt, _rhtn = _make_dots(None)
_qnn, _qnt, _qtn, _rqnn, _rqnt, _rqtn = _make_dots(BF16, batched=True)


def _sigmoid(x):
    return 1.0 / (1.0 + jnp.exp(-x))


def _silu(x):
    return x * _sigmoid(x)


def _softplus(x):
    return jnp.maximum(x, 0.0) + jnp.log(1.0 + jnp.exp(-jnp.abs(x)))


def _log_sigmoid(x):
    return -_softplus(-x)


def _iota(shape, dim):
    return lax.broadcasted_iota(jnp.int32, shape, dim)


def _matmul(name, form, grid, a_ops, b_ops, acc_ids, n_acc, extras, outs, epilogue, tm, tn,
            alias_inputs=(), alias_map=None):
    na, nb, ne, nal, no = len(a_ops), len(b_ops), len(extras), len(alias_inputs), len(outs)
    nk = grid[2]
    dims = {"nn": _NN, "nt": _NT, "tn": _TN}[form]

    def body(*refs):
        a_refs = refs[:na]
        b_refs = refs[na:na + nb]
        e_refs = refs[na + nb:na + nb + ne]
        o_refs = refs[na + nb + ne + nal:na + nb + ne + nal + no]
        acc = refs[-1]
        k = pl.program_id(2)

        @pl.when(k == 0)
        def _():
            acc[...] = jnp.zeros_like(acc)

        for p in range(na):
            acc[acc_ids[p]] += _dg(a_refs[p][...].astype(BF16), b_refs[p][...].astype(BF16), dims)

        @pl.when(k == nk - 1)
        def _():
            res = epilogue([acc[i] for i in range(n_acc)], [e[...] for e in e_refs])
            for o, r in zip(o_refs, res):
                o[...] = r.astype(o.dtype)

    ops = list(a_ops) + list(b_ops) + list(extras)
    in_specs = [pl.BlockSpec(bs, im) for (_, bs, im) in ops] + [pl.BlockSpec(memory_space=pl.ANY)] * nal
    aliases = {}
    if alias_map:
        aliases = {len(ops) + i: o for i, o in alias_map.items()}
    res = pl.pallas_call(
        body, name=name, grid=grid,
        in_specs=in_specs,
        out_specs=[pl.BlockSpec(bs, im) for (_, bs, im) in outs],
        out_shape=[s for (s, _, _) in outs],
        scratch_shapes=[pltpu.VMEM((n_acc, tm, tn), F32)],
        input_output_aliases=aliases,
        compiler_params=_cp(("parallel", "parallel", "arbitrary")),
    )(*[o[0] for o in ops], *alias_inputs)
    return res


def _tiles(M, N, K, fixed_bytes_per_tm_tn, k_cands=(512, 384, 256, 128), n_cands=(2048, 1792, 1408, 1280, 1024, 896, 768, 640, 512, 384, 256, 128),
           m_cands=(1408, 1056, 704, 528, 384, 256, 128, 64, 32, 16, 8), a_bytes=2, b_bytes=2, n_pairs=1):
    tn = _div(N, n_cands)
    tk = _div(K, k_cands)
    for tm in m_cands:
        if tm > M or M % tm:
            continue
        est = tm * tn * fixed_bytes_per_tm_tn + n_pairs * 2 * (tm * tk * a_bytes + tk * tn * b_bytes)
        if est <= VMEM_BUDGET:
            return tm, tn, tk
    return _div(M, (8,)), tn, tk


def _mm(name, form, a, b, out_dtype, epilogue=None, extras=(), extra_kinds=(), n_out=1, out_dtypes=None):
    if form == "nn":
        (M, K), N = a.shape, b.shape[1]
    elif form == "nt":
        (M, K), N = a.shape, b.shape[0]
    else:
        (K, M), N = a.shape, b.shape[1]
    out_dtypes = out_dtypes or [out_dtype] * n_out
    per = 4 + sum(2 * jnp.dtype(d).itemsize for d in out_dtypes)
    per += sum(2 * e.dtype.itemsize for e, kd in zip(extras, extra_kinds) if kd == "tile")
    kc = (1408, 1056, 704, 512, 384, 256, 128) if form == "tn" else (896, 512, 384, 256, 128)
    mc = (1024, 896, 768, 640, 512, 384, 256, 128) if form == "tn" else (1408, 1056, 704, 528, 384, 256, 128, 64, 32, 16, 8)
    tm, tn, tk = _tiles(M, N, K, per, k_cands=kc, m_cands=mc, a_bytes=a.dtype.itemsize, b_bytes=b.dtype.itemsize)
    grid = (M // tm, N // tn, K // tk)
    if form == "nn":
        a_op = (a, (tm, tk), lambda i, j, k: (i, k))
        b_op = (b, (tk, tn), lambda i, j, k: (k, j))
    elif form == "nt":
        a_op = (a, (tm, tk), lambda i, j, k: (i, k))
        b_op = (b, (tn, tk), lambda i, j, k: (j, k))
    else:
        a_op = (a, (tk, tm), lambda i, j, k: (k, i))
        b_op = (b, (tk, tn), lambda i, j, k: (k, j))
    e_ops = []
    for e, kd in zip(extras, extra_kinds):
        if kd == "tile":
            e_ops.append((e, (tm, tn), lambda i, j, k: (i, j)))
        else:
            e_ops.append((e, (1, tn), lambda i, j, k: (0, j)))
    outs = [(jax.ShapeDtypeStruct((M, N), d), (tm, tn), lambda i, j, k: (i, j)) for d in out_dtypes]

    def epi(accs, ex):
        if epilogue is None:
            return [accs[0]]
        r = epilogue(accs[0], *ex)
        return list(r) if isinstance(r, (tuple, list)) else [r]

    res = _matmul(name, form, grid, [a_op], [b_op], [0], 1, e_ops, outs, epi, tm, tn)
    return res[0] if len(res) == 1 else res


def _rowwise(name, fn, ins, outs, tr, rows):
    n_in, n_out = len(ins), len(outs)
    nblk = rows // tr

    def body(*refs):
        i = pl.program_id(0)
        vals = fn(i * tr, *[r[...] for r in refs[:n_in]])
        for o, v, (_, _, kind) in zip(refs[n_in:], vals, outs):
            if kind == "row":
                o[...] = v.astype(o.dtype)
            else:
                @pl.when(i == 0)
                def _(o=o):
                    o[...] = jnp.zeros_like(o)
                o[...] += v.astype(o.dtype)

    in_specs = []
    for arr, w, cb in ins:
        if arr.shape[0] == 1 and rows != 1:
            in_specs.append(pl.BlockSpec((1, w), lambda i, cb=cb: (0, cb)))
        else:
            in_specs.append(pl.BlockSpec((tr, w), lambda i, cb=cb: (i, cb)))
    out_specs, out_shape = [], []
    for w, d, kind in outs:
        if kind == "row":
            out_specs.append(pl.BlockSpec((tr, w), lambda i: (i, 0)))
            out_shape.append(jax.ShapeDtypeStruct((rows, w), d))
        else:
            out_specs.append(pl.BlockSpec((1, w), lambda i: (0, 0)))
            out_shape.append(jax.ShapeDtypeStruct((1, w), d))
    return pl.pallas_call(
        body, name=name, grid=(nblk,), in_specs=in_specs, out_specs=out_specs, out_shape=out_shape,
        compiler_params=_cp(("arbitrary",)),
    )(*[a for a, _, _ in ins])


def _full(arr):
    return (arr, arr.shape[1], 0)


def _rmsnorm_fwd(name, h, w, out_dtype):
    D = h.shape[1]

    def fn(_, hb, wb):
        rstd = lax.rsqrt(jnp.mean(hb * hb, axis=1, keepdims=True) + EPS)
        return [hb * rstd * wb]
    return _rowwise(name, fn, [_full(h), _full(w)], [(D, out_dtype, "row")], ROW_BLOCK, h.shape[0])[0]


def _rmsnorm_bwd(name, h, w, dn, dh_in):
    D = h.shape[1]

    def fn(_, hb, wb, dnb, dhb):
        rstd = lax.rsqrt(jnp.mean(hb * hb, axis=1, keepdims=True) + EPS)
        xhat = hb * rstd
        dxh = dnb.astype(F32) * wb
        dh = rstd * (dxh - xhat * jnp.mean(dxh * xhat, axis=1, keepdims=True))
        return [dhb + dh, jnp.sum(dnb.astype(F32) * xhat, axis=0, keepdims=True)]
    return _rowwise(name, fn, [_full(h), _full(w), _full(dn), _full(dh_in)],
                    [(D, F32, "row"), (D, F32, "acc")], ROW_BLOCK, h.shape[0])


def _headnorm_fn(group, act):
    def f(o, gate, w):
        rstd = lax.rsqrt(jnp.mean(o * o, axis=1, keepdims=True) + EPS)
        return o * rstd * w * act(gate)
    return f


def _headnorm_fwd(name, o, gate_arr, gate_w, gate_cb, w, group, act):
    N = o.shape[1]
    f = _headnorm_fn(group, act)

    def fn(_, ob, gb, wb):
        parts = [f(ob[:, s:s + group], gb[:, s:s + group], wb[:, s:s + group]) for s in range(0, N, group)]
        return [jnp.concatenate(parts, axis=1)]
    return _rowwise(name, fn, [_full(o), (gate_arr, gate_w, gate_cb), _full(w)], [(N, BF16, "row")], ROW_BLOCK, o.shape[0])[0]


def _headnorm_bwd(name, o, gate_arr, gate_w, gate_cb, w, dout, group, act):
    N = o.shape[1]
    f = _headnorm_fn(group, act)

    def fn(_, ob, gb, wb, db):
        dos, dgs, dws = [], [], []
        for s in range(0, N, group):
            _, vjp = jax.vjp(f, ob[:, s:s + group], gb[:, s:s + group], jnp.broadcast_to(wb[:, s:s + group], (ob.shape[0], group)))
            do, dgt, dw = vjp(db[:, s:s + group])
            dos.append(do)
            dgs.append(dgt)
            dws.append(jnp.sum(dw, axis=0, keepdims=True))
        return [jnp.concatenate(dos, axis=1), jnp.concatenate(dgs, axis=1), jnp.concatenate(dws, axis=1)]
    return _rowwise(name, fn, [_full(o), (gate_arr, gate_w, gate_cb), _full(w), _full(dout)],
                    [(N, F32, "row"), (N, F32, "row"), (N, F32, "acc")], ROW_BLOCK, o.shape[0])


def _colsum(name, a):
    def fn(_, ab):
        return [jnp.sum(ab.astype(F32), axis=0, keepdims=True)]
    tr = _div(a.shape[0], (512, 384, 256, 128, 64))
    return _rowwise(name, fn, [_full(a)], [(a.shape[1], F32, "acc")], tr, a.shape[0])[0]


def _loss_head(name, h, w, tgt, n_meta, seq):
    D = h.shape[1]
    tr = ROW_BLOCK

    def fn(row0, hb, wb, tb):
        row = row0 + _iota((tr, 1), 0)
        valid = (row >= n_meta) & (row < n_meta + seq)
        rstd = lax.rsqrt(jnp.mean(hb * hb, axis=1, keepdims=True) + EPS)
        xhat = hb * rstd
        err = jnp.where(valid, xhat * wb - tb, 0.0)
        loss = 0.5 * jnp.sum(jnp.mean(err * err, axis=1, keepdims=True), axis=0, keepdims=True)
        dy = err * (1.0 / D)
        dxh = dy * wb
        dh = rstd * (dxh - xhat * jnp.mean(dxh * xhat, axis=1, keepdims=True))
        return [dh, jnp.sum(dy * xhat, axis=0, keepdims=True), jnp.broadcast_to(loss, (1, LANES))]
    return _rowwise(name, fn, [_full(h), _full(w), _full(tgt)],
                    [(D, F32, "row"), (D, F32, "acc"), (LANES, F32, "acc")], tr, h.shape[0])


def _chunk_valid(ci, n_meta):
    lim = jnp.where(ci == 0, n_meta, CHUNK)
    return _iota((CHUNK, 1), 0) < lim, _iota((1, CHUNK), 1) < lim


def _tri_masks():
    r = _iota((CHUNK, CHUNK), 0)
    c = _iota((CHUNK, CHUNK), 1)
    return r >= c, r > c, r <= c


def _mlstm_chunk(vc, vr, m_st, c_st, n_st, q, k, v, li_c, lf_c, li_r, lf_r):
    tril, _, triu = _tri_masks()
    dk = q.shape[1]
    li_c = jnp.where(vc, li_c, NEG)
    li_r = jnp.where(vr, li_r, NEG)
    lf_c = jnp.where(vc, _log_sigmoid(lf_c), 0.0)
    lf_r = jnp.where(vr, _log_sigmoid(lf_r), 0.0)
    b_c = jnp.sum(jnp.where(tril, lf_r, 0.0), axis=1, keepdims=True)
    b_r = jnp.sum(jnp.where(triu, lf_c, 0.0), axis=0, keepdims=True)
    b_last = jnp.sum(lf_r, axis=1, keepdims=True)
    log_w = jnp.where(tril, b_c - b_r + li_r, NEG)
    log_init = b_c + m_st
    m_t = lax.stop_gradient(jnp.maximum(log_init, jnp.max(log_w, axis=1, keepdims=True)))
    w = jnp.exp(log_w - m_t)
    w_init = jnp.exp(log_init - m_t)
    qs = q * (dk ** -0.5)
    qk = _bnt(qs, k) * w
    num = w_init * _bnn(qs, c_st) + _bnn(qk, v)
    den = w_init * jnp.sum(qs * n_st, axis=1, keepdims=True) + jnp.sum(qk, axis=1, keepdims=True)
    h = num / jnp.maximum(jnp.abs(den), jnp.exp(-m_t))
    log_end_init = b_last + m_st
    log_end_r = b_last - b_r + li_r
    m_new = lax.stop_gradient(jnp.maximum(log_end_init, jnp.max(log_end_r, axis=1, keepdims=True)))
    a_init = jnp.exp(log_end_init - m_new)
    a_c = jnp.exp(b_last - b_c + li_c - m_new)
    ka = k * a_c
    c_new = a_init * c_st + _btn(ka, v)
    n_new = a_init * n_st + jnp.sum(ka, axis=0, keepdims=True)
    return (c_new, n_new, h), m_new


HEADS_PER_STEP = 4


def _head_batch(heads):
    return _div(heads, (HEADS_PER_STEP, 2, 1))


def _mlstm_io_specs(heads, hb, dk, dv, NC, rev):
    ci = (lambda c: NC - 1 - c) if rev else (lambda c: c)
    nb = heads // hb
    q = pl.BlockSpec((CHUNK, hb * dk), lambda h, c: (ci(c), h))
    k = pl.BlockSpec((CHUNK, hb * dk), lambda h, c: (ci(c), nb + h))
    v = pl.BlockSpec((CHUNK, hb * dv), lambda h, c: (ci(c), (2 * heads * dk) // (hb * dv) + h))
    col = pl.BlockSpec((hb, CHUNK, 1), lambda h, c: (h, ci(c), 0))
    row = pl.BlockSpec((hb, 1, 1, CHUNK), lambda h, c: (h, ci(c), 0, 0))
    st = [pl.BlockSpec((hb, 1, dk, dv), lambda h, c: (h, ci(c), 0, 0)),
          pl.BlockSpec((hb, 1, 1, dk), lambda h, c: (h, ci(c), 0, 0)),
          pl.BlockSpec((hb, 1, 1, 1), lambda h, c: (h, ci(c), 0, 0))]
    wide = pl.BlockSpec((CHUNK, hb * dv), lambda h, c: (ci(c), h))
    return q, k, v, col, row, st, wide


def _mlstm_core_fwd(pc, li_c, lf_c, li_r, lf_r, heads, dk, dv, n_meta):
    TC = pc.shape[0]
    NC = TC // CHUNK
    hb = _head_batch(heads)

    def body(q_ref, k_ref, v_ref, lic, lfc, lir, lfr, h_ref, cs_ref, ns_ref, ms_ref, c_s, n_s, m_s):
        ci = pl.program_id(1)

        @pl.when(ci == 0)
        def _():
            c_s[...] = jnp.zeros_like(c_s)
            n_s[...] = jnp.zeros_like(n_s)
            m_s[...] = jnp.zeros_like(m_s)

        vc, vr = _chunk_valid(ci, n_meta)
        for j in range(hb):
            cs_ref[j, 0] = c_s[j]
            ns_ref[j, 0] = n_s[j]
            ms_ref[j, 0] = m_s[j]
            (c_new, n_new, h), m_new = _mlstm_chunk(
                vc, vr, m_s[j], c_s[j], n_s[j], q_ref[:, j * dk:(j + 1) * dk], k_ref[:, j * dk:(j + 1) * dk],
                v_ref[:, j * dv:(j + 1) * dv], lic[j], lfc[j], lir[j, 0], lfr[j, 0])
            h_ref[:, j * dv:(j + 1) * dv] = h
            c_s[j] = c_new
            n_s[j] = n_new
            m_s[j] = m_new

    q, k, v, col, row, st, wide = _mlstm_io_specs(heads, hb, dk, dv, NC, False)
    return pl.pallas_call(
        body, name="mlstm_core_fwd", grid=(heads // hb, NC),
        in_specs=[q, k, v, col, col, row, row],
        out_specs=[wide] + st,
        out_shape=[jax.ShapeDtypeStruct((TC, heads * dv), F32),
                   jax.ShapeDtypeStruct((heads, NC, dk, dv), F32),
                   jax.ShapeDtypeStruct((heads, NC, 1, dk), F32),
                   jax.ShapeDtypeStruct((heads, NC, 1, 1), F32)],
        scratch_shapes=[pltpu.VMEM((hb, dk, dv), F32), pltpu.VMEM((hb, 1, dk), F32), pltpu.VMEM((hb, 1, 1), F32)],
        compiler_params=_cp(("parallel", "arbitrary")),
    )(pc, pc, pc, li_c, lf_c, li_r, lf_r)


def _mlstm_core_bwd(pc, li_c, lf_c, li_r, lf_r, cs, ns, ms, dh, heads, dk, dv, n_meta):
    TC = pc.shape[0]
    NC = TC // CHUNK
    hb = _head_batch(heads)

    def body(q_ref, k_ref, v_ref, lic, lfc, lir, lfr, cs_ref, ns_ref, ms_ref, dh_ref,
             dq_ref, dk_ref, dv_ref, dlic, dlfc, dlir, dlfr, dc_s, dn_s):
        step = pl.program_id(1)
        ci = NC - 1 - step

        @pl.when(step == 0)
        def _():
            dc_s[...] = jnp.zeros_like(dc_s)
            dn_s[...] = jnp.zeros_like(dn_s)

        vc, vr = _chunk_valid(ci, n_meta)
        for j in range(hb):
            ks, vs = slice(j * dk, (j + 1) * dk), slice(j * dv, (j + 1) * dv)
            m_st = ms_ref[j, 0]
            fn = lambda *a, m_st=m_st: _mlstm_chunk(vc, vr, m_st, *a)
            _, vjp, _ = jax.vjp(fn, cs_ref[j, 0], ns_ref[j, 0], q_ref[:, ks], k_ref[:, ks], v_ref[:, vs],
                                lic[j], lfc[j], lir[j, 0], lfr[j, 0], has_aux=True)
            dc, dn, dq, dkk, dvv, g0, g1, g2, g3 = vjp((dc_s[j], dn_s[j], dh_ref[:, vs]))
            dq_ref[:, ks] = dq
            dk_ref[:, ks] = dkk
            dv_ref[:, vs] = dvv
            dlic[j] = g0
            dlfc[j] = g1
            dlir[j, 0] = g2
            dlfr[j, 0] = g3
            dc_s[j] = dc
            dn_s[j] = dn

    q, k, v, col, row, st, wide = _mlstm_io_specs(heads, hb, dk, dv, NC, True)
    return pl.pallas_call(
        body, name="mlstm_core_bwd", grid=(heads // hb, NC),
        in_specs=[q, k, v, col, col, row, row] + st + [wide],
        out_specs=[q, q, wide, col, col, row, row],
        out_shape=[jax.ShapeDtypeStruct((TC, heads * dk), F32), jax.ShapeDtypeStruct((TC, heads * dk), F32),
                   jax.ShapeDtypeStruct((TC, heads * dv), F32),
                   jax.ShapeDtypeStruct(li_c.shape, F32), jax.ShapeDtypeStruct(lf_c.shape, F32),
                   jax.ShapeDtypeStruct(li_r.shape, F32), jax.ShapeDtypeStruct(lf_r.shape, F32)],
        scratch_shapes=[pltpu.VMEM((hb, dk, dv), F32), pltpu.VMEM((hb, 1, dk), F32)],
        compiler_params=_cp(("parallel", "arbitrary")),
    )(pc, pc, pc, li_c, lf_c, li_r, lf_r, cs, ns, ms, dh)


@jax.custom_vjp
def _tri_solve(low, rhs):
    return _tri_solve_fwd(low, rhs)[0]


def _tri_solve_fwd(low, rhs):
    levels = int(math.log2(low.shape[-1]))
    p = -low
    r = p
    for i in range(levels):
        if i > 0:
            r = r + p + _rqnn(p, r)
        if i < levels - 1:
            p = _rqnn(p, p)
    sol = rhs + _rqnn(r, rhs)
    return sol, (r, sol)


def _tri_solve_bwd(res, ct):
    r, sol = res
    d_rhs = ct + _rqtn(r, ct)
    return -_rqnt(d_rhs, sol), d_rhs


_tri_solve.defvjp(_tri_solve_fwd, _tri_solve_bwd)


def _l2norm(x):
    return x * lax.rsqrt(jnp.sum(x * x, axis=-1, keepdims=True) + EPS)


def _lane_heads(ref, n, width):
    return jnp.stack([ref[:, j * width:(j + 1) * width] for j in range(n)], axis=0)


def _gdn_chunk(vc, vr, s_st, q, k, v, g_c, b_c, g_r):
    tril, strict, triu = _tri_masks()
    dk = q.shape[-1]
    pair = lambda t: jnp.concatenate([t[j:j + 1] for j in range(t.shape[0]) for _ in (0, 1)], axis=0)
    qn = _l2norm(q) * (dk ** -0.5)
    kn = _l2norm(k)
    qk = pair(_qnt(qn, kn))
    qn, kn = pair(qn), pair(kn)
    g_c = jnp.where(vc, g_c, 0.0)
    g_r = jnp.where(vr, g_r, 0.0)
    b_c = jnp.where(vc, b_c, 0.0)
    gc_c = jnp.sum(jnp.where(tril, g_r, 0.0), axis=2, keepdims=True)
    gc_r = jnp.sum(jnp.where(triu, g_c, 0.0), axis=1, keepdims=True)
    g_last = jnp.sum(g_r, axis=2, keepdims=True)
    decay = jnp.exp(jnp.where(tril, gc_c - gc_r, NEG))
    kb = kn * b_c
    low = jnp.where(strict, _qnt(kb, kn) * decay, 0.0)
    eg = jnp.exp(gc_c)
    sol = _tri_solve(low, jnp.concatenate([v * b_c, kb * eg], axis=2))
    u_vec, w_vec = sol[:, :, :dk], sol[:, :, dk:]
    v_new = u_vec - _qnn(w_vec, s_st)
    o = _qnn(qn * eg, s_st) + _qnn(qk * decay, v_new)
    s_new = jnp.exp(g_last) * s_st + _qtn(kn * jnp.exp(g_last - gc_c), v_new)
    return s_new, o


def _gdn_core_fwd(qkv, g_c, b_c, g_r, qk_heads, dk, n_meta):
    TC = qkv.shape[0]
    NC = TC // CHUNK
    H = qk_heads

    hb = _head_batch(H)

    def body(q_ref, k_ref, v_ref, gc, bc, gr, o_ref, st_ref, s_s):
        ci = pl.program_id(1)

        @pl.when(ci == 0)
        def _():
            s_s[...] = jnp.zeros_like(s_s)

        vc, vr = _chunk_valid(ci, n_meta)
        s_st = s_s[...]
        st_ref[:, 0] = s_st.reshape(hb, 2, dk, dk)
        s_new, o = _gdn_chunk(vc, vr, s_st, _lane_heads(q_ref, hb, dk), _lane_heads(k_ref, hb, dk),
                              _lane_heads(v_ref, 2 * hb, dk), gc[...], bc[...], gr[:, 0])
        for b in range(2 * hb):
            o_ref[:, b * dk:(b + 1) * dk] = o[b]
        s_s[...] = s_new

    q, k, v, col, row, st = _gdn_io_specs(H, hb, dk, NC, False)
    return pl.pallas_call(
        body, name="gdn_core_fwd", grid=(H // hb, NC),
        in_specs=[q, k, v, col, col, row],
        out_specs=[_gdn_wide_spec(hb, dk, NC, False), st],
        out_shape=[jax.ShapeDtypeStruct((TC, 2 * H * dk), F32), jax.ShapeDtypeStruct((H, NC, 2, dk, dk), F32)],
        scratch_shapes=[pltpu.VMEM((2 * hb, dk, dk), F32)],
        compiler_params=_cp(("parallel", "arbitrary")),
    )(qkv, qkv, qkv, g_c, b_c, g_r)


def _gdn_wide_spec(hb, dk, NC, rev):
    ci = (lambda c: NC - 1 - c) if rev else (lambda c: c)
    return pl.BlockSpec((CHUNK, 2 * hb * dk), lambda h, c: (ci(c), h))


def _gdn_io_specs(H, hb, dk, NC, rev):
    ci = (lambda c: NC - 1 - c) if rev else (lambda c: c)
    nb = H // hb
    q = pl.BlockSpec((CHUNK, hb * dk), lambda h, c: (ci(c), h))
    k = pl.BlockSpec((CHUNK, hb * dk), lambda h, c: (ci(c), nb + h))
    v = pl.BlockSpec((CHUNK, 2 * hb * dk), lambda h, c: (ci(c), nb + h))
    col = pl.BlockSpec((2 * hb, CHUNK, 1), lambda h, c: (h, ci(c), 0))
    row = pl.BlockSpec((2 * hb, 1, 1, CHUNK), lambda h, c: (h, ci(c), 0, 0))
    st = pl.BlockSpec((hb, 1, 2, dk, dk), lambda h, c: (h, ci(c), 0, 0, 0))
    return q, k, v, col, row, st


def _gdn_core_bwd(qkv, g_c, b_c, g_r, st, do, qk_heads, dk, n_meta):
    TC = qkv.shape[0]
    NC = TC // CHUNK
    H = qk_heads
    hb = _head_batch(H)

    def body(q_ref, k_ref, v_ref, gc, bc, gr, st_ref, do_ref, dq_ref, dk_ref, dv_ref, dgc, dbc, dgr, ds_s):
        step = pl.program_id(1)
        ci = NC - 1 - step

        @pl.when(step == 0)
        def _():
            ds_s[...] = jnp.zeros_like(ds_s)

        vc, vr = _chunk_valid(ci, n_meta)
        fn = lambda *a: _gdn_chunk(vc, vr, *a)
        _, vjp = jax.vjp(fn, st_ref[:, 0].reshape(2 * hb, dk, dk), _lane_heads(q_ref, hb, dk), _lane_heads(k_ref, hb, dk),
                         _lane_heads(v_ref, 2 * hb, dk), gc[...], bc[...], gr[:, 0])
        ds, dq, dkk, dvv, d_gc, d_bc, d_gr = vjp((ds_s[...], _lane_heads(do_ref, 2 * hb, dk)))
        ds_s[...] = ds
        for j in range(hb):
            dq_ref[:, j * dk:(j + 1) * dk] = dq[j]
            dk_ref[:, j * dk:(j + 1) * dk] = dkk[j]
        for b in range(2 * hb):
            dv_ref[:, b * dk:(b + 1) * dk] = dvv[b]
        dgc[...] = d_gc
        dbc[...] = d_bc
        dgr[:, 0] = d_gr

    q, k, v, col, row, stspec = _gdn_io_specs(H, hb, dk, NC, True)
    wide = _gdn_wide_spec(hb, dk, NC, True)
    return pl.pallas_call(
        body, name="gdn_core_bwd", grid=(H // hb, NC),
        in_specs=[q, k, v, col, col, row, stspec, wide],
        out_specs=[q, q, wide, col, col, row],
        out_shape=[jax.ShapeDtypeStruct((TC, H * dk), F32), jax.ShapeDtypeStruct((TC, H * dk), F32),
                   jax.ShapeDtypeStruct((TC, 2 * H * dk), F32),
                   jax.ShapeDtypeStruct(g_c.shape, F32), jax.ShapeDtypeStruct(g_c.shape, F32),
                   jax.ShapeDtypeStruct(g_r.shape, F32)],
        scratch_shapes=[pltpu.VMEM((2 * hb, dk, dk), F32)],
        compiler_params=_cp(("parallel", "arbitrary")),
    )(qkv, qkv, qkv, g_c, b_c, g_r, st, do)


def _gdn_gate_fn(a_pre, b_pre, a_log, dt_bias):
    return -jnp.exp(a_log) * _softplus(a_pre + dt_bias), _sigmoid(b_pre)


def _gdn_gates_fwd(a_pre, b_pre, a_log, dt_bias):
    n = a_pre.shape[1]

    def fn(_, ab, bb, al, dt):
        g, beta = _gdn_gate_fn(ab, bb, al, dt)
        return [g, beta]
    return _rowwise("gdn_gates_fwd", fn, [_full(a_pre), _full(b_pre), _full(a_log), _full(dt_bias)],
                    [(n, F32, "row"), (n, F32, "row")], ROW_BLOCK, a_pre.shape[0])


def _gdn_gates_bwd(a_pre, b_pre, a_log, dt_bias, dg, dbeta):
    n = a_pre.shape[1]

    def fn(_, ab, bb, al, dt, dgb, dbb):
        rows = ab.shape[0]
        _, vjp = jax.vjp(_gdn_gate_fn, ab, bb, jnp.broadcast_to(al, (rows, n)), jnp.broadcast_to(dt, (rows, n)))
        da, db, dal, ddt = vjp((dgb, dbb))
        return [da, db, jnp.sum(dal, axis=0, keepdims=True), jnp.sum(ddt, axis=0, keepdims=True)]
    return _rowwise("gdn_gates_bwd", fn, [_full(a_pre), _full(b_pre), _full(a_log), _full(dt_bias), _full(dg), _full(dbeta)],
                    [(n, F32, "row"), (n, F32, "row"), (n, F32, "acc"), (n, F32, "acc")], ROW_BLOCK, a_pre.shape[0])


def _shift_down(cur, prev, j):
    row = _iota(cur.shape, 0)
    return jnp.where(row >= j, pltpu.roll(cur, j, 0), pltpu.roll(prev, j, 0))


def _shift_up(cur, nxt, j):
    n = cur.shape[0]
    row = _iota(cur.shape, 0)
    return jnp.where(row < n - j, pltpu.roll(cur, n - j, 0), pltpu.roll(nxt, n - j, 0))


def _conv_acc(cur, prev, w):
    acc = cur * w[GDN_CONV - 1:GDN_CONV, :]
    for j in range(1, GDN_CONV):
        acc = acc + _shift_down(cur, prev, j) * w[GDN_CONV - 1 - j:GDN_CONV - j, :]
    return acc


def _conv_tiles(width):
    return _div(width, (1024, 512, 256, 128))


def _conv_fwd(p, w, width):
    TP = p.shape[0]
    nb, tn = TP // ROW_BLOCK, _conv_tiles(width)

    def body(cur_ref, prev_ref, w_ref, y_ref):
        i = pl.program_id(1)
        prev = jnp.where(i > 0, prev_ref[...], 0.0)
        y_ref[...] = _silu(_conv_acc(cur_ref[...], prev, w_ref[...]))

    return pl.pallas_call(
        body, name="gdn_conv_fwd", grid=(width // tn, nb),
        in_specs=[pl.BlockSpec((ROW_BLOCK, tn), lambda j, i: (i, j)),
                  pl.BlockSpec((ROW_BLOCK, tn), lambda j, i: (jnp.maximum(i - 1, 0), j)),
                  pl.BlockSpec((GDN_CONV, tn), lambda j, i: (0, j))],
        out_specs=pl.BlockSpec((ROW_BLOCK, tn), lambda j, i: (i, j)),
        out_shape=jax.ShapeDtypeStruct((TP, width), F32),
        compiler_params=_cp(("parallel", "arbitrary")),
    )(p, p, w)


def _conv_bwd_pre(p, w, dy, width):
    TP = p.shape[0]
    nb, tn = TP // ROW_BLOCK, _conv_tiles(width)

    def body(cur_ref, prev_ref, w_ref, dy_ref, da_ref, dw_ref):
        i = pl.program_id(1)
        cur = cur_ref[...]
        prev = jnp.where(i > 0, prev_ref[...], 0.0)
        acc = _conv_acc(cur, prev, w_ref[...])
        s = _sigmoid(acc)
        da = dy_ref[...] * (s * (1.0 + acc * (1.0 - s)))
        da_ref[...] = da

        @pl.when(i == 0)
        def _():
            dw_ref[...] = jnp.zeros_like(dw_ref)

        rows = [jnp.sum(da * (cur if j == 0 else _shift_down(cur, prev, j)), axis=0, keepdims=True)
                for j in range(GDN_CONV - 1, -1, -1)]
        dw_ref[...] += jnp.concatenate(rows, axis=0)

    return pl.pallas_call(
        body, name="gdn_conv_bwd_pre", grid=(width // tn, nb),
        in_specs=[pl.BlockSpec((ROW_BLOCK, tn), lambda j, i: (i, j)),
                  pl.BlockSpec((ROW_BLOCK, tn), lambda j, i: (jnp.maximum(i - 1, 0), j)),
                  pl.BlockSpec((GDN_CONV, tn), lambda j, i: (0, j)),
                  pl.BlockSpec((ROW_BLOCK, tn), lambda j, i: (i, j))],
        out_specs=[pl.BlockSpec((ROW_BLOCK, tn), lambda j, i: (i, j)), pl.BlockSpec((GDN_CONV, tn), lambda j, i: (0, j))],
        out_shape=[jax.ShapeDtypeStruct((TP, width), F32), jax.ShapeDtypeStruct((GDN_CONV, width), F32)],
        compiler_params=_cp(("parallel", "arbitrary")),
    )(p, p, w, dy)


def _conv_bwd_dx(da, w, width):
    TP = da.shape[0]
    nb, tn = TP // ROW_BLOCK, _conv_tiles(width)

    def body(cur_ref, nxt_ref, w_ref, dx_ref):
        i = pl.program_id(1)
        cur = cur_ref[...]
        nxt = jnp.where(i < nb - 1, nxt_ref[...], 0.0)
        w_all = w_ref[...]
        dx = cur * w_all[GDN_CONV - 1:GDN_CONV, :]
        for j in range(1, GDN_CONV):
            dx = dx + _shift_up(cur, nxt, j) * w_all[GDN_CONV - 1 - j:GDN_CONV - j, :]
        dx_ref[...] = dx

    return pl.pallas_call(
        body, name="gdn_conv_bwd_dx", grid=(width // tn, nb),
        in_specs=[pl.BlockSpec((ROW_BLOCK, tn), lambda j, i: (i, j)),
                  pl.BlockSpec((ROW_BLOCK, tn), lambda j, i: (jnp.minimum(i + 1, nb - 1), j)),
                  pl.BlockSpec((GDN_CONV, tn), lambda j, i: (0, j))],
        out_specs=pl.BlockSpec((ROW_BLOCK, tn), lambda j, i: (i, j)),
        out_shape=jax.ShapeDtypeStruct((TP, width), F32),
        compiler_params=_cp(("parallel", "arbitrary")),
    )(da, da, w)


def _pool_bands(i, win):
    n = ROW_BLOCK
    t = _iota((n, n), 0)
    s = _iota((n, n), 1)
    cnt = jnp.minimum(i * n + t + 1, win).astype(F32)
    cur = jnp.where((t - s >= 0) & (t - s < win), 1.0 / cnt, 0.0)
    prev = jnp.where((t + n - s < win) & (i > 0), 1.0 / cnt, 0.0)
    return cur, prev


def _pool_fwd(u):
    TP, D = u.shape
    nb, grp = TP // ROW_BLOCK, D // len(POOL_WINDOWS)

    def body(cur_ref, prev_ref, out_ref):
        i = pl.program_id(0)
        for gi, win in enumerate(POOL_WINDOWS):
            sl = slice(gi * grp, (gi + 1) * grp)
            bc, bp = _pool_bands(i, win)
            cur = cur_ref[:, sl]
            out_ref[:, sl] = (_rhnn(bc, cur) + _rhnn(bp, prev_ref[:, sl]) - cur).astype(out_ref.dtype)

    return pl.pallas_call(
        body, name="pool_fwd", grid=(nb,),
        in_specs=[pl.BlockSpec((ROW_BLOCK, D), lambda i: (i, 0)),
                  pl.BlockSpec((ROW_BLOCK, D), lambda i: (jnp.maximum(i - 1, 0), 0))],
        out_specs=pl.BlockSpec((ROW_BLOCK, D), lambda i: (i, 0)),
        out_shape=jax.ShapeDtypeStruct((TP, D), BF16),
        compiler_params=_cp(("arbitrary",)),
    )(u, u)


def _pool_bwd(dp):
    TP, D = dp.shape
    nb, grp = TP // ROW_BLOCK, D // len(POOL_WINDOWS)

    def body(cur_ref, nxt_ref, out_ref):
        i = pl.program_id(0)
        for gi, win in enumerate(POOL_WINDOWS):
            sl = slice(gi * grp, (gi + 1) * grp)
            bc, _ = _pool_bands(i, win)
            _, bp = _pool_bands(i + 1, win)
            cur = cur_ref[:, sl]
            nxt = jnp.where(i < nb - 1, nxt_ref[:, sl], 0.0)
            out_ref[:, sl] = _rhtn(bc, cur) + _rhtn(bp, nxt) - cur

    return pl.pallas_call(
        body, name="pool_bwd", grid=(nb,),
        in_specs=[pl.BlockSpec((ROW_BLOCK, D), lambda i: (i, 0)),
                  pl.BlockSpec((ROW_BLOCK, D), lambda i: (jnp.minimum(i + 1, nb - 1), 0))],
        out_specs=pl.BlockSpec((ROW_BLOCK, D), lambda i: (i, 0)),
        out_shape=jax.ShapeDtypeStruct((TP, D), F32),
        compiler_params=_cp(("arbitrary",)),
    )(dp, dp)


def _rot_matrix(dh):
    s = _iota((dh, dh), 0)
    t = _iota((dh, dh), 1)
    return jnp.where(s == t + dh // 2, -1.0, 0.0) + jnp.where(s == t - dh // 2, 1.0, 0.0)


def _swa_block(i, t_real, q, k_prev, k_cur, v_prev, v_cur, cos_q, sin_q, cos_p, sin_p, sink):
    n = ROW_BLOCK
    dh = q.shape[1]
    g = q.shape[0] // n
    rot = _rot_matrix(dh)
    rope = lambda x, c, s: x * c + _hnn(x, rot) * s
    qr = rope(q, jnp.concatenate([cos_q] * g, axis=0), jnp.concatenate([sin_q] * g, axis=0))
    kb = jnp.concatenate([rope(k_prev, cos_p, sin_p), rope(k_cur, cos_q, sin_q)], axis=0)
    vb = jnp.concatenate([v_prev, v_cur], axis=0)
    s = _bnt(qr, kb) * (dh ** -0.5)
    qpos = i * n + (_iota((g * n, 2 * n), 0) % n)
    kpos = (i - 1) * n + _iota((g * n, 2 * n), 1)
    mask = (kpos <= qpos) & (qpos - kpos < SWA_WINDOW) & (kpos >= 0) & (kpos < t_real)
    s = jnp.where(mask, s, NEG)
    m = lax.stop_gradient(jnp.maximum(jnp.max(s, axis=1, keepdims=True), sink))
    e = jnp.where(mask, jnp.exp(s - m), 0.0)
    den = jnp.sum(e, axis=1, keepdims=True) + jnp.exp(sink - m)
    return _bnn(e / den, vb)


def _swa_core(q, k, v, cos, sin, sink, t_real, do=None):
    hkv, g, TP, dh = q.shape
    n = ROW_BLOCK
    nb = TP // n
    bwd = do is not None

    def body(*refs):
        q_ref, kp_ref, kc_ref, vp_ref, vc_ref, cq, sq, cpv, spv, sink_ref = refs[:10]
        i = pl.program_id(1)
        fn = lambda *a: _swa_block(i, t_real, *a)
        args = (q_ref[0].reshape(g * n, dh), kp_ref[0], kc_ref[0], vp_ref[0], vc_ref[0],
                cq[...], sq[...], cpv[...], spv[...], sink_ref[0])
        if not bwd:
            refs[10][0] = fn(*args).reshape(g, n, dh)
            return
        do_ref, dq_ref, dkp_ref, dkc_ref, dvp_ref, dvc_ref, dsink_ref = refs[10:17]
        _, vjp = jax.vjp(fn, *args)
        d = vjp(do_ref[0].reshape(g * n, dh))
        dq_ref[0] = d[0].reshape(g, n, dh)
        dkp_ref[0] = d[1]
        dkc_ref[0] = d[2]
        dvp_ref[0] = d[3]
        dvc_ref[0] = d[4]

        @pl.when(i == 0)
        def _():
            dsink_ref[...] = jnp.zeros_like(dsink_ref)
        dsink_ref[0] += d[9]

    qspec = pl.BlockSpec((1, g, n, dh), lambda h, i: (h, 0, i, 0))
    cur = pl.BlockSpec((1, n, dh), lambda h, i: (h, i, 0))
    prev = pl.BlockSpec((1, n, dh), lambda h, i: (h, jnp.maximum(i - 1, 0), 0))
    tcur = pl.BlockSpec((n, dh), lambda h, i: (i, 0))
    tprev = pl.BlockSpec((n, dh), lambda h, i: (jnp.maximum(i - 1, 0), 0))
    sspec = pl.BlockSpec((1, g * n, 1), lambda h, i: (h, 0, 0))
    in_specs = [qspec, prev, cur, prev, cur, tcur, tcur, tprev, tprev, sspec]
    ins = [q, k, k, v, v, cos, sin, cos, sin, sink]
    if not bwd:
        out_specs, out_shape = [qspec], [jax.ShapeDtypeStruct(q.shape, F32)]
    else:
        in_specs.append(qspec)
        ins.append(do)
        kv = jax.ShapeDtypeStruct(k.shape, F32)
        out_specs = [qspec, cur, cur, cur, cur, sspec]
        out_shape = [jax.ShapeDtypeStruct(q.shape, F32), kv, kv, kv, kv, jax.ShapeDtypeStruct(sink.shape, F32)]
    return pl.pallas_call(
        body, name="swa_core_bwd" if bwd else "swa_core_fwd", grid=(hkv, nb),
        in_specs=in_specs, out_specs=out_specs, out_shape=out_shape,
        compiler_params=_cp(("parallel", "arbitrary")),
    )(*ins)


def _coords():
    return lax.axis_index("x"), lax.axis_index("y"), lax.axis_index("c")


def _other_chips(x, y):
    return [(1 - x, y), (x, 1 - y), (1 - x, 1 - y)]


def _rcopy(src, dst, send, recv, dev):
    return pltpu.make_async_remote_copy(src_ref=src, dst_ref=dst, send_sem=send, recv_sem=recv,
                                        device_id=dev, device_id_type=MESH)


ANY = pl.BlockSpec(memory_space=pl.ANY)


def _allgather(name, shards, split):
    n = len(shards)

    def body(*refs):
        ins, outs = refs[:n], refs[n:2 * n]
        send, recv, loc = refs[2 * n:]
        x, y, c = _coords()
        s_me = 2 * x + y
        chips = _other_chips(x, y)
        started = []
        for i in range(n):
            lc = pltpu.make_async_copy(ins[i], outs[i].at[s_me], loc.at[i])
            lc.start()
            started.append(lc)
        if split:
            halves = [shards[i].shape[0] // 2 for i in range(n)]
            mine = [pl.ds(c * h, h) for h in halves]
            other = [pl.ds((1 - c) * h, h) for h in halves]
        sends = []
        for i in range(n):
            for j, (px, py) in enumerate(chips):
                if split:
                    cp = _rcopy(ins[i].at[mine[i]], outs[i].at[s_me, mine[i]], send.at[i, j], recv.at[i, j], (px, py, c))
                else:
                    cp = _rcopy(ins[i], outs[i].at[s_me], send.at[i, j], recv.at[i, j], (px, py, c))
                cp.start()
                sends.append(cp)
        for i in range(n):
            for j, (px, py) in enumerate(chips):
                s_j = 2 * px + py
                if split:
                    land = outs[i].at[s_j, mine[i]]
                    _rcopy(land, land, send.at[i, j], recv.at[i, j], (px, py, c)).wait_recv()
                    fw = _rcopy(land, land, send.at[i, 3 + j], recv.at[i, 3 + j], (x, y, 1 - c))
                    fw.start()
                    sends.append(fw)
                else:
                    land = outs[i].at[s_j]
                    _rcopy(land, land, send.at[i, j], recv.at[i, j], (px, py, c)).wait_recv()
        if split:
            for i in range(n):
                for j, (px, py) in enumerate(chips):
                    land = outs[i].at[2 * px + py, other[i]]
                    _rcopy(land, land, send.at[i, 3 + j], recv.at[i, 3 + j], (x, y, 1 - c)).wait_recv()
        for cp in sends:
            cp.wait_send()
        for lc in started:
            lc.wait()

    nsem = 6 if split else 3
    return pl.pallas_call(
        body, name=name,
        in_specs=[ANY] * n, out_specs=[ANY] * n,
        out_shape=[jax.ShapeDtypeStruct((N_CHIPS,) + s.shape, s.dtype) for s in shards],
        scratch_shapes=[pltpu.SemaphoreType.DMA((n, nsem)), pltpu.SemaphoreType.DMA((n, nsem)), pltpu.SemaphoreType.DMA((n,))],
        compiler_params=pltpu.CompilerParams(has_side_effects=True),
    )(*shards)


_HBM = pl.BlockSpec(memory_space=pltpu.HBM)
_SEM = pl.BlockSpec(memory_space=pltpu.SEMAPHORE)
_DATAFLOW = pltpu.SideEffectType.DATAFLOW_SIDE_EFFECTING


def _in_hbm(a):
    return pltpu.with_memory_space_constraint(a, pltpu.HBM)


def _ici_copy(gather, src, land, rows, c, s_from, s_to, send, recv, dev):
    if gather:
        mine = pl.ds(c * (rows // 2), rows // 2)
        return _rcopy(land.at[s_from, mine], land.at[s_from, mine], send, recv, dev)
    return _rcopy(src.at[s_to], land.at[s_from], send, recv, dev)


def _own_slot(name, src, lead, a, shape, where):
    _, r, c = src.shape
    tr = _row_tile(r, c, 4, 2)

    def body(x_ref, y_ref, s_ref, o_ref):
        o_ref[...] = s_ref[...].astype(o_ref.dtype)

    out = pl.pallas_call(
        body, name=name,
        grid_spec=pltpu.PrefetchScalarGridSpec(
            num_scalar_prefetch=2, grid=(a, r // tr),
            in_specs=[pl.BlockSpec((1, tr, c), lambda i, j, xr, yr: (lead + i, j, 0))],
            out_specs=pl.BlockSpec((1, tr, c), lambda i, j, xr, yr: ((2 * xr[0] + yr[0]) * a + i, j, 0))),
        out_shape=jax.ShapeDtypeStruct((N_CHIPS * a, r, c), BF16),
        compiler_params=_cp(("arbitrary", "arbitrary")),
    )(where[1], where[2], src)
    return out.reshape((N_CHIPS,) + tuple(shape))


def _ici_start(name, srcs, lands, after, gather):
    ns, n = len(srcs), len(lands)

    def body(*refs):
        s_refs, l_refs = refs[:ns], refs[ns:ns + n]
        send, recv = refs[ns + n + 1], refs[ns + n + 2]
        token = refs[-1]
        x, y, c = _coords()
        s_me = 2 * x + y
        for i in range(n):
            for j, (px, py) in enumerate(_other_chips(x, y)):
                _ici_copy(gather, None if gather else s_refs[i], l_refs[i], lands[i].shape[1], c, s_me, 2 * px + py,
                          send.at[3 * i + j], recv.at[3 * i + j], (px, py, c)).start()
        token[...] = jnp.zeros_like(token)

    thru = list(srcs) + list(lands)
    out_shape = [pltpu.SemaphoreType.DMA((3 * n,)), pltpu.SemaphoreType.DMA((3 * n,))]
    out_shape += [pltpu.HBM(a.shape, a.dtype) for a in thru]
    out_shape.append(jax.ShapeDtypeStruct((8, LANES), F32))
    res = pl.pallas_call(
        body, name=name, out_shape=out_shape,
        in_specs=[_HBM] * (ns + n) + [ANY],
        out_specs=[_SEM, _SEM] + [_HBM] * (ns + n) + [pl.BlockSpec(memory_space=pltpu.VMEM)],
        input_output_aliases={i: 2 + i for i in range(ns + n)},
        compiler_params=pltpu.CompilerParams(has_side_effects=_DATAFLOW),
    )(*[_in_hbm(a) for a in thru], after)
    return (res[0], res[1]), list(res[2:2 + ns]), list(res[2 + ns:2 + ns + n]), res[-1]


def _ici_wait(name, srcs, lands, send, recv, after, gather):
    ns, n = len(srcs), len(lands)

    def body(*refs):
        s_refs, l_refs = refs[:ns], refs[ns:ns + n]
        send_ref, recv_ref = refs[ns + n], refs[ns + n + 1]
        x, y, c = _coords()
        for i in range(n):
            for j, (px, py) in enumerate(_other_chips(x, y)):
                s_j = 2 * px + py
                cp = _ici_copy(gather, None if gather else s_refs[i], l_refs[i], lands[i].shape[1], c, s_j, s_j,
                               send_ref.at[3 * i + j], recv_ref.at[3 * i + j], (px, py, c))
                cp.wait_send()
                cp.wait_recv()

    thru = list(srcs) + list(lands)
    res = pl.pallas_call(
        body, name=name,
        out_shape=[pltpu.HBM(a.shape, a.dtype) for a in thru],
        in_specs=[_HBM] * (ns + n) + [_SEM, _SEM, ANY], out_specs=[_HBM] * (ns + n),
        input_output_aliases={i: i for i in range(ns + n)},
        compiler_params=pltpu.CompilerParams(has_side_effects=_DATAFLOW),
    )(*thru, send, recv, after)
    return list(res[:ns]), list(res[ns:])


def _ag_finish(name, lands):
    n = len(lands)

    def body(*refs):
        outs = refs[n:2 * n]
        send, recv = refs[2 * n:]
        x, y, c = _coords()
        chips = _other_chips(x, y)
        cps = []
        for i in range(n):
            half = lands[i].shape[1] // 2
            for j, (px, py) in enumerate(chips):
                land = outs[i].at[2 * px + py, pl.ds(c * half, half)]
                cp = _rcopy(land, land, send.at[i, j], recv.at[i, j], (x, y, 1 - c))
                cp.start()
                cps.append(cp)
        for i in range(n):
            half = lands[i].shape[1] // 2
            for j, (px, py) in enumerate(chips):
                land = outs[i].at[2 * px + py, pl.ds((1 - c) * half, half)]
                _rcopy(land, land, send.at[i, j], recv.at[i, j], (x, y, 1 - c)).wait_recv()
        for cp in cps:
            cp.wait_send()

    return pl.pallas_call(
        body, name=name, in_specs=[ANY] * n, out_specs=[ANY] * n,
        out_shape=[jax.ShapeDtypeStruct(a.shape, a.dtype) for a in lands],
        scratch_shapes=[pltpu.SemaphoreType.DMA((n, 3)), pltpu.SemaphoreType.DMA((n, 3))],
        input_output_aliases={i: i for i in range(n)},
        compiler_params=pltpu.CompilerParams(has_side_effects=True),
    )(*lands)


def _rs_pair_exchange(name, grads):
    n = len(grads)

    def body(*refs):
        ins, bufs = refs[:n], refs[n:2 * n]
        send, recv = refs[2 * n:]
        x, y, c = _coords()
        cps = []
        for i in range(n):
            half = grads[i].shape[1] // 2
            cp = _rcopy(ins[i].at[pl.ds(0, N_CHIPS), pl.ds((1 - c) * half, half)], bufs[i], send.at[i], recv.at[i], (x, y, 1 - c))
            cp.start()
            cps.append(cp)
        for cp in cps:
            cp.wait_recv()
        for cp in cps:
            cp.wait_send()

    return pl.pallas_call(
        body, name=name, in_specs=[ANY] * n, out_specs=[ANY] * n,
        out_shape=[jax.ShapeDtypeStruct((N_CHIPS, g.shape[1] // 2, g.shape[2]), g.dtype) for g in grads],
        scratch_shapes=[pltpu.SemaphoreType.DMA((n,)), pltpu.SemaphoreType.DMA((n,))],
        compiler_params=pltpu.CompilerParams(has_side_effects=True),
    )(*grads)


def _row_tile(rows, cols, itemsize, n_bufs):
    for tr in (2048, 1024, 512, 256, 128, 64, 32, 16, 8):
        if rows % tr == 0 and 2 * n_bufs * tr * cols * itemsize <= VMEM_BUDGET // 2:
            return tr
    return rows


def _rs_pair_sum(name, g, buf, core):
    _, R, C = g.shape
    half = R // 2
    tr = _row_tile(half, C, 4, 3)
    nhb = half // tr

    def body(c_ref, g_ref, b_ref, o_ref):
        o_ref[...] = (g_ref[...].astype(F32) + b_ref[...].astype(F32)).astype(o_ref.dtype)

    return pl.pallas_call(
        body, name=name,
        grid_spec=pltpu.PrefetchScalarGridSpec(
            num_scalar_prefetch=1, grid=(N_CHIPS, nhb),
            in_specs=[pl.BlockSpec((1, tr, C), lambda s, r, c_ref: (s, c_ref[0] * nhb + r, 0)),
                      pl.BlockSpec((1, tr, C), lambda s, r, c_ref: (s, r, 0))],
            out_specs=pl.BlockSpec((1, tr, C), lambda s, r, c_ref: (s, r, 0))),
        out_shape=jax.ShapeDtypeStruct((N_CHIPS, half, C), BF16),
        compiler_params=_cp(("arbitrary", "arbitrary")),
    )(core, g, buf)


def _rs_sum4(name, buf, part, where, out=None, row_base=0):
    _, half, C = buf.shape
    tr = _row_tile(half, C, 4, 4)
    nhb = half // tr
    base = row_base // tr

    def body(c_ref, x_ref, y_ref, b_ref, p_ref, *rest):
        o_ref = rest[-1]
        s_me = 2 * x_ref[0] + y_ref[0]
        acc = None
        for s in range(N_CHIPS):
            term = jnp.where(s_me == s, p_ref[0], b_ref[s]).astype(F32)
            acc = term if acc is None else acc + term
        o_ref[...] = acc

    in_specs = [pl.BlockSpec((N_CHIPS, tr, C), lambda r, cr, xr, yr: (0, r, 0)),
                pl.BlockSpec((1, tr, C), lambda r, cr, xr, yr: (2 * xr[0] + yr[0], r, 0))]
    args = [*where, buf, part]
    aliases = {}
    if out is not None:
        in_specs.append(ANY)
        args.append(out)
        aliases = {5: 0}
    return pl.pallas_call(
        body, name=name,
        grid_spec=pltpu.PrefetchScalarGridSpec(
            num_scalar_prefetch=3, grid=(nhb,), in_specs=in_specs,
            out_specs=pl.BlockSpec((tr, C), lambda r, cr, xr, yr: (base + cr[0] * nhb + r, 0))),
        out_shape=jax.ShapeDtypeStruct((2 * half, C) if out is None else out.shape, F32),
        input_output_aliases=aliases,
        compiler_params=_cp(("arbitrary",)),
    )(*args)


def _rs_share_halves(name, outs, spans):
    n = len(outs)

    def body(*refs):
        o = refs[n:2 * n]
        send, recv = refs[2 * n:]
        x, y, c = _coords()
        cps = []
        for i in range(n):
            base, half = spans[i][0], spans[i][1] // 2
            mine = o[i].at[pl.ds(base + c * half, half)]
            cp = _rcopy(mine, mine, send.at[i], recv.at[i], (x, y, 1 - c))
            cp.start()
            cps.append(cp)
        for i in range(n):
            base, half = spans[i][0], spans[i][1] // 2
            land = o[i].at[pl.ds(base + (1 - c) * half, half)]
            _rcopy(land, land, send.at[i], recv.at[i], (x, y, 1 - c)).wait_recv()
        for cp in cps:
            cp.wait_send()

    return pl.pallas_call(
        body, name=name, in_specs=[ANY] * n, out_specs=[ANY] * n,
        out_shape=[jax.ShapeDtypeStruct(a.shape, a.dtype) for a in outs],
        scratch_shapes=[pltpu.SemaphoreType.DMA((n,)), pltpu.SemaphoreType.DMA((n,))],
        input_output_aliases={i: i for i in range(n)},
        compiler_params=pltpu.CompilerParams(has_side_effects=True),
    )(*outs)


def _allreduce_small(name, v):
    rows = v.shape[0]

    def body(v_ref, o_ref, gath, send, recv):
        x, y, c = _coords()
        me = 4 * x + 2 * y + c

        def peer(kk):
            return (1 - x if kk & 4 else x, 1 - y if kk & 2 else y, 1 - c if kk & 1 else c)

        cps = []
        for kk in range(1, N_DEV):
            cp = _rcopy(v_ref, gath.at[me], send.at[kk - 1], recv.at[kk - 1], peer(kk))
            cp.start()
            cps.append(cp)
        gath[me] = v_ref[...]
        for kk in range(1, N_DEV):
            px, py, pc = peer(kk)
            land = gath.at[4 * px + 2 * py + pc]
            _rcopy(land, land, send.at[kk - 1], recv.at[kk - 1], (px, py, pc)).wait_recv()
        for cp in cps:
            cp.wait_send()
        acc = gath[0]
        for d in range(1, N_DEV):
            acc = acc + gath[d]
        o_ref[...] = acc

    return pl.pallas_call(
        body, name=name,
        in_specs=[pl.BlockSpec(memory_space=pltpu.VMEM)], out_specs=pl.BlockSpec(memory_space=pltpu.VMEM),
        out_shape=jax.ShapeDtypeStruct(v.shape, F32),
        scratch_shapes=[pltpu.VMEM((N_DEV, rows, LANES), F32), pltpu.SemaphoreType.DMA((N_DEV - 1,)), pltpu.SemaphoreType.DMA((N_DEV - 1,))],
        compiler_params=pltpu.CompilerParams(has_side_effects=True, vmem_limit_bytes=VMEM_LIMIT),
    )(v)


def _adamw(name, w, g, m, v):
    rows, cols = w.shape
    tr = _row_tile(rows, cols, 4, 7)

    def body(w_ref, g_ref, m_ref, v_ref, d_ref, nm_ref, nv_ref):
        gg = g_ref[...]
        nm = ADAM_B1 * m_ref[...] + (1.0 - ADAM_B1) * gg
        nv = ADAM_B2 * v_ref[...] + (1.0 - ADAM_B2) * (gg * gg)
        m_hat = nm / (1.0 - ADAM_B1 ** ADAM_STEP)
        v_hat = nv / (1.0 - ADAM_B2 ** ADAM_STEP)
        d_ref[...] = -ADAM_LR * (m_hat / (jnp.sqrt(v_hat) + ADAM_EPS) + ADAM_WD * w_ref[...])
        nm_ref[...] = nm
        nv_ref[...] = nv

    spec = pl.BlockSpec((tr, cols), lambda i: (i, 0))
    shp = jax.ShapeDtypeStruct((rows, cols), F32)
    return pl.pallas_call(
        body, name=name, grid=(rows // tr,), in_specs=[spec] * 4, out_specs=[spec] * 3, out_shape=[shp] * 3,
        compiler_params=_cp(("parallel",)),
    )(w, g, m, v)


def _as2d(a):
    if a.ndim == 1:
        return a.reshape(1, a.shape[0])
    return a.reshape(-1, a.shape[-1])


_WEIGHTS = ['meta_tokens', 'norm_w', 'ffn_w_gate', 'ffn_w_up', 'ffn_w_down', 'mlstm_w_in', 'mlstm_b_if', 'mlstm_norm_w',
            'mlstm_w_out', 'pool_w', 'pool_scale', 'gdn_w_in', 'gdn_conv_w', 'gdn_a_log', 'gdn_dt_bias', 'gdn_norm_w',
            'gdn_w_out', 'swa_w_qkv', 'swa_b_qkv', 'swa_sinks', 'swa_w_out', 'swa_b_out', 'final_norm_w']
_SMALL = [('meta_tokens', True), ('norm_w', True), ('pool_scale', True), ('gdn_conv_w', True), ('swa_b_qkv', True),
          ('swa_b_out', True), ('mlstm_b_if', False), ('mlstm_norm_w', False), ('gdn_a_log', False),
          ('gdn_dt_bias', False), ('gdn_norm_w', False), ('swa_sinks', False), ('final_norm_w', False)]


def _pack(vals):
    flat = jnp.concatenate([v.reshape(-1).astype(F32) for v in vals])
    n = _round_up(flat.shape[0], 8 * LANES)
    return jnp.pad(flat, (0, n - flat.shape[0])).reshape(n // LANES, LANES)


def _unpack(packed, shapes):
    flat = packed.reshape(-1)
    out, off = [], 0
    for s in shapes:
        n = int(np.prod(s))
        out.append(flat[off:off + n].reshape(s))
        off += n
    return out


def _to_chunks(a, n_meta, seq):
    pad = jnp.zeros((CHUNK - n_meta,) + a.shape[1:], a.dtype)
    return jnp.concatenate([a[:n_meta], pad, a[n_meta:n_meta + seq]], axis=0)


def _from_chunks(a, n_meta, seq, tp):
    pad = jnp.zeros((tp - n_meta - seq,) + a.shape[1:], a.dtype)
    return jnp.concatenate([a[:n_meta], a[CHUNK:CHUNK + seq], pad], axis=0)


def _col_row(g, heads):
    t = g.T
    return t[:, :, None], t.reshape(heads, -1, 1, CHUNK)


def _from_col_row(dc, dr):
    heads = dc.shape[0]
    return (dc[:, :, 0] + dr.reshape(heads, -1)).T


def kernel(x, meta_tokens, norm_w, ffn_w_gate, ffn_w_up, ffn_w_down, mlstm_w_in, mlstm_b_if, mlstm_norm_w, mlstm_w_out, pool_w, pool_scale, gdn_w_in, gdn_conv_w, gdn_a_log, gdn_dt_bias, gdn_norm_w, gdn_w_out, swa_w_qkv, swa_b_qkv, swa_sinks, swa_w_out, swa_b_out, final_norm_w, loss_target, m_meta_tokens, m_norm_w, m_ffn_w_gate, m_ffn_w_up, m_ffn_w_down, m_mlstm_w_in, m_mlstm_b_if, m_mlstm_norm_w, m_mlstm_w_out, m_pool_w, m_pool_scale, m_gdn_w_in, m_gdn_conv_w, m_gdn_a_log, m_gdn_dt_bias, m_gdn_norm_w, m_gdn_w_out, m_swa_w_qkv, m_swa_b_qkv, m_swa_sinks, m_swa_w_out, m_swa_b_out, m_final_norm_w, v_meta_tokens, v_norm_w, v_ffn_w_gate, v_ffn_w_up, v_ffn_w_down, v_mlstm_w_in, v_mlstm_b_if, v_mlstm_norm_w, v_mlstm_w_out, v_pool_w, v_pool_scale, v_gdn_w_in, v_gdn_conv_w, v_gdn_a_log, v_gdn_dt_bias, v_gdn_norm_w, v_gdn_w_out, v_swa_w_qkv, v_swa_b_qkv, v_swa_sinks, v_swa_w_out, v_swa_b_out, v_final_norm_w):
    args = locals()
    W = {n: args[n] for n in _WEIGHTS}
    M1 = {n: args["m_" + n] for n in _WEIGHTS}
    V2 = {n: args["v_" + n] for n in _WEIGHTS}

    SEQ, D = x.shape[1], x.shape[2]
    NM = meta_tokens.shape[0]
    T = NM + SEQ
    TP = _round_up(T, ROW_BLOCK)
    DEPTH = ffn_w_gate.shape[0]
    FFS = ffn_w_gate.shape[3]
    ML_H = mlstm_b_if.shape[1] // 2
    ML_DV = D // ML_H
    ML_DK = ML_DV // 2
    ML_IN = 2 * ML_H * ML_DK + 2 * D + 2 * ML_H
    ML_INP = _pad_cols(ML_IN)
    GD_DK = gdn_norm_w.shape[1]
    GD_VH = gdn_a_log.shape[1]
    GD_QH = GD_VH // 2
    GD_QKW = GD_QH * GD_DK
    GD_VW = GD_VH * GD_DK
    GD_CC = 2 * GD_QKW + GD_VW
    GD_IN = GD_CC + GD_VW + 2 * GD_VH
    GD_INP = _pad_cols(GD_IN)
    SW_HQ = swa_sinks.shape[1]
    SW_DH = D // SW_HQ
    SW_HKV = SW_HQ // SWA_GROUP
    SW_KVW = SW_HKV * SW_DH
    SW_IN = D + 2 * SW_KVW
    n_pool = len(POOL_WINDOWS)
    PG = D // n_pool

    cx, cy, cc = _coords()
    s_me = 2 * cx + cy
    core = cc.astype(jnp.int32).reshape(1)
    where = (core, cx.astype(jnp.int32).reshape(1), cy.astype(jnp.int32).reshape(1))

    def my_cols(full, width):
        return lax.dynamic_slice_in_dim(full, s_me * width, width, axis=full.ndim - 1)

    big_names = ['ffn_w_gate', 'ffn_w_up', 'ffn_w_down', 'mlstm_w_in', 'mlstm_w_out', 'pool_w', 'gdn_w_in', 'gdn_w_out',
                 'swa_w_qkv', 'swa_w_out']
    as3 = lambda w: w.reshape((-1,) + w.shape[-2:])
    mixer_shards = [[mlstm_w_in, mlstm_w_out], [pool_w.reshape(1, n_pool * (PG // N_CHIPS), PG)],
                    [gdn_w_in, gdn_w_out], [swa_w_qkv, swa_w_out]]
    ffn3 = [as3(ffn_w_gate), as3(ffn_w_up), as3(ffn_w_down)]
    ag_groups = [[(w, 2 * li, 2, (2,) + w.shape[1:]) for w in ffn3] + [(m, 0, 1, m.shape[1:]) for m in mixer_shards[li % 4]]
                 for li in range(DEPTH)]
    ag_state = {}
    WF = {}

    def cols_full(g, pad_to=None):
        k = g.shape[1]
        full = jnp.transpose(g, (1, 0, 2)).reshape(k, -1)
        if pad_to is not None and pad_to > full.shape[1]:
            full = jnp.pad(full, ((0, 0), (0, pad_to - full.shape[1])))
        return full

    def rows_full(g):
        return g.reshape(-1, g.shape[2])

    def start_gather(li, after):
        lands = [_own_slot(f"ag_own_{li}_{i}", w, lead, cnt, shp, where) for i, (w, lead, cnt, shp) in enumerate(ag_groups[li])]
        sems, _, lands, token = _ici_start(f"ag_start_{li}", [], lands, after, True)
        ag_state[li] = (sems, lands)
        return token

    def gather_layer(li, after):
        sems, lands = ag_state.pop(li)
        _, lands = _ici_wait(f"ag_wait_{li}", [], lands, *sems, after, True)
        full = _ag_finish(f"ag_finish_{li}", lands)
        WF['wg', li], WF['wu', li], WF['wd', li] = full[0], full[1], full[2]
        m = full[3:]
        if li % 4 == 0:
            WF['ml_win'], WF['ml_wout'] = cols_full(m[0], ML_INP), rows_full(m[1])
        elif li % 4 == 1:
            WF['pool'] = jnp.transpose(m[0].reshape(N_CHIPS, n_pool, PG // N_CHIPS, PG), (1, 0, 2, 3)).reshape(n_pool, PG, PG)
        elif li % 4 == 2:
            WF['gd_win'], WF['gd_wout'] = cols_full(m[0], GD_INP), rows_full(m[1])
        else:
            WF['sw_wqkv'], WF['sw_wout'] = cols_full(m[0]), rows_full(m[1])

    small_sharded = [n for n, sh in _SMALL if sh]
    sm_shapes = [W[n].shape for n in small_sharded]
    sm_gath = _allgather("allgather_small", [_pack([W[n] for n in small_sharded])], False)[0]
    sm_parts = [_unpack(sm_gath[s], sm_shapes) for s in range(N_CHIPS)]
    SF = {n: jnp.concatenate([sm_parts[s][i] for s in range(N_CHIPS)], axis=-1) for i, n in enumerate(small_sharded)}
    meta_full, normw_full = SF['meta_tokens'], SF['norm_w']
    pool_scale_full, conv_full = SF['pool_scale'], SF['gdn_conv_w'][0]
    bqkv_full, bout_full = SF['swa_b_qkv'], SF['swa_b_out']

    nps = 1
    ff_tm = _tiles(TP, FFS, D, 2 * 4 + 3 * 2 * 2, n_pairs=2)[0]

    def ffn_fwd(h, li, wi, nw):
        n = _rmsnorm_fwd(f"ffn_norm_{li}_{wi}", h, nw, BF16)
        tm, tn, tk = ff_tm, FFS, _div(D, (512, 256, 128))
        bspec = lambda j, k: (j, wi, k, 0)
        g, u, a = _matmul(
            f"ffn_gateup_{li}_{wi}", "nn", (TP // tm, N_CHIPS, D // tk),
            [(n, (tm, tk), lambda i, j, k: (i, k))] * 2,
            [(WF['wg', li], (None, None, tk, tn), lambda i, j, k: bspec(j, k)),
             (WF['wu', li], (None, None, tk, tn), lambda i, j, k: bspec(j, k))],
            [0, 1], 2, [],
            [(jax.ShapeDtypeStruct((TP, N_CHIPS * FFS), BF16), (tm, tn), lambda i, j, k: (i, j))] * 3,
            lambda accs, ex: [accs[0], accs[1], _silu(accs[0]) * accs[1]], tm, tn)
        tm2, tn2, tk2 = _tiles(TP, D, FFS, 4 + 2 * 4 + 2 * 4, k_cands=(FFS,), n_cands=(1024, 512, 256, 128))
        kps = FFS // tk2
        h2 = _matmul(
            f"ffn_down_{li}_{wi}", "nn", (TP // tm2, D // tn2, N_CHIPS * kps),
            [(a, (tm2, tk2), lambda i, j, k: (i, k))],
            [(WF['wd', li], (None, None, tk2, tn2), lambda i, j, k: (k // kps, wi, k % kps, j))],
            [0], 1, [(h, (tm2, tn2), lambda i, j, k: (i, j))],
            [(jax.ShapeDtypeStruct((TP, D), F32), (tm2, tn2), lambda i, j, k: (i, j))],
            lambda accs, ex: [ex[0] + 0.5 * accs[0]], tm2, tn2)[0]
        return h2, (h, n, g, u, a)

    def ffn_bwd(dh2, saved, li, wi, nw, gbufs):
        h, n, g, u, a = saved
        gg, gu, gd = gbufs
        slot = wi
        tm, tn = ff_tm, FFS
        tk = _div(D, (512, 256, 128))

        def epi(accs, ex):
            gb, ub = ex[0].astype(F32), ex[1].astype(F32)
            da = 0.5 * accs[0]
            s = _sigmoid(gb)
            return [da * ub * (s * (1.0 + gb * (1.0 - s))), da * (gb * s)]
        dg, du = _matmul(
            f"ffn_dact_{li}_{wi}", "nt", (TP // tm, N_CHIPS, D // tk),
            [(dh2, (tm, tk), lambda i, j, k: (i, k))],
            [(WF['wd', li], (None, None, tn, tk), lambda i, j, k: (j, wi, 0, k))],
            [0], 1, [(g, (tm, tn), lambda i, j, k: (i, j)), (u, (tm, tn), lambda i, j, k: (i, j))],
            [(jax.ShapeDtypeStruct((TP, N_CHIPS * FFS), BF16), (tm, tn), lambda i, j, k: (i, j))] * 2, epi, tm, tn)
        tkr = _div(TP, (1408, 1056, 704, 384, 256, 128))
        tnd = _div(D, (1024, 512, 256, 128))
        gd = _matmul(
            f"ffn_dwd_{li}_{wi}", "tn", (N_CHIPS, D // tnd, TP // tkr),
            [(a, (tkr, FFS), lambda i, j, k: (k, i))], [(dh2, (tkr, tnd), lambda i, j, k: (k, j))],
            [0], 1, [], [(jax.ShapeDtypeStruct(gd.shape, BF16), (None, None, FFS, tnd), lambda i, j, k: (i, slot, 0, j))],
            lambda accs, ex: [0.5 * accs[0]], FFS, tnd, alias_inputs=[gd], alias_map={0: 0})[0]
        tmw = _div(D, (512, 256, 128))
        gg, gu = _matmul(
            f"ffn_dwgu_{li}_{wi}", "tn", (D // tmw, N_CHIPS, TP // tkr),
            [(n, (tkr, tmw), lambda i, j, k: (k, i))] * 2,
            [(dg, (tkr, FFS), lambda i, j, k: (k, j)), (du, (tkr, FFS), lambda i, j, k: (k, j))],
            [0, 1], 2, [],
            [(jax.ShapeDtypeStruct(gg.shape, BF16), (None, None, tmw, FFS), lambda i, j, k: (j, slot, i, 0))] * 2,
            lambda accs, ex: [accs[0], accs[1]], tmw, FFS, alias_inputs=[gg, gu], alias_map={0: 0, 1: 1})
        tm3 = _div(TP, (704, 528, 384, 256, 128))
        tn3 = _div(D, (1024, 512, 256, 128))
        dn = _matmul(
            f"ffn_dn_{li}_{wi}", "nt", (TP // tm3, D // tn3, N_CHIPS),
            [(dg, (tm3, FFS), lambda i, j, k: (i, k)), (du, (tm3, FFS), lambda i, j, k: (i, k))],
            [(WF['wg', li], (None, None, tn3, FFS), lambda i, j, k: (k, wi, j, 0)),
             (WF['wu', li], (None, None, tn3, FFS), lambda i, j, k: (k, wi, j, 0))],
            [0, 0], 1, [], [(jax.ShapeDtypeStruct((TP, D), F32), (tm3, tn3), lambda i, j, k: (i, j))],
            lambda accs, ex: [accs[0]], tm3, tn3)[0]
        dh, dnw = _rmsnorm_bwd(f"ffn_norm_bwd_{li}_{wi}", h, nw, dn, dh2)
        return dh, dnw, (gg, gu, gd)

    def mlstm_fwd(h, nw):
        u = _rmsnorm_fwd("mlstm_norm", h, nw, BF16)
        p = _mm("mlstm_in", "nn", u, WF['ml_win'], F32)
        pc = _to_chunks(p, NM, SEQ)
        qkw = ML_H * ML_DK
        gates = pc[:, 2 * qkw + 2 * D:2 * qkw + 2 * D + 2 * ML_H] + mlstm_b_if
        li_c, li_r = _col_row(gates[:, :ML_H], ML_H)
        lf_c, lf_r = _col_row(gates[:, ML_H:], ML_H)
        hh, cs, ns, ms = _mlstm_core_fwd(pc, li_c, lf_c, li_r, lf_r, ML_H, ML_DK, ML_DV, NM)
        hh_s = _from_chunks(hh, NM, SEQ, TP)
        og_cb = (2 * qkw + D) // D
        out = _headnorm_fwd("mlstm_post", hh_s, p, D, og_cb, mlstm_norm_w, ML_DV, _sigmoid)
        h2 = _mm("mlstm_out", "nn", out, WF['ml_wout'], F32, lambda acc, hb: hb + acc, [h], ["tile"])
        return h2, (h, u, p, pc, (li_c, lf_c, li_r, lf_r), (cs, ns, ms), hh_s, out, og_cb)

    def mlstm_bwd(dh2, saved, nw):
        h, u, p, pc, gts, sts, hh_s, out, og_cb = saved
        dout = _mm("mlstm_out_dx", "nt", dh2, WF['ml_wout'], F32)
        d_wout = _mm("mlstm_out_dw", "tn", out, dh2, BF16)
        dhh, dog, dnormw = _headnorm_bwd("mlstm_post_bwd", hh_s, p, D, og_cb, mlstm_norm_w, dout, ML_DV, _sigmoid)
        dq, dkk, dvv, d0, d1, d2, d3 = _mlstm_core_bwd(pc, *gts, *sts, _to_chunks(dhh, NM, SEQ), ML_H, ML_DK, ML_DV, NM)
        dgates = jnp.concatenate([_from_col_row(d0, d2), _from_col_row(d1, d3)], axis=1)
        dqkvg = _from_chunks(jnp.concatenate([dq, dkk, dvv], axis=1), NM, SEQ, TP)
        dgs = _from_chunks(dgates, NM, SEQ, TP)
        pad = jnp.zeros((TP, ML_INP - ML_IN), F32)
        dp = jnp.concatenate([dqkvg, dog, dgs, pad], axis=1)
        d_bif = _colsum("mlstm_dbias", jnp.pad(dgs, ((0, 0), (0, LANES - 2 * ML_H))))[:, :2 * ML_H]
        d_win = _mm("mlstm_in_dw", "tn", u, dp, BF16)[:, :ML_IN]
        du = _mm("mlstm_in_dx", "nt", dp, WF['ml_win'], F32)
        dh, dnw = _rmsnorm_bwd("mlstm_norm_bwd", h, nw, du, dh2)
        return dh, dnw, {'mlstm_w_in': d_win, 'mlstm_w_out': d_wout, 'mlstm_b_if': d_bif, 'mlstm_norm_w': dnormw}

    def pool_fwd_layer(h, nw):
        u = _rmsnorm_fwd("pool_norm", h, nw, F32)
        pooled = _pool_fwd(u)
        tm = _div(TP, (704, 528, 384, 256, 128))
        tk = _div(PG, (512, 256, 128))
        kpg = PG // tk
        h2, ypre = _matmul(
            "pool_mix", "nn", (TP // tm, n_pool, kpg),
            [(pooled, (tm, tk), lambda i, j, k: (i, j * kpg + k))],
            [(WF['pool'],(None, tk, PG), lambda i, j, k: (j, k, 0))],
            [0], 1, [(h, (tm, PG), lambda i, j, k: (i, j)), (pool_scale_full, (1, PG), lambda i, j, k: (0, j))],
            [(jax.ShapeDtypeStruct((TP, D), F32), (tm, PG), lambda i, j, k: (i, j))] * 2,
            lambda accs, ex: [ex[0] + accs[0] * ex[1], accs[0]], tm, PG)
        return h2, (h, pooled, ypre)

    def pool_bwd_layer(dh2, saved, nw):
        h, pooled, ypre = saved

        def fn(_, dyb, ypb, sb):
            return [dyb * sb, jnp.sum(dyb * ypb, axis=0, keepdims=True)]
        dys, dscale = _rowwise("pool_scale_bwd", fn, [_full(dh2), _full(ypre), _full(pool_scale_full)],
                               [(D, BF16, "row"), (D, F32, "acc")], ROW_BLOCK, TP)
        tm = _div(TP, (704, 528, 384, 256, 128))
        tk = _div(PG, (512, 256, 128))
        kpg = PG // tk
        dpooled = _matmul(
            "pool_mix_dx", "nt", (TP // tm, n_pool, kpg),
            [(dys, (tm, tk), lambda i, j, k: (i, j * kpg + k))],
            [(WF['pool'],(None, PG, tk), lambda i, j, k: (j, 0, k))],
            [0], 1, [], [(jax.ShapeDtypeStruct((TP, D), F32), (tm, PG), lambda i, j, k: (i, j))],
            lambda accs, ex: [accs[0]], tm, PG)[0]
        tkr = _div(TP, (1408, 1056, 704, 384, 256, 128))
        d_pw = _matmul(
            "pool_mix_dw", "tn", (1, n_pool, TP // tkr),
            [(pooled, (tkr, PG), lambda i, j, k: (k, j))], [(dys, (tkr, PG), lambda i, j, k: (k, j))],
            [0], 1, [], [(jax.ShapeDtypeStruct((n_pool, PG, PG), BF16), (None, PG, PG), lambda i, j, k: (j, 0, 0))],
            lambda accs, ex: [accs[0]], PG, PG)[0]
        du = _pool_bwd(dpooled)
        dh, dnw = _rmsnorm_bwd("pool_norm_bwd", h, nw, du, dh2)
        return dh, dnw, {'pool_w': d_pw, 'pool_scale': dscale}

    def gdn_fwd(h, nw):
        u = _rmsnorm_fwd("gdn_norm", h, nw, BF16)
        p = _mm("gdn_in", "nn", u, WF['gd_win'], F32)
        qkv_act = _conv_fwd(p, conv_full, GD_CC)
        b_pre = p[:, GD_CC + GD_VW:GD_CC + GD_VW + GD_VH]
        a_pre = p[:, GD_CC + GD_VW + GD_VH:GD_IN]
        g, beta = _gdn_gates_fwd(a_pre, b_pre, gdn_a_log, gdn_dt_bias)
        qkv_c = _to_chunks(qkv_act, NM, SEQ)
        g_c, g_r = _col_row(_to_chunks(g, NM, SEQ), GD_VH)
        b_c, _ = _col_row(_to_chunks(beta, NM, SEQ), GD_VH)
        o, st = _gdn_core_fwd(qkv_c, g_c, b_c, g_r, GD_QH, GD_DK, NM)
        o_s = _from_chunks(o, NM, SEQ, TP)
        nw_t = jnp.tile(gdn_norm_w, (1, GD_VH))
        z_cb = GD_CC // GD_VW
        out = _headnorm_fwd("gdn_post", o_s, p, GD_VW, z_cb, nw_t, GD_DK, _silu)
        h2 = _mm("gdn_out", "nn", out, WF['gd_wout'], F32, lambda acc, hb: hb + acc, [h], ["tile"])
        return h2, (h, u, p, qkv_c, (g_c, b_c, g_r), st, o_s, out, nw_t, z_cb, a_pre, b_pre)

    def gdn_bwd(dh2, saved, nw):
        h, u, p, qkv_c, gts, st, o_s, out, nw_t, z_cb, a_pre, b_pre = saved
        dout = _mm("gdn_out_dx", "nt", dh2, WF['gd_wout'], F32)
        d_wout = _mm("gdn_out_dw", "tn", out, dh2, BF16)
        do, dz, dnw_t = _headnorm_bwd("gdn_post_bwd", o_s, p, GD_VW, z_cb, nw_t, dout, GD_DK, _silu)
        dnormw = jnp.sum(dnw_t.reshape(GD_VH, GD_DK), axis=0, keepdims=True)
        res = _gdn_core_bwd(qkv_c, *gts, st, _to_chunks(do, NM, SEQ), GD_QH, GD_DK, NM)
        dq, dkk, dvv = res[0], res[1], res[2]
        dgc, dbc, dgr = res[3], res[4], res[5]
        dg = _from_chunks(_from_col_row(dgc, dgr), NM, SEQ, TP)
        dbeta = _from_chunks(dbc[:, :, 0].T, NM, SEQ, TP)
        da_pre, db_pre, d_alog, d_dt = _gdn_gates_bwd(a_pre, b_pre, gdn_a_log, gdn_dt_bias, dg, dbeta)
        dact = _from_chunks(jnp.concatenate([dq, dkk, dvv], axis=1), NM, SEQ, TP)
        dacc, d_conv = _conv_bwd_pre(p, conv_full, dact, GD_CC)
        dqkv_pre = _conv_bwd_dx(dacc, conv_full, GD_CC)
        pad = jnp.zeros((TP, GD_INP - GD_IN), F32)
        dp = jnp.concatenate([dqkv_pre, dz, db_pre, da_pre, pad], axis=1)
        d_win = _mm("gdn_in_dw", "tn", u, dp, BF16)[:, :GD_IN]
        du = _mm("gdn_in_dx", "nt", dp, WF['gd_win'], F32)
        dh, dnw = _rmsnorm_bwd("gdn_norm_bwd", h, nw, du, dh2)
        return dh, dnw, {'gdn_w_in': d_win, 'gdn_w_out': d_wout, 'gdn_conv_w': d_conv, 'gdn_a_log': d_alog,
                         'gdn_dt_bias': d_dt, 'gdn_norm_w': dnormw}

    inv = ROPE_THETA ** (-jnp.arange(0, SW_DH, 2, dtype=F32) / SW_DH)
    ang = jnp.arange(TP, dtype=F32)[:, None] * inv[None, :]
    ang = jnp.concatenate([ang, ang], axis=-1)
    rope_cos, rope_sin = jnp.cos(ang), jnp.sin(ang)

    def swa_split(p):
        q = jnp.transpose(p[:, :D].reshape(TP, SW_HKV, SWA_GROUP, SW_DH), (1, 2, 0, 3))
        k = jnp.transpose(p[:, D:D + SW_KVW].reshape(TP, SW_HKV, SW_DH), (1, 0, 2))
        v = jnp.transpose(p[:, D + SW_KVW:].reshape(TP, SW_HKV, SW_DH), (1, 0, 2))
        return q, k, v

    def swa_fwd(h, nw):
        u = _rmsnorm_fwd("swa_norm", h, nw, BF16)
        p = _mm("swa_in", "nn", u, WF['sw_wqkv'], F32, lambda acc, bb: acc + bb, [bqkv_full], ["row"])
        q, k, v = swa_split(p)
        sink = jnp.repeat(swa_sinks.reshape(SW_HKV, SWA_GROUP), ROW_BLOCK, axis=1)[:, :, None]
        o = _swa_core(q, k, v, rope_cos, rope_sin, sink, T)[0]
        o2 = jnp.transpose(o, (2, 0, 1, 3)).reshape(TP, D).astype(BF16)
        h2 = _mm("swa_out", "nn", o2, WF['sw_wout'], F32, lambda acc, hb, bb: hb + acc + bb, [h, bout_full], ["tile", "row"])
        return h2, (h, u, q, k, v, sink, o2)

    def swa_bwd(dh2, saved, nw):
        h, u, q, k, v, sink, o2 = saved
        do = _mm("swa_out_dx", "nt", dh2, WF['sw_wout'], F32)
        d_wout = _mm("swa_out_dw", "tn", o2, dh2, BF16)
        d_bout = _colsum("swa_dbout", dh2)
        do4 = jnp.transpose(do.reshape(TP, SW_HKV, SWA_GROUP, SW_DH), (1, 2, 0, 3))
        dq, dkp, dkc, dvp, dvc, dsink = _swa_core(q, k, v, rope_cos, rope_sin, sink, T, do=do4)
        shift = lambda a: jnp.concatenate([a[:, ROW_BLOCK:], jnp.zeros_like(a[:, :ROW_BLOCK])], axis=1)
        dk = dkc + shift(dkp)
        dv = dvc + shift(dvp)
        dp = jnp.concatenate([jnp.transpose(dq, (2, 0, 1, 3)).reshape(TP, D),
                              jnp.transpose(dk, (1, 0, 2)).reshape(TP, SW_KVW),
                              jnp.transpose(dv, (1, 0, 2)).reshape(TP, SW_KVW)], axis=1)
        d_sinks = jnp.sum(dsink.reshape(SW_HKV, SWA_GROUP, ROW_BLOCK), axis=2).reshape(1, SW_HQ)
        d_bqkv = _colsum("swa_dbqkv", dp)
        d_wqkv = _mm("swa_in_dw", "tn", u, dp, BF16)
        du = _mm("swa_in_dx", "nt", dp, WF['sw_wqkv'], F32)
        dh, dnw = _rmsnorm_bwd("swa_norm_bwd", h, nw, du, dh2)
        return dh, dnw, {'swa_w_qkv': d_wqkv, 'swa_w_out': d_wout, 'swa_b_qkv': d_bqkv, 'swa_b_out': d_bout,
                         'swa_sinks': d_sinks}

    mixers_fwd = [mlstm_fwd, pool_fwd_layer, gdn_fwd, swa_fwd]
    mixers_bwd = [mlstm_bwd, pool_bwd_layer, gdn_bwd, swa_bwd]

    h = jnp.concatenate([meta_full, x[0], jnp.zeros((TP - T, D), F32)], axis=0)
    saved = []
    token = start_gather(0, sm_gath)
    for li in range(DEPTH):
        nws = [normw_full[li, t].reshape(1, D) for t in range(3)]
        gather_layer(li, token if li == 0 else h)
        if li + 1 < DEPTH:
            nws[0] = nws[0] + start_gather(li + 1, WF['wg', li])[:1, :1]
        h, s0 = ffn_fwd(h, li, 0, nws[0])
        h, s1 = mixers_fwd[li % 4](h, nws[1])
        h, s2 = ffn_fwd(h, li, 1, nws[2])
        saved.append((s0, s1, s2, nws))

    tgt = jnp.concatenate([jnp.zeros((NM, D), F32), loss_target[0], jnp.zeros((TP - T, D), F32)], axis=0)
    dh, d_final_w, loss_vec = _loss_head("loss_head", h, final_norm_w.reshape(1, D), tgt, NM, SEQ)

    def col_shards(g, w):
        return jnp.transpose(g.reshape(g.shape[0], N_CHIPS, w), (1, 0, 2))

    def row_shards(g):
        return g.reshape(N_CHIPS, -1, g.shape[1])

    def mixer_grads(li, gm):
        if li % 4 == 0:
            return ['mlstm_w_in', 'mlstm_w_out'], [col_shards(gm['mlstm_w_in'], ML_IN // N_CHIPS), row_shards(gm['mlstm_w_out'])]
        if li % 4 == 1:
            pw = jnp.transpose(gm['pool_w'].reshape(n_pool, N_CHIPS, PG // N_CHIPS, PG), (1, 0, 2, 3))
            return ['pool_w'], [pw.reshape(N_CHIPS, -1, PG)]
        if li % 4 == 2:
            return ['gdn_w_in', 'gdn_w_out'], [col_shards(gm['gdn_w_in'], GD_IN // N_CHIPS), row_shards(gm['gdn_w_out'])]
        return ['swa_w_qkv', 'swa_w_out'], [col_shards(gm['swa_w_qkv'], SW_IN // N_CHIPS), row_shards(gm['swa_w_out'])]

    ffn_rows = (2 * D, 2 * D, 2 * FFS)
    ffn_out = [lax.empty((DEPTH * r, c_), F32) for r, c_ in zip(ffn_rows, (FFS, FFS, D))]
    grads = {}

    def rs_begin(li, names, glist):
        bufs = _rs_pair_exchange(f"rs_pair_{li}", glist)
        parts = [_rs_pair_sum(f"rs_pairsum_{li}_{i}", g, b, core) for i, (g, b) in enumerate(zip(glist, bufs))]
        sems, parts, lands, token = _ici_start(f"rs_start_{li}", parts, [lax.empty(p.shape, p.dtype) for p in parts], core, False)
        return (li, names, sems, parts, lands), token

    def rs_finish(state, after):
        li, names, sems, parts, lands = state
        parts, lands = _ici_wait(f"rs_wait_{li}", parts, lands, *sems, after, False)
        outs, spans = [], []
        for i, (p, b) in enumerate(zip(parts, lands)):
            if i < 3:
                outs.append(_rs_sum4(f"rs_sum4_{li}_{i}", b, p, where, out=ffn_out[i], row_base=li * ffn_rows[i]))
                spans.append((li * ffn_rows[i], ffn_rows[i]))
            else:
                outs.append(_rs_sum4(f"rs_sum4_{li}_{i}", b, p, where))
                spans.append((0, outs[-1].shape[0]))
        outs = _rs_share_halves(f"rs_share_{li}", outs, spans)
        ffn_out[:] = outs[:3]
        for n, o in zip(names, outs[3:]):
            grads[n] = o.reshape(W[n].shape)

    d_normw = [[None] * 3 for _ in range(DEPTH)]
    GR = {}
    pending = None
    for li in reversed(range(DEPTH)):
        s0, s1, s2, nws = saved[li]
        gbufs = (lax.empty((N_CHIPS, 2, D, FFS), BF16), lax.empty((N_CHIPS, 2, D, FFS), BF16),
                 lax.empty((N_CHIPS, 2, FFS, D), BF16))
        dh, d_normw[li][2], gbufs = ffn_bwd(dh, s2, li, 1, nws[2], gbufs)
        dh, d_normw[li][1], gm = mixers_bwd[li % 4](dh, s1, nws[1])
        GR.update(gm)
        dh, d_normw[li][0], gbufs = ffn_bwd(dh, s0, li, 0, nws[0], gbufs)
        if pending is not None:
            rs_finish(pending, dh)
        names, mg = mixer_grads(li, gm)
        glist = [gbufs[0].reshape(N_CHIPS, 2 * D, FFS), gbufs[1].reshape(N_CHIPS, 2 * D, FFS),
                 gbufs[2].reshape(N_CHIPS, 2 * FFS, D)] + mg
        pending, token = rs_begin(li, names, glist)
        dh = dh + token[:1, :1]
    rs_finish(pending, dh)
    for n, o in zip(['ffn_w_gate', 'ffn_w_up', 'ffn_w_down'], ffn_out):
        grads[n] = o.reshape(W[n].shape)
    grad_x = dh[NM:NM + SEQ][None]
    GR['meta_tokens'] = dh[:NM]
    GR['norm_w'] = jnp.stack([jnp.concatenate(r, axis=0) for r in d_normw], axis=0)
    GR['final_norm_w'] = d_final_w

    small_names = [n for n, _ in _SMALL]
    small_full_shapes = [GR[n].shape for n in small_names]
    packed = _pack([GR[n] for n in small_names] + [loss_vec[:, :1]])
    red = _allreduce_small("allreduce_small", packed)
    parts = _unpack(red, small_full_shapes + [(1, 1)])
    loss = parts[-1].reshape(())
    for (n, sharded), full in zip(_SMALL, parts[:-1]):
        full = full.reshape(W[n].shape[:-1] + (-1,))
        grads[n] = my_cols(full, W[n].shape[-1]) if sharded else full

    delta, new_m, new_v = {}, {}, {}
    for n in _WEIGHTS:
        shp = W[n].shape
        d, nm, nv = _adamw(f"adamw_{n}", _as2d(W[n]), _as2d(grads[n]), _as2d(M1[n]), _as2d(V2[n]))
        delta[n], new_m[n], new_v[n] = d.reshape(shp), nm.reshape(shp), nv.reshape(shp)

    return (loss, grad_x, *[grads[n] for n in _WEIGHTS], *[delta[n] for n in _WEIGHTS],
            *[new_m[n] for n in _WEIGHTS], *[new_v[n] for n in _WEIGHTS])
```

```python
import functools
import math

import jax
import jax.numpy as jnp
import numpy as np
from jax import lax
from jax.experimental import pallas as pl
from jax.experimental.pallas import tpu as pltpu

F32 = jnp.float32
BF16 = jnp.bfloat16
MESH = pl.DeviceIdType.MESH

EPS = 1e-6
CHUNK = 64
ROW_BLOCK = 128
SWA_WINDOW = 128
SWA_GROUP = 8
POOL_WINDOWS = (2, 4, 8, 16)
GDN_CONV = 4
ROPE_THETA = 10000.0
NEG = -1e30
N_CHIPS = 4
N_DEV = 8
LANES = 128
WIDE_TILE = 896
VMEM_LIMIT = 56 * 1024 * 1024
VMEM_BUDGET = 36 * 1024 * 1024

ADAM_LR = 0.001
ADAM_B1 = 0.9
ADAM_B2 = 0.999
ADAM_EPS = 1e-08
ADAM_WD = 0.01
ADAM_STEP = 10

_NN = ((1,), (0,))
_NT = ((1,), (1,))
_TN = ((0,), (0,))


def _cp(dims=None):
    return pltpu.CompilerParams(dimension_semantics=dims, vmem_limit_bytes=VMEM_LIMIT)


def _round_up(n, m):
    return -(-n // m) * m


def _div(n, cands):
    for c in cands:
        if c <= n and n % c == 0:
            return c
    return n


def _pad_cols(n):
    return _round_up(n, WIDE_TILE) if n > 2048 else _round_up(n, LANES)


def _dg(a, b, dims, prec=None, batched=False):
    if batched:
        dims = (((dims[0][0] + 1,), (dims[1][0] + 1,)), ((0,), (0,)))
    else:
        dims = (dims, ((), ()))
    return lax.dot_general(a, b, dims, precision=prec, preferred_element_type=F32)


def _make_dots(cast, batched=False):
    prec = None if cast is not None else lax.Precision.HIGHEST
    c = (lambda t: t.astype(cast)) if cast is not None else (lambda t: t)
    rnn = lambda a, b: _dg(c(a), c(b), _NN, prec, batched)
    rnt = lambda a, b: _dg(c(a), c(b), _NT, prec, batched)
    rtn = lambda a, b: _dg(c(a), c(b), _TN, prec, batched)

    @jax.custom_vjp
    def nn(a, b):
        return rnn(a, b)
    nn.defvjp(lambda a, b: (rnn(a, b), (a, b)), lambda r, ct: (rnt(ct, r[1]), rtn(r[0], ct)))

    @jax.custom_vjp
    def nt(a, b):
        return rnt(a, b)
    nt.defvjp(lambda a, b: (rnt(a, b), (a, b)), lambda r, ct: (rnn(ct, r[1]), rtn(ct, r[0])))

    @jax.custom_vjp
    def tn(a, b):
        return rtn(a, b)
    tn.defvjp(lambda a, b: (rtn(a, b), (a, b)), lambda r, ct: (rnt(r[1], ct), rnn(r[0], ct)))
    return nn, nt, tn, rnn, rnt, rtn


_bnn, _bnt, _btn, _rbnn, _rbnt, _rbtn = _make_dots(BF16)
_hnn, _hnt, _htn, _rhnn, _rhnt, _rhtn = _make_dots(None)
_qnn, _qnt, _qtn, _rqnn, _rqnt, _rqtn = _make_dots(BF16, batched=True)


def _sigmoid(x):
    return 1.0 / (1.0 + jnp.exp(-x))


def _silu(x):
    return x * _sigmoid(x)


def _softplus(x):
    return jnp.maximum(x, 0.0) + jnp.log(1.0 + jnp.exp(-jnp.abs(x)))


def _log_sigmoid(x):
    return -_softplus(-x)


def _iota(shape, dim):
    return lax.broadcasted_iota(jnp.int32, shape, dim)


def _matmul(name, form, grid, a_ops, b_ops, acc_ids, n_acc, extras, outs, epilogue, tm, tn,
            alias_inputs=(), alias_map=None):
    na, nb, ne, nal, no = len(a_ops), len(b_ops), len(extras), len(alias_inputs), len(outs)
    nk = grid[2]
    dims = {"nn": _NN, "nt": _NT, "tn": _TN}[form]

    def body(*refs):
        a_refs = refs[:na]
        b_refs = refs[na:na + nb]
        e_refs = refs[na + nb:na + nb + ne]
        o_refs = refs[na + nb + ne + nal:na + nb + ne + nal + no]
        acc = refs[-1]
        k = pl.program_id(2)

        @pl.when(k == 0)
        def _():
            acc[...] = jnp.zeros_like(acc)

        for p in range(na):
            acc[acc_ids[p]] += _dg(a_refs[p][...].astype(BF16), b_refs[p][...].astype(BF16), dims)

        @pl.when(k == nk - 1)
        def _():
            res = epilogue([acc[i] for i in range(n_acc)], [e[...] for e in e_refs])
            for o, r in zip(o_refs, res):
                o[...] = r.astype(o.dtype)

    ops = list(a_ops) + list(b_ops) + list(extras)
    in_specs = [pl.BlockSpec(bs, im) for (_, bs, im) in ops] + [pl.BlockSpec(memory_space=pl.ANY)] * nal
    aliases = {}
    if alias_map:
        aliases = {len(ops) + i: o for i, o in alias_map.items()}
    res = pl.pallas_call(
        body, name=name, grid=grid,
        in_specs=in_specs,
        out_specs=[pl.BlockSpec(bs, im) for (_, bs, im) in outs],
        out_shape=[s for (s, _, _) in outs],
        scratch_shapes=[pltpu.VMEM((n_acc, tm, tn), F32)],
        input_output_aliases=aliases,
        compiler_params=_cp(("parallel", "parallel", "arbitrary")),
    )(*[o[0] for o in ops], *alias_inputs)
    return res


def _tiles(M, N, K, fixed_bytes_per_tm_tn, k_cands=(512, 384, 256, 128), n_cands=(2048, 1792, 1408, 1280, 1024, 896, 768, 640, 512, 384, 256, 128),
           m_cands=(1408, 1056, 704, 528, 384, 256, 128, 64, 32, 16, 8), a_bytes=2, b_bytes=2, n_pairs=1):
    tn = _div(N, n_cands)
    tk = _div(K, k_cands)
    for tm in m_cands:
        if tm > M or M % tm:
            continue
        est = tm * tn * fixed_bytes_per_tm_tn + n_pairs * 2 * (tm * tk * a_bytes + tk * tn * b_bytes)
        if est <= VMEM_BUDGET:
            return tm, tn, tk
    return _div(M, (8,)), tn, tk


def _mm(name, form, a, b, out_dtype, epilogue=None, extras=(), extra_kinds=(), n_out=1, out_dtypes=None):
    if form == "nn":
        (M, K), N = a.shape, b.shape[1]
    elif form == "nt":
        (M, K), N = a.shape, b.shape[0]
    else:
        (K, M), N = a.shape, b.shape[1]
    out_dtypes = out_dtypes or [out_dtype] * n_out
    per = 4 + sum(2 * jnp.dtype(d).itemsize for d in out_dtypes)
    per += sum(2 * e.dtype.itemsize for e, kd in zip(extras, extra_kinds) if kd == "tile")
    kc = (1408, 1056, 704, 512, 384, 256, 128) if form == "tn" else (896, 512, 384, 256, 128)
    mc = (1024, 896, 768, 640, 512, 384, 256, 128) if form == "tn" else (1408, 1056, 704, 528, 384, 256, 128, 64, 32, 16, 8)
    tm, tn, tk = _tiles(M, N, K, per, k_cands=kc, m_cands=mc, a_bytes=a.dtype.itemsize, b_bytes=b.dtype.itemsize)
    grid = (M // tm, N // tn, K // tk)
    if form == "nn":
        a_op = (a, (tm, tk), lambda i, j, k: (i, k))
        b_op = (b, (tk, tn), lambda i, j, k: (k, j))
    elif form == "nt":
        a_op = (a, (tm, tk), lambda i, j, k: (i, k))
        b_op = (b, (tn, tk), lambda i, j, k: (j, k))
    else:
        a_op = (a, (tk, tm), lambda i, j, k: (k, i))
        b_op = (b, (tk, tn), lambda i, j, k: (k, j))
    e_ops = []
    for e, kd in zip(extras, extra_kinds):
        if kd == "tile":
            e_ops.append((e, (tm, tn), lambda i, j, k: (i, j)))
        else:
            e_ops.append((e, (1, tn), lambda i, j, k: (0, j)))
    outs = [(jax.ShapeDtypeStruct((M, N), d), (tm, tn), lambda i, j, k: (i, j)) for d in out_dtypes]

    def epi(accs, ex):
        if epilogue is None:
            return [accs[0]]
        r = epilogue(accs[0], *ex)
        return list(r) if isinstance(r, (tuple, list)) else [r]

    res = _matmul(name, form, grid, [a_op], [b_op], [0], 1, e_ops, outs, epi, tm, tn)
    return res[0] if len(res) == 1 else res


def _rowwise(name, fn, ins, outs, tr, rows):
    n_in, n_out = len(ins), len(outs)
    nblk = rows // tr

    def body(*refs):
        i = pl.program_id(0)
        vals = fn(i * tr, *[r[...] for r in refs[:n_in]])
        for o, v, (_, _, kind) in zip(refs[n_in:], vals, outs):
            if kind == "row":
                o[...] = v.astype(o.dtype)
            else:
                @pl.when(i == 0)
                def _(o=o):
                    o[...] = jnp.zeros_like(o)
                o[...] += v.astype(o.dtype)

    in_specs = []
    for arr, w, cb in ins:
        if arr.shape[0] == 1 and rows != 1:
            in_specs.append(pl.BlockSpec((1, w), lambda i, cb=cb: (0, cb)))
        else:
            in_specs.append(pl.BlockSpec((tr, w), lambda i, cb=cb: (i, cb)))
    out_specs, out_shape = [], []
    for w, d, kind in outs:
        if kind == "row":
            out_specs.append(pl.BlockSpec((tr, w), lambda i: (i, 0)))
            out_shape.append(jax.ShapeDtypeStruct((rows, w), d))
        else:
            out_specs.append(pl.BlockSpec((1, w), lambda i: (0, 0)))
            out_shape.append(jax.ShapeDtypeStruct((1, w), d))
    return pl.pallas_call(
        body, name=name, grid=(nblk,), in_specs=in_specs, out_specs=out_specs, out_shape=out_shape,
        compiler_params=_cp(("arbitrary",)),
    )(*[a for a, _, _ in ins])


def _full(arr):
    return (arr, arr.shape[1], 0)


def _rmsnorm_fwd(name, h, w, out_dtype):
    D = h.shape[1]

    def fn(_, hb, wb):
        rstd = lax.rsqrt(jnp.mean(hb * hb, axis=1, keepdims=True) + EPS)
        return [hb * rstd * wb]
    return _rowwise(name, fn, [_full(h), _full(w)], [(D, out_dtype, "row")], ROW_BLOCK, h.shape[0])[0]


def _rmsnorm_bwd(name, h, w, dn, dh_in):
    D = h.shape[1]

    def fn(_, hb, wb, dnb, dhb):
        rstd = lax.rsqrt(jnp.mean(hb * hb, axis=1, keepdims=True) + EPS)
        xhat = hb * rstd
        dxh = dnb.astype(F32) * wb
        dh = rstd * (dxh - xhat * jnp.mean(dxh * xhat, axis=1, keepdims=True))
        return [dhb + dh, jnp.sum(dnb.astype(F32) * xhat, axis=0, keepdims=True)]
    return _rowwise(name, fn, [_full(h), _full(w), _full(dn), _full(dh_in)],
                    [(D, F32, "row"), (D, F32, "acc")], ROW_BLOCK, h.shape[0])


def _headnorm_fn(group, act):
    def f(o, gate, w):
        rstd = lax.rsqrt(jnp.mean(o * o, axis=1, keepdims=True) + EPS)
        return o * rstd * w * act(gate)
    return f


def _headnorm_fwd(name, o, gate_arr, gate_w, gate_cb, w, group, act):
    N = o.shape[1]
    f = _headnorm_fn(group, act)

    def fn(_, ob, gb, wb):
        parts = [f(ob[:, s:s + group], gb[:, s:s + group], wb[:, s:s + group]) for s in range(0, N, group)]
        return [jnp.concatenate(parts, axis=1)]
    return _rowwise(name, fn, [_full(o), (gate_arr, gate_w, gate_cb), _full(w)], [(N, BF16, "row")], ROW_BLOCK, o.shape[0])[0]


def _headnorm_bwd(name, o, gate_arr, gate_w, gate_cb, w, dout, group, act):
    N = o.shape[1]
    f = _headnorm_fn(group, act)

    def fn(_, ob, gb, wb, db):
        dos, dgs, dws = [], [], []
        for s in range(0, N, group):
            _, vjp = jax.vjp(f, ob[:, s:s + group], gb[:, s:s + group], jnp.broadcast_to(wb[:, s:s + group], (ob.shape[0], group)))
            do, dgt, dw = vjp(db[:, s:s + group])
            dos.append(do)
            dgs.append(dgt)
            dws.append(jnp.sum(dw, axis=0, keepdims=True))
        return [jnp.concatenate(dos, axis=1), jnp.concatenate(dgs, axis=1), jnp.concatenate(dws, axis=1)]
    return _rowwise(name, fn, [_full(o), (gate_arr, gate_w, gate_cb), _full(w), _full(dout)],
                    [(N, F32, "row"), (N, F32, "row"), (N, F32, "acc")], ROW_BLOCK, o.shape[0])


def _colsum(name, a):
    def fn(_, ab):
        return [jnp.sum(ab.astype(F32), axis=0, keepdims=True)]
    tr = _div(a.shape[0], (512, 384, 256, 128, 64))
    return _rowwise(name, fn, [_full(a)], [(a.shape[1], F32, "acc")], tr, a.shape[0])[0]


def _loss_head(name, h, w, tgt, n_meta, seq):
    D = h.shape[1]
    tr = ROW_BLOCK

    def fn(row0, hb, wb, tb):
        row = row0 + _iota((tr, 1), 0)
        valid = (row >= n_meta) & (row < n_meta + seq)
        rstd = lax.rsqrt(jnp.mean(hb * hb, axis=1, keepdims=True) + EPS)
        xhat = hb * rstd
        err = jnp.where(valid, xhat * wb - tb, 0.0)
        loss = 0.5 * jnp.sum(jnp.mean(err * err, axis=1, keepdims=True), axis=0, keepdims=True)
        dy = err * (1.0 / D)
        dxh = dy * wb
        dh = rstd * (dxh - xhat * jnp.mean(dxh * xhat, axis=1, keepdims=True))
        return [dh, jnp.sum(dy * xhat, axis=0, keepdims=True), jnp.broadcast_to(loss, (1, LANES))]
    return _rowwise(name, fn, [_full(h), _full(w), _full(tgt)],
                    [(D, F32, "row"), (D, F32, "acc"), (LANES, F32, "acc")], tr, h.shape[0])


def _chunk_valid(ci, n_meta):
    lim = jnp.where(ci == 0, n_meta, CHUNK)
    return _iota((CHUNK, 1), 0) < lim, _iota((1, CHUNK), 1) < lim


def _tri_masks():
    r = _iota((CHUNK, CHUNK), 0)
    c = _iota((CHUNK, CHUNK), 1)
    return r >= c, r > c, r <= c


def _mlstm_chunk(vc, vr, m_st, c_st, n_st, q, k, v, li_c, lf_c, li_r, lf_r):
    tril, _, triu = _tri_masks()
    dk = q.shape[1]
    li_c = jnp.where(vc, li_c, NEG)
    li_r = jnp.where(vr, li_r, NEG)
    lf_c = jnp.where(vc, _log_sigmoid(lf_c), 0.0)
    lf_r = jnp.where(vr, _log_sigmoid(lf_r), 0.0)
    b_c = jnp.sum(jnp.where(tril, lf_r, 0.0), axis=1, keepdims=True)
    b_r = jnp.sum(jnp.where(triu, lf_c, 0.0), axis=0, keepdims=True)
    b_last = jnp.sum(lf_r, axis=1, keepdims=True)
    log_w = jnp.where(tril, b_c - b_r + li_r, NEG)
    log_init = b_c + m_st
    m_t = lax.stop_gradient(jnp.maximum(log_init, jnp.max(log_w, axis=1, keepdims=True)))
    w = jnp.exp(log_w - m_t)
    w_init = jnp.exp(log_init - m_t)
    qs = q * (dk ** -0.5)
    qk = _bnt(qs, k) * w
    num = w_init * _bnn(qs, c_st) + _bnn(qk, v)
    den = w_init * jnp.sum(qs * n_st, axis=1, keepdims=True) + jnp.sum(qk, axis=1, keepdims=True)
    h = num / jnp.maximum(jnp.abs(den), jnp.exp(-m_t))
    log_end_init = b_last + m_st
    log_end_r = b_last - b_r + li_r
    m_new = lax.stop_gradient(jnp.maximum(log_end_init, jnp.max(log_end_r, axis=1, keepdims=True)))
    a_init = jnp.exp(log_end_init - m_new)
    a_c = jnp.exp(b_last - b_c + li_c - m_new)
    ka = k * a_c
    c_new = a_init * c_st + _btn(ka, v)
    n_new = a_init * n_st + jnp.sum(ka, axis=0, keepdims=True)
    return (c_new, n_new, h), m_new


HEADS_PER_STEP = 4


def _head_batch(heads):
    return _div(heads, (HEADS_PER_STEP, 2, 1))


def _mlstm_io_specs(heads, hb, dk, dv, NC, rev):
    ci = (lambda c: NC - 1 - c) if rev else (lambda c: c)
    nb = heads // hb
    q = pl.BlockSpec((CHUNK, hb * dk), lambda h, c: (ci(c), h))
    k = pl.BlockSpec((CHUNK, hb * dk), lambda h, c: (ci(c), nb + h))
    v = pl.BlockSpec((CHUNK, hb * dv), lambda h, c: (ci(c), (2 * heads * dk) // (hb * dv) + h))
    col = pl.BlockSpec((hb, CHUNK, 1), lambda h, c: (h, ci(c), 0))
    row = pl.BlockSpec((hb, 1, 1, CHUNK), lambda h, c: (h, ci(c), 0, 0))
    st = [pl.BlockSpec((hb, 1, dk, dv), lambda h, c: (h, ci(c), 0, 0)),
          pl.BlockSpec((hb, 1, 1, dk), lambda h, c: (h, ci(c), 0, 0)),
          pl.BlockSpec((hb, 1, 1, 1), lambda h, c: (h, ci(c), 0, 0))]
    wide = pl.BlockSpec((CHUNK, hb * dv), lambda h, c: (ci(c), h))
    return q, k, v, col, row, st, wide


def _mlstm_core_fwd(pc, li_c, lf_c, li_r, lf_r, heads, dk, dv, n_meta):
    TC = pc.shape[0]
    NC = TC // CHUNK
    hb = _head_batch(heads)

    def body(q_ref, k_ref, v_ref, lic, lfc, lir, lfr, h_ref, cs_ref, ns_ref, ms_ref, c_s, n_s, m_s):
        ci = pl.program_id(1)

        @pl.when(ci == 0)
        def _():
            c_s[...] = jnp.zeros_like(c_s)
            n_s[...] = jnp.zeros_like(n_s)
            m_s[...] = jnp.zeros_like(m_s)

        vc, vr = _chunk_valid(ci, n_meta)
        for j in range(hb):
            cs_ref[j, 0] = c_s[j]
            ns_ref[j, 0] = n_s[j]
            ms_ref[j, 0] = m_s[j]
            (c_new, n_new, h), m_new = _mlstm_chunk(
                vc, vr, m_s[j], c_s[j], n_s[j], q_ref[:, j * dk:(j + 1) * dk], k_ref[:, j * dk:(j + 1) * dk],
                v_ref[:, j * dv:(j + 1) * dv], lic[j], lfc[j], lir[j, 0], lfr[j, 0])
            h_ref[:, j * dv:(j + 1) * dv] = h
            c_s[j] = c_new
            n_s[j] = n_new
            m_s[j] = m_new

    q, k, v, col, row, st, wide = _mlstm_io_specs(heads, hb, dk, dv, NC, False)
    return pl.pallas_call(
        body, name="mlstm_core_fwd", grid=(heads // hb, NC),
        in_specs=[q, k, v, col, col, row, row],
        out_specs=[wide] + st,
        out_shape=[jax.ShapeDtypeStruct((TC, heads * dv), F32),
                   jax.ShapeDtypeStruct((heads, NC, dk, dv), F32),
                   jax.ShapeDtypeStruct((heads, NC, 1, dk), F32),
                   jax.ShapeDtypeStruct((heads, NC, 1, 1), F32)],
        scratch_shapes=[pltpu.VMEM((hb, dk, dv), F32), pltpu.VMEM((hb, 1, dk), F32), pltpu.VMEM((hb, 1, 1), F32)],
        compiler_params=_cp(("parallel", "arbitrary")),
    )(pc, pc, pc, li_c, lf_c, li_r, lf_r)


def _mlstm_core_bwd(pc, li_c, lf_c, li_r, lf_r, cs, ns, ms, dh, heads, dk, dv, n_meta):
    TC = pc.shape[0]
    NC = TC // CHUNK
    hb = _head_batch(heads)

    def body(q_ref, k_ref, v_ref, lic, lfc, lir, lfr, cs_ref, ns_ref, ms_ref, dh_ref,
             dq_ref, dk_ref, dv_ref, dlic, dlfc, dlir, dlfr, dc_s, dn_s):
        step = pl.program_id(1)
        ci = NC - 1 - step

        @pl.when(step == 0)
        def _():
            dc_s[...] = jnp.zeros_like(dc_s)
            dn_s[...] = jnp.zeros_like(dn_s)

        vc, vr = _chunk_valid(ci, n_meta)
        for j in range(hb):
            ks, vs = slice(j * dk, (j + 1) * dk), slice(j * dv, (j + 1) * dv)
            m_st = ms_ref[j, 0]
            fn = lambda *a, m_st=m_st: _mlstm_chunk(vc, vr, m_st, *a)
            _, vjp, _ = jax.vjp(fn, cs_ref[j, 0], ns_ref[j, 0], q_ref[:, ks], k_ref[:, ks], v_ref[:, vs],
                                lic[j], lfc[j], lir[j, 0], lfr[j, 0], has_aux=True)
            dc, dn, dq, dkk, dvv, g0, g1, g2, g3 = vjp((dc_s[j], dn_s[j], dh_ref[:, vs]))
            dq_ref[:, ks] = dq
            dk_ref[:, ks] = dkk
            dv_ref[:, vs] = dvv
            dlic[j] = g0
            dlfc[j] = g1
            dlir[j, 0] = g2
            dlfr[j, 0] = g3
            dc_s[j] = dc
            dn_s[j] = dn

    q, k, v, col, row, st, wide = _mlstm_io_specs(heads, hb, dk, dv, NC, True)
    return pl.pallas_call(
        body, name="mlstm_core_bwd", grid=(heads // hb, NC),
        in_specs=[q, k, v, col, col, row, row] + st + [wide],
        out_specs=[q, q, wide, col, col, row, row],
        out_shape=[jax.ShapeDtypeStruct((TC, heads * dk), F32), jax.ShapeDtypeStruct((TC, heads * dk), F32),
                   jax.ShapeDtypeStruct((TC, heads * dv), F32),
                   jax.ShapeDtypeStruct(li_c.shape, F32), jax.ShapeDtypeStruct(lf_c.shape, F32),
                   jax.ShapeDtypeStruct(li_r.shape, F32), jax.ShapeDtypeStruct(lf_r.shape, F32)],
        scratch_shapes=[pltpu.VMEM((hb, dk, dv), F32), pltpu.VMEM((hb, 1, dk), F32)],
        compiler_params=_cp(("parallel", "arbitrary")),
    )(pc, pc, pc, li_c, lf_c, li_r, lf_r, cs, ns, ms, dh)


@jax.custom_vjp
def _tri_solve(low, rhs):
    return _tri_solve_fwd(low, rhs)[0]


def _tri_solve_fwd(low, rhs):
    levels = int(math.log2(low.shape[-1]))
    p = -low
    r = p
    for i in range(levels):
        if i > 0:
            r = r + p + _rqnn(p, r)
        if i < levels - 1:
            p = _rqnn(p, p)
    sol = rhs + _rqnn(r, rhs)
    return sol, (r, sol)


def _tri_solve_bwd(res, ct):
    r, sol = res
    d_rhs = ct + _rqtn(r, ct)
    return -_rqnt(d_rhs, sol), d_rhs


_tri_solve.defvjp(_tri_solve_fwd, _tri_solve_bwd)


def _l2norm(x):
    return x * lax.rsqrt(jnp.sum(x * x, axis=-1, keepdims=True) + EPS)


def _lane_heads(ref, n, width):
    return jnp.stack([ref[:, j * width:(j + 1) * width] for j in range(n)], axis=0)


def _gdn_chunk(vc, vr, s_st, q, k, v, g_c, b_c, g_r):
    tril, strict, triu = _tri_masks()
    dk = q.shape[-1]
    pair = lambda t: jnp.concatenate([t[j:j + 1] for j in range(t.shape[0]) for _ in (0, 1)], axis=0)
    qn = _l2norm(q) * (dk ** -0.5)
    kn = _l2norm(k)
    qk = pair(_qnt(qn, kn))
    qn, kn = pair(qn), pair(kn)
    g_c = jnp.where(vc, g_c, 0.0)
    g_r = jnp.where(vr, g_r, 0.0)
    b_c = jnp.where(vc, b_c, 0.0)
    gc_c = jnp.sum(jnp.where(tril, g_r, 0.0), axis=2, keepdims=True)
    gc_r = jnp.sum(jnp.where(triu, g_c, 0.0), axis=1, keepdims=True)
    g_last = jnp.sum(g_r, axis=2, keepdims=True)
    decay = jnp.exp(jnp.where(tril, gc_c - gc_r, NEG))
    kb = kn * b_c
    low = jnp.where(strict, _qnt(kb, kn) * decay, 0.0)
    eg = jnp.exp(gc_c)
    sol = _tri_solve(low, jnp.concatenate([v * b_c, kb * eg], axis=2))
    u_vec, w_vec = sol[:, :, :dk], sol[:, :, dk:]
    v_new = u_vec - _qnn(w_vec, s_st)
    o = _qnn(qn * eg, s_st) + _qnn(qk * decay, v_new)
    s_new = jnp.exp(g_last) * s_st + _qtn(kn * jnp.exp(g_last - gc_c), v_new)
    return s_new, o


def _gdn_core_fwd(qkv, g_c, b_c, g_r, qk_heads, dk, n_meta):
    TC = qkv.shape[0]
    NC = TC // CHUNK
    H = qk_heads

    hb = _head_batch(H)

    def body(q_ref, k_ref, v_ref, gc, bc, gr, o_ref, st_ref, s_s):
        ci = pl.program_id(1)

        @pl.when(ci == 0)
        def _():
            s_s[...] = jnp.zeros_like(s_s)

        vc, vr = _chunk_valid(ci, n_meta)
        s_st = s_s[...]
        st_ref[:, 0] = s_st.reshape(hb, 2, dk, dk)
        s_new, o = _gdn_chunk(vc, vr, s_st, _lane_heads(q_ref, hb, dk), _lane_heads(k_ref, hb, dk),
                              _lane_heads(v_ref, 2 * hb, dk), gc[...], bc[...], gr[:, 0])
        for b in range(2 * hb):
            o_ref[:, b * dk:(b + 1) * dk] = o[b]
        s_s[...] = s_new

    q, k, v, col, row, st = _gdn_io_specs(H, hb, dk, NC, False)
    return pl.pallas_call(
        body, name="gdn_core_fwd", grid=(H // hb, NC),
        in_specs=[q, k, v, col, col, row],
        out_specs=[_gdn_wide_spec(hb, dk, NC, False), st],
        out_shape=[jax.ShapeDtypeStruct((TC, 2 * H * dk), F32), jax.ShapeDtypeStruct((H, NC, 2, dk, dk), F32)],
        scratch_shapes=[pltpu.VMEM((2 * hb, dk, dk), F32)],
        compiler_params=_cp(("parallel", "arbitrary")),
    )(qkv, qkv, qkv, g_c, b_c, g_r)


def _gdn_wide_spec(hb, dk, NC, rev):
    ci = (lambda c: NC - 1 - c) if rev else (lambda c: c)
    return pl.BlockSpec((CHUNK, 2 * hb * dk), lambda h, c: (ci(c), h))


def _gdn_io_specs(H, hb, dk, NC, rev):
    ci = (lambda c: NC - 1 - c) if rev else (lambda c: c)
    nb = H // hb
    q = pl.BlockSpec((CHUNK, hb * dk), lambda h, c: (ci(c), h))
    k = pl.BlockSpec((CHUNK, hb * dk), lambda h, c: (ci(c), nb + h))
    v = pl.BlockSpec((CHUNK, 2 * hb * dk), lambda h, c: (ci(c), nb + h))
    col = pl.BlockSpec((2 * hb, CHUNK, 1), lambda h, c: (h, ci(c), 0))
    row = pl.BlockSpec((2 * hb, 1, 1, CHUNK), lambda h, c: (h, ci(c), 0, 0))
    st = pl.BlockSpec((hb, 1, 2, dk, dk), lambda h, c: (h, ci(c), 0, 0, 0))
    return q, k, v, col, row, st


def _gdn_core_bwd(qkv, g_c, b_c, g_r, st, do, qk_heads, dk, n_meta):
    TC = qkv.shape[0]
    NC = TC // CHUNK
    H = qk_heads
    hb = _head_batch(H)

    def body(q_ref, k_ref, v_ref, gc, bc, gr, st_ref, do_ref, dq_ref, dk_ref, dv_ref, dgc, dbc, dgr, ds_s):
        step = pl.program_id(1)
        ci = NC - 1 - step

        @pl.when(step == 0)
        def _():
            ds_s[...] = jnp.zeros_like(ds_s)

        vc, vr = _chunk_valid(ci, n_meta)
        fn = lambda *a: _gdn_chunk(vc, vr, *a)
        _, vjp = jax.vjp(fn, st_ref[:, 0].reshape(2 * hb, dk, dk), _lane_heads(q_ref, hb, dk), _lane_heads(k_ref, hb, dk),
                         _lane_heads(v_ref, 2 * hb, dk), gc[...], bc[...], gr[:, 0])
        ds, dq, dkk, dvv, d_gc, d_bc, d_gr = vjp((ds_s[...], _lane_heads(do_ref, 2 * hb, dk)))
        ds_s[...] = ds
        for j in range(hb):
            dq_ref[:, j * dk:(j + 1) * dk] = dq[j]
            dk_ref[:, j * dk:(j + 1) * dk] = dkk[j]
        for b in range(2 * hb):
            dv_ref[:, b * dk:(b + 1) * dk] = dvv[b]
        dgc[...] = d_gc
        dbc[...] = d_bc
        dgr[:, 0] = d_gr

    q, k, v, col, row, stspec = _gdn_io_specs(H, hb, dk, NC, True)
    wide = _gdn_wide_spec(hb, dk, NC, True)
    return pl.pallas_call(
        body, name="gdn_core_bwd", grid=(H // hb, NC),
        in_specs=[q, k, v, col, col, row, stspec, wide],
        out_specs=[q, q, wide, col, col, row],
        out_shape=[jax.ShapeDtypeStruct((TC, H * dk), F32), jax.ShapeDtypeStruct((TC, H * dk), F32),
                   jax.ShapeDtypeStruct((TC, 2 * H * dk), F32),
                   jax.ShapeDtypeStruct(g_c.shape, F32), jax.ShapeDtypeStruct(g_c.shape, F32),
                   jax.ShapeDtypeStruct(g_r.shape, F32)],
        scratch_shapes=[pltpu.VMEM((2 * hb, dk, dk), F32)],
        compiler_params=_cp(("parallel", "arbitrary")),
    )(qkv, qkv, qkv, g_c, b_c, g_r, st, do)


def _gdn_gate_fn(a_pre, b_pre, a_log, dt_bias):
    return -jnp.exp(a_log) * _softplus(a_pre + dt_bias), _sigmoid(b_pre)


def _gdn_gates_fwd(a_pre, b_pre, a_log, dt_bias):
    n = a_pre.shape[1]

    def fn(_, ab, bb, al, dt):
        g, beta = _gdn_gate_fn(ab, bb, al, dt)
        return [g, beta]
    return _rowwise("gdn_gates_fwd", fn, [_full(a_pre), _full(b_pre), _full(a_log), _full(dt_bias)],
                    [(n, F32, "row"), (n, F32, "row")], ROW_BLOCK, a_pre.shape[0])


def _gdn_gates_bwd(a_pre, b_pre, a_log, dt_bias, dg, dbeta):
    n = a_pre.shape[1]

    def fn(_, ab, bb, al, dt, dgb, dbb):
        rows = ab.shape[0]
        _, vjp = jax.vjp(_gdn_gate_fn, ab, bb, jnp.broadcast_to(al, (rows, n)), jnp.broadcast_to(dt, (rows, n)))
        da, db, dal, ddt = vjp((dgb, dbb))
        return [da, db, jnp.sum(dal, axis=0, keepdims=True), jnp.sum(ddt, axis=0, keepdims=True)]
    return _rowwise("gdn_gates_bwd", fn, [_full(a_pre), _full(b_pre), _full(a_log), _full(dt_bias), _full(dg), _full(dbeta)],
                    [(n, F32, "row"), (n, F32, "row"), (n, F32, "acc"), (n, F32, "acc")], ROW_BLOCK, a_pre.shape[0])


def _shift_down(cur, prev, j):
    row = _iota(cur.shape, 0)
    return jnp.where(row >= j, pltpu.roll(cur, j, 0), pltpu.roll(prev, j, 0))


def _shift_up(cur, nxt, j):
    n = cur.shape[0]
    row = _iota(cur.shape, 0)
    return jnp.where(row < n - j, pltpu.roll(cur, n - j, 0), pltpu.roll(nxt, n - j, 0))


def _conv_acc(cur, prev, w):
    acc = cur * w[GDN_CONV - 1:GDN_CONV, :]
    for j in range(1, GDN_CONV):
        acc = acc + _shift_down(cur, prev, j) * w[GDN_CONV - 1 - j:GDN_CONV - j, :]
    return acc


def _conv_tiles(width):
    return _div(width, (1024, 512, 256, 128))


def _conv_fwd(p, w, width):
    TP = p.shape[0]
    nb, tn = TP // ROW_BLOCK, _conv_tiles(width)

    def body(cur_ref, prev_ref, w_ref, y_ref):
        i = pl.program_id(1)
        prev = jnp.where(i > 0, prev_ref[...], 0.0)
        y_ref[...] = _silu(_conv_acc(cur_ref[...], prev, w_ref[...]))

    return pl.pallas_call(
        body, name="gdn_conv_fwd", grid=(width // tn, nb),
        in_specs=[pl.BlockSpec((ROW_BLOCK, tn), lambda j, i: (i, j)),
                  pl.BlockSpec((ROW_BLOCK, tn), lambda j, i: (jnp.maximum(i - 1, 0), j)),
                  pl.BlockSpec((GDN_CONV, tn), lambda j, i: (0, j))],
        out_specs=pl.BlockSpec((ROW_BLOCK, tn), lambda j, i: (i, j)),
        out_shape=jax.ShapeDtypeStruct((TP, width), F32),
        compiler_params=_cp(("parallel", "arbitrary")),
    )(p, p, w)


def _conv_bwd_pre(p, w, dy, width):
    TP = p.shape[0]
    nb, tn = TP // ROW_BLOCK, _conv_tiles(width)

    def body(cur_ref, prev_ref, w_ref, dy_ref, da_ref, dw_ref):
        i = pl.program_id(1)
        cur = cur_ref[...]
        prev = jnp.where(i > 0, prev_ref[...], 0.0)
        acc = _conv_acc(cur, prev, w_ref[...])
        s = _sigmoid(acc)
        da = dy_ref[...] * (s * (1.0 + acc * (1.0 - s)))
        da_ref[...] = da

        @pl.when(i == 0)
        def _():
            dw_ref[...] = jnp.zeros_like(dw_ref)

        rows = [jnp.sum(da * (cur if j == 0 else _shift_down(cur, prev, j)), axis=0, keepdims=True)
                for j in range(GDN_CONV - 1, -1, -1)]
        dw_ref[...] += jnp.concatenate(rows, axis=0)

    return pl.pallas_call(
        body, name="gdn_conv_bwd_pre", grid=(width // tn, nb),
        in_specs=[pl.BlockSpec((ROW_BLOCK, tn), lambda j, i: (i, j)),
                  pl.BlockSpec((ROW_BLOCK, tn), lambda j, i: (jnp.maximum(i - 1, 0), j)),
                  pl.BlockSpec((GDN_CONV, tn), lambda j, i: (0, j)),
                  pl.BlockSpec((ROW_BLOCK, tn), lambda j, i: (i, j))],
        out_specs=[pl.BlockSpec((ROW_BLOCK, tn), lambda j, i: (i, j)), pl.BlockSpec((GDN_CONV, tn), lambda j, i: (0, j))],
        out_shape=[jax.ShapeDtypeStruct((TP, width), F32), jax.ShapeDtypeStruct((GDN_CONV, width), F32)],
        compiler_params=_cp(("parallel", "arbitrary")),
    )(p, p, w, dy)


def _conv_bwd_dx(da, w, width):
    TP = da.shape[0]
    nb, tn = TP // ROW_BLOCK, _conv_tiles(width)

    def body(cur_ref, nxt_ref, w_ref, dx_ref):
        i = pl.program_id(1)
        cur = cur_ref[...]
        nxt = jnp.where(i < nb - 1, nxt_ref[...], 0.0)
        w_all = w_ref[...]
        dx = cur * w_all[GDN_CONV - 1:GDN_CONV, :]
        for j in range(1, GDN_CONV):
            dx = dx + _shift_up(cur, nxt, j) * w_all[GDN_CONV - 1 - j:GDN_CONV - j, :]
        dx_ref[...] = dx

    return pl.pallas_call(
        body, name="gdn_conv_bwd_dx", grid=(width // tn, nb),
        in_specs=[pl.BlockSpec((ROW_BLOCK, tn), lambda j, i: (i, j)),
                  pl.BlockSpec((ROW_BLOCK, tn), lambda j, i: (jnp.minimum(i + 1, nb - 1), j)),
                  pl.BlockSpec((GDN_CONV, tn), lambda j, i: (0, j))],
        out_specs=pl.BlockSpec((ROW_BLOCK, tn), lambda j, i: (i, j)),
        out_shape=jax.ShapeDtypeStruct((TP, width), F32),
        compiler_params=_cp(("parallel", "arbitrary")),
    )(da, da, w)


def _pool_bands(i, win):
    n = ROW_BLOCK
    t = _iota((n, n), 0)
    s = _iota((n, n), 1)
    cnt = jnp.minimum(i * n + t + 1, win).astype(F32)
    cur = jnp.where((t - s >= 0) & (t - s < win), 1.0 / cnt, 0.0)
    prev = jnp.where((t + n - s < win) & (i > 0), 1.0 / cnt, 0.0)
    return cur, prev


def _pool_fwd(u):
    TP, D = u.shape
    nb, grp = TP // ROW_BLOCK, D // len(POOL_WINDOWS)

    def body(cur_ref, prev_ref, out_ref):
        i = pl.program_id(0)
        for gi, win in enumerate(POOL_WINDOWS):
            sl = slice(gi * grp, (gi + 1) * grp)
            bc, bp = _pool_bands(i, win)
            cur = cur_ref[:, sl]
            out_ref[:, sl] = (_rhnn(bc, cur) + _rhnn(bp, prev_ref[:, sl]) - cur).astype(out_ref.dtype)

    return pl.pallas_call(
        body, name="pool_fwd", grid=(nb,),
        in_specs=[pl.BlockSpec((ROW_BLOCK, D), lambda i: (i, 0)),
                  pl.BlockSpec((ROW_BLOCK, D), lambda i: (jnp.maximum(i - 1, 0), 0))],
        out_specs=pl.BlockSpec((ROW_BLOCK, D), lambda i: (i, 0)),
        out_shape=jax.ShapeDtypeStruct((TP, D), BF16),
        compiler_params=_cp(("arbitrary",)),
    )(u, u)


def _pool_bwd(dp):
    TP, D = dp.shape
    nb, grp = TP // ROW_BLOCK, D // len(POOL_WINDOWS)

    def body(cur_ref, nxt_ref, out_ref):
        i = pl.program_id(0)
        for gi, win in enumerate(POOL_WINDOWS):
            sl = slice(gi * grp, (gi + 1) * grp)
            bc, _ = _pool_bands(i, win)
            _, bp = _pool_bands(i + 1, win)
            cur = cur_ref[:, sl]
            nxt = jnp.where(i < nb - 1, nxt_ref[:, sl], 0.0)
            out_ref[:, sl] = _rhtn(bc, cur) + _rhtn(bp, nxt) - cur

    return pl.pallas_call(
        body, name="pool_bwd", grid=(nb,),
        in_specs=[pl.BlockSpec((ROW_BLOCK, D), lambda i: (i, 0)),
                  pl.BlockSpec((ROW_BLOCK, D), lambda i: (jnp.minimum(i + 1, nb - 1), 0))],
        out_specs=pl.BlockSpec((ROW_BLOCK, D), lambda i: (i, 0)),
        out_shape=jax.ShapeDtypeStruct((TP, D), F32),
        compiler_params=_cp(("arbitrary",)),
    )(dp, dp)


def _rot_matrix(dh):
    s = _iota((dh, dh), 0)
    t = _iota((dh, dh), 1)
    return jnp.where(s == t + dh // 2, -1.0, 0.0) + jnp.where(s == t - dh // 2, 1.0, 0.0)


def _swa_block(i, t_real, q, k_prev, k_cur, v_prev, v_cur, cos_q, sin_q, cos_p, sin_p, sink):
    n = ROW_BLOCK
    dh = q.shape[1]
    g = q.shape[0] // n
    rot = _rot_matrix(dh)
    rope = lambda x, c, s: x * c + _hnn(x, rot) * s
    qr = rope(q, jnp.concatenate([cos_q] * g, axis=0), jnp.concatenate([sin_q] * g, axis=0))
    kb = jnp.concatenate([rope(k_prev, cos_p, sin_p), rope(k_cur, cos_q, sin_q)], axis=0)
    vb = jnp.concatenate([v_prev, v_cur], axis=0)
    s = _bnt(qr, kb) * (dh ** -0.5)
    qpos = i * n + (_iota((g * n, 2 * n), 0) % n)
    kpos = (i - 1) * n + _iota((g * n, 2 * n), 1)
    mask = (kpos <= qpos) & (qpos - kpos < SWA_WINDOW) & (kpos >= 0) & (kpos < t_real)
    s = jnp.where(mask, s, NEG)
    m = lax.stop_gradient(jnp.maximum(jnp.max(s, axis=1, keepdims=True), sink))
    e = jnp.where(mask, jnp.exp(s - m), 0.0)
    den = jnp.sum(e, axis=1, keepdims=True) + jnp.exp(sink - m)
    return _bnn(e / den, vb)


def _swa_core(q, k, v, cos, sin, sink, t_real, do=None):
    hkv, g, TP, dh = q.shape
    n = ROW_BLOCK
    nb = TP // n
    bwd = do is not None

    def body(*refs):
        q_ref, kp_ref, kc_ref, vp_ref, vc_ref, cq, sq, cpv, spv, sink_ref = refs[:10]
        i = pl.program_id(1)
        fn = lambda *a: _swa_block(i, t_real, *a)
        args = (q_ref[0].reshape(g * n, dh), kp_ref[0], kc_ref[0], vp_ref[0], vc_ref[0],
                cq[...], sq[...], cpv[...], spv[...], sink_ref[0])
        if not bwd:
            refs[10][0] = fn(*args).reshape(g, n, dh)
            return
        do_ref, dq_ref, dkp_ref, dkc_ref, dvp_ref, dvc_ref, dsink_ref = refs[10:17]
        _, vjp = jax.vjp(fn, *args)
        d = vjp(do_ref[0].reshape(g * n, dh))
        dq_ref[0] = d[0].reshape(g, n, dh)
        dkp_ref[0] = d[1]
        dkc_ref[0] = d[2]
        dvp_ref[0] = d[3]
        dvc_ref[0] = d[4]

        @pl.when(i == 0)
        def _():
            dsink_ref[...] = jnp.zeros_like(dsink_ref)
        dsink_ref[0] += d[9]

    qspec = pl.BlockSpec((1, g, n, dh), lambda h, i: (h, 0, i, 0))
    cur = pl.BlockSpec((1, n, dh), lambda h, i: (h, i, 0))
    prev = pl.BlockSpec((1, n, dh), lambda h, i: (h, jnp.maximum(i - 1, 0), 0))
    tcur = pl.BlockSpec((n, dh), lambda h, i: (i, 0))
    tprev = pl.BlockSpec((n, dh), lambda h, i: (jnp.maximum(i - 1, 0), 0))
    sspec = pl.BlockSpec((1, g * n, 1), lambda h, i: (h, 0, 0))
    in_specs = [qspec, prev, cur, prev, cur, tcur, tcur, tprev, tprev, sspec]
    ins = [q, k, k, v, v, cos, sin, cos, sin, sink]
    if not bwd:
        out_specs, out_shape = [qspec], [jax.ShapeDtypeStruct(q.shape, F32)]
    else:
        in_specs.append(qspec)
        ins.append(do)
        kv = jax.ShapeDtypeStruct(k.shape, F32)
        out_specs = [qspec, cur, cur, cur, cur, sspec]
        out_shape = [jax.ShapeDtypeStruct(q.shape, F32), kv, kv, kv, kv, jax.ShapeDtypeStruct(sink.shape, F32)]
    return pl.pallas_call(
        body, name="swa_core_bwd" if bwd else "swa_core_fwd", grid=(hkv, nb),
        in_specs=in_specs, out_specs=out_specs, out_shape=out_shape,
        compiler_params=_cp(("parallel", "arbitrary")),
    )(*ins)


def _coords():
    return lax.axis_index("x"), lax.axis_index("y"), lax.axis_index("c")


def _other_chips(x, y):
    return [(1 - x, y), (x, 1 - y), (1 - x, 1 - y)]


def _rcopy(src, dst, send, recv, dev):
    return pltpu.make_async_remote_copy(src_ref=src, dst_ref=dst, send_sem=send, recv_sem=recv,
                                        device_id=dev, device_id_type=MESH)


ANY = pl.BlockSpec(memory_space=pl.ANY)


def _allgather(name, shards, split):
    n = len(shards)

    def body(*refs):
        ins, outs = refs[:n], refs[n:2 * n]
        send, recv, loc = refs[2 * n:]
        x, y, c = _coords()
        s_me = 2 * x + y
        chips = _other_chips(x, y)
        started = []
        for i in range(n):
            lc = pltpu.make_async_copy(ins[i], outs[i].at[s_me], loc.at[i])
            lc.start()
            started.append(lc)
        if split:
            halves = [shards[i].shape[0] // 2 for i in range(n)]
            mine = [pl.ds(c * h, h) for h in halves]
            other = [pl.ds((1 - c) * h, h) for h in halves]
        sends = []
        for i in range(n):
            for j, (px, py) in enumerate(chips):
                if split:
                    cp = _rcopy(ins[i].at[mine[i]], outs[i].at[s_me, mine[i]], send.at[i, j], recv.at[i, j], (px, py, c))
                else:
                    cp = _rcopy(ins[i], outs[i].at[s_me], send.at[i, j], recv.at[i, j], (px, py, c))
                cp.start()
                sends.append(cp)
        for i in range(n):
            for j, (px, py) in enumerate(chips):
                s_j = 2 * px + py
                if split:
                    land = outs[i].at[s_j, mine[i]]
                    _rcopy(land, land, send.at[i, j], recv.at[i, j], (px, py, c)).wait_recv()
                    fw = _rcopy(land, land, send.at[i, 3 + j], recv.at[i, 3 + j], (x, y, 1 - c))
                    fw.start()
                    sends.append(fw)
                else:
                    land = outs[i].at[s_j]
                    _rcopy(land, land, send.at[i, j], recv.at[i, j], (px, py, c)).wait_recv()
        if split:
            for i in range(n):
                for j, (px, py) in enumerate(chips):
                    land = outs[i].at[2 * px + py, other[i]]
                    _rcopy(land, land, send.at[i, 3 + j], recv.at[i, 3 + j], (x, y, 1 - c)).wait_recv()
        for cp in sends:
            cp.wait_send()
        for lc in started:
            lc.wait()

    nsem = 6 if split else 3
    return pl.pallas_call(
        body, name=name,
        in_specs=[ANY] * n, out_specs=[ANY] * n,
        out_shape=[jax.ShapeDtypeStruct((N_CHIPS,) + s.shape, s.dtype) for s in shards],
        scratch_shapes=[pltpu.SemaphoreType.DMA((n, nsem)), pltpu.SemaphoreType.DMA((n, nsem)), pltpu.SemaphoreType.DMA((n,))],
        compiler_params=pltpu.CompilerParams(has_side_effects=True),
    )(*shards)


_HBM = pl.BlockSpec(memory_space=pltpu.HBM)
_SEM = pl.BlockSpec(memory_space=pltpu.SEMAPHORE)
_DATAFLOW = pltpu.SideEffectType.DATAFLOW_SIDE_EFFECTING


def _in_hbm(a):
    return pltpu.with_memory_space_constraint(a, pltpu.HBM)


def _ici_copy(gather, src, land, rows, c, s_from, s_to, send, recv, dev):
    if gather:
        mine = pl.ds(c * (rows // 2), rows // 2)
        return _rcopy(land.at[s_from, mine], land.at[s_from, mine], send, recv, dev)
    return _rcopy(src.at[s_to], land.at[s_from], send, recv, dev)


def _own_slot(name, src, lead, a, shape, where):
    _, r, c = src.shape
    tr = _row_tile(r, c, 4, 2)

    def body(x_ref, y_ref, s_ref, o_ref):
        o_ref[...] = s_ref[...].astype(o_ref.dtype)

    out = pl.pallas_call(
        body, name=name,
        grid_spec=pltpu.PrefetchScalarGridSpec(
            num_scalar_prefetch=2, grid=(a, r // tr),
            in_specs=[pl.BlockSpec((1, tr, c), lambda i, j, xr, yr: (lead + i, j, 0))],
            out_specs=pl.BlockSpec((1, tr, c), lambda i, j, xr, yr: ((2 * xr[0] + yr[0]) * a + i, j, 0))),
        out_shape=jax.ShapeDtypeStruct((N_CHIPS * a, r, c), BF16),
        compiler_params=_cp(("arbitrary", "arbitrary")),
    )(where[1], where[2], src)
    return out.reshape((N_CHIPS,) + tuple(shape))


def _ici_start(name, srcs, lands, after, gather):
    ns, n = len(srcs), len(lands)

    def body(*refs):
        s_refs, l_refs = refs[:ns], refs[ns:ns + n]
        send, recv = refs[ns + n + 1], refs[ns + n + 2]
        token = refs[-1]
        x, y, c = _coords()
        s_me = 2 * x + y
        for i in range(n):
            for j, (px, py) in enumerate(_other_chips(x, y)):
                _ici_copy(gather, None if gather else s_refs[i], l_refs[i], lands[i].shape[1], c, s_me, 2 * px + py,
                          send.at[3 * i + j], recv.at[3 * i + j], (px, py, c)).start()
        token[...] = jnp.zeros_like(token)

    thru = list(srcs) + list(lands)
    out_shape = [pltpu.SemaphoreType.DMA((3 * n,)), pltpu.SemaphoreType.DMA((3 * n,))]
    out_shape += [pltpu.HBM(a.shape, a.dtype) for a in thru]
    out_shape.append(jax.ShapeDtypeStruct((8, LANES), F32))
    res = pl.pallas_call(
        body, name=name, out_shape=out_shape,
        in_specs=[_HBM] * (ns + n) + [ANY],
        out_specs=[_SEM, _SEM] + [_HBM] * (ns + n) + [pl.BlockSpec(memory_space=pltpu.VMEM)],
        input_output_aliases={i: 2 + i for i in range(ns + n)},
        compiler_params=pltpu.CompilerParams(has_side_effects=_DATAFLOW),
    )(*[_in_hbm(a) for a in thru], after)
    return (res[0], res[1]), list(res[2:2 + ns]), list(res[2 + ns:2 + ns + n]), res[-1]


def _ici_wait(name, srcs, lands, send, recv, after, gather):
    ns, n = len(srcs), len(lands)

    def body(*refs):
        s_refs, l_refs = refs[:ns], refs[ns:ns + n]
        send_ref, recv_ref = refs[ns + n], refs[ns + n + 1]
        x, y, c = _coords()
        for i in range(n):
            for j, (px, py) in enumerate(_other_chips(x, y)):
                s_j = 2 * px + py
                cp = _ici_copy(gather, None if gather else s_refs[i], l_refs[i], lands[i].shape[1], c, s_j, s_j,
                               send_ref.at[3 * i + j], recv_ref.at[3 * i + j], (px, py, c))
                cp.wait_send()
                cp.wait_recv()

    thru = list(srcs) + list(lands)
    res = pl.pallas_call(
        body, name=name,
        out_shape=[pltpu.HBM(a.shape, a.dtype) for a in thru],
        in_specs=[_HBM] * (ns + n) + [_SEM, _SEM, ANY], out_specs=[_HBM] * (ns + n),
        input_output_aliases={i: i for i in range(ns + n)},
        compiler_params=pltpu.CompilerParams(has_side_effects=_DATAFLOW),
    )(*thru, send, recv, after)
    return list(res[:ns]), list(res[ns:])


def _ag_finish(name, lands):
    n = len(lands)

    def body(*refs):
        outs = refs[n:2 * n]
        send, recv = refs[2 * n:]
        x, y, c = _coords()
        chips = _other_chips(x, y)
        cps = []
        for i in range(n):
            half = lands[i].shape[1] // 2
            for j, (px, py) in enumerate(chips):
                land = outs[i].at[2 * px + py, pl.ds(c * half, half)]
                cp = _rcopy(land, land, send.at[i, j], recv.at[i, j], (x, y, 1 - c))
                cp.start()
                cps.append(cp)
        for i in range(n):
            half = lands[i].shape[1] // 2
            for j, (px, py) in enumerate(chips):
                land = outs[i].at[2 * px + py, pl.ds((1 - c) * half, half)]
                _rcopy(land, land, send.at[i, j], recv.at[i, j], (x, y, 1 - c)).wait_recv()
        for cp in cps:
            cp.wait_send()

    return pl.pallas_call(
        body, name=name, in_specs=[ANY] * n, out_specs=[ANY] * n,
        out_shape=[jax.ShapeDtypeStruct(a.shape, a.dtype) for a in lands],
        scratch_shapes=[pltpu.SemaphoreType.DMA((n, 3)), pltpu.SemaphoreType.DMA((n, 3))],
        input_output_aliases={i: i for i in range(n)},
        compiler_params=pltpu.CompilerParams(has_side_effects=True),
    )(*lands)


def _rs_pair_exchange(name, grads):
    n = len(grads)

    def body(*refs):
        ins, bufs = refs[:n], refs[n:2 * n]
        send, recv = refs[2 * n:]
        x, y, c = _coords()
        cps = []
        for i in range(n):
            half = grads[i].shape[1] // 2
            cp = _rcopy(ins[i].at[pl.ds(0, N_CHIPS), pl.ds((1 - c) * half, half)], bufs[i], send.at[i], recv.at[i], (x, y, 1 - c))
            cp.start()
            cps.append(cp)
        for cp in cps:
            cp.wait_recv()
        for cp in cps:
            cp.wait_send()

    return pl.pallas_call(
        body, name=name, in_specs=[ANY] * n, out_specs=[ANY] * n,
        out_shape=[jax.ShapeDtypeStruct((N_CHIPS, g.shape[1] // 2, g.shape[2]), g.dtype) for g in grads],
        scratch_shapes=[pltpu.SemaphoreType.DMA((n,)), pltpu.SemaphoreType.DMA((n,))],
        compiler_params=pltpu.CompilerParams(has_side_effects=True),
    )(*grads)


def _row_tile(rows, cols, itemsize, n_bufs):
    for tr in (2048, 1024, 512, 256, 128, 64, 32, 16, 8):
        if rows % tr == 0 and 2 * n_bufs * tr * cols * itemsize <= VMEM_BUDGET // 2:
            return tr
    return rows


def _rs_pair_sum(name, g, buf, core):
    _, R, C = g.shape
    half = R // 2
    tr = _row_tile(half, C, 4, 3)
    nhb = half // tr

    def body(c_ref, g_ref, b_ref, o_ref):
        o_ref[...] = (g_ref[...].astype(F32) + b_ref[...].astype(F32)).astype(o_ref.dtype)

    return pl.pallas_call(
        body, name=name,
        grid_spec=pltpu.PrefetchScalarGridSpec(
            num_scalar_prefetch=1, grid=(N_CHIPS, nhb),
            in_specs=[pl.BlockSpec((1, tr, C), lambda s, r, c_ref: (s, c_ref[0] * nhb + r, 0)),
                      pl.BlockSpec((1, tr, C), lambda s, r, c_ref: (s, r, 0))],
            out_specs=pl.BlockSpec((1, tr, C), lambda s, r, c_ref: (s, r, 0))),
        out_shape=jax.ShapeDtypeStruct((N_CHIPS, half, C), BF16),
        compiler_params=_cp(("arbitrary", "arbitrary")),
    )(core, g, buf)


def _rs_sum4(name, buf, part, where, out=None, row_base=0):
    _, half, C = buf.shape
    tr = _row_tile(half, C, 4, 4)
    nhb = half // tr
    base = row_base // tr

    def body(c_ref, x_ref, y_ref, b_ref, p_ref, *rest):
        o_ref = rest[-1]
        s_me = 2 * x_ref[0] + y_ref[0]
        acc = None
        for s in range(N_CHIPS):
            term = jnp.where(s_me == s, p_ref[0], b_ref[s]).astype(F32)
            acc = term if acc is None else acc + term
        o_ref[...] = acc

    in_specs = [pl.BlockSpec((N_CHIPS, tr, C), lambda r, cr, xr, yr: (0, r, 0)),
                pl.BlockSpec((1, tr, C), lambda r, cr, xr, yr: (2 * xr[0] + yr[0], r, 0))]
    args = [*where, buf, part]
    aliases = {}
    if out is not None:
        in_specs.append(ANY)
        args.append(out)
        aliases = {5: 0}
    return pl.pallas_call(
        body, name=name,
        grid_spec=pltpu.PrefetchScalarGridSpec(
            num_scalar_prefetch=3, grid=(nhb,), in_specs=in_specs,
            out_specs=pl.BlockSpec((tr, C), lambda r, cr, xr, yr: (base + cr[0] * nhb + r, 0))),
        out_shape=jax.ShapeDtypeStruct((2 * half, C) if out is None else out.shape, F32),
        input_output_aliases=aliases,
        compiler_params=_cp(("arbitrary",)),
    )(*args)


def _rs_share_halves(name, outs, spans):
    n = len(outs)

    def body(*refs):
        o = refs[n:2 * n]
        send, recv = refs[2 * n:]
        x, y, c = _coords()
        cps = []
        for i in range(n):
            base, half = spans[i][0], spans[i][1] // 2
            mine = o[i].at[pl.ds(base + c * half, half)]
            cp = _rcopy(mine, mine, send.at[i], recv.at[i], (x, y, 1 - c))
            cp.start()
            cps.append(cp)
        for i in range(n):
            base, half = spans[i][0], spans[i][1] // 2
            land = o[i].at[pl.ds(base + (1 - c) * half, half)]
            _rcopy(land, land, send.at[i], recv.at[i], (x, y, 1 - c)).wait_recv()
        for cp in cps:
            cp.wait_send()

    return pl.pallas_call(
        body, name=name, in_specs=[ANY] * n, out_specs=[ANY] * n,
        out_shape=[jax.ShapeDtypeStruct(a.shape, a.dtype) for a in outs],
        scratch_shapes=[pltpu.SemaphoreType.DMA((n,)), pltpu.SemaphoreType.DMA((n,))],
        input_output_aliases={i: i for i in range(n)},
        compiler_params=pltpu.CompilerParams(has_side_effects=True),
    )(*outs)


def _allreduce_small(name, v):
    rows = v.shape[0]

    def body(v_ref, o_ref, gath, send, recv):
        x, y, c = _coords()
        me = 4 * x + 2 * y + c

        def peer(kk):
            return (1 - x if kk & 4 else x, 1 - y if kk & 2 else y, 1 - c if kk & 1 else c)

        cps = []
        for kk in range(1, N_DEV):
            cp = _rcopy(v_ref, gath.at[me], send.at[kk - 1], recv.at[kk - 1], peer(kk))
            cp.start()
            cps.append(cp)
        gath[me] = v_ref[...]
        for kk in range(1, N_DEV):
            px, py, pc = peer(kk)
            land = gath.at[4 * px + 2 * py + pc]
            _rcopy(land, land, send.at[kk - 1], recv.at[kk - 1], (px, py, pc)).wait_recv()
        for cp in cps:
            cp.wait_send()
        acc = gath[0]
        for d in range(1, N_DEV):
            acc = acc + gath[d]
        o_ref[...] = acc

    return pl.pallas_call(
        body, name=name,
        in_specs=[pl.BlockSpec(memory_space=pltpu.VMEM)], out_specs=pl.BlockSpec(memory_space=pltpu.VMEM),
        out_shape=jax.ShapeDtypeStruct(v.shape, F32),
        scratch_shapes=[pltpu.VMEM((N_DEV, rows, LANES), F32), pltpu.SemaphoreType.DMA((N_DEV - 1,)), pltpu.SemaphoreType.DMA((N_DEV - 1,))],
        compiler_params=pltpu.CompilerParams(has_side_effects=True, vmem_limit_bytes=VMEM_LIMIT),
    )(v)


def _adamw(name, w, g, m, v):
    rows, cols = w.shape
    tr = _row_tile(rows, cols, 4, 7)

    def body(w_ref, g_ref, m_ref, v_ref, d_ref, nm_ref, nv_ref):
        gg = g_ref[...]
        nm = ADAM_B1 * m_ref[...] + (1.0 - ADAM_B1) * gg
        nv = ADAM_B2 * v_ref[...] + (1.0 - ADAM_B2) * (gg * gg)
        m_hat = nm / (1.0 - ADAM_B1 ** ADAM_STEP)
        v_hat = nv / (1.0 - ADAM_B2 ** ADAM_STEP)
        d_ref[...] = -ADAM_LR * (m_hat / (jnp.sqrt(v_hat) + ADAM_EPS) + ADAM_WD * w_ref[...])
        nm_ref[...] = nm
        nv_ref[...] = nv

    spec = pl.BlockSpec((tr, cols), lambda i: (i, 0))
    shp = jax.ShapeDtypeStruct((rows, cols), F32)
    return pl.pallas_call(
        body, name=name, grid=(rows // tr,), in_specs=[spec] * 4, out_specs=[spec] * 3, out_shape=[shp] * 3,
        compiler_params=_cp(("parallel",)),
    )(w, g, m, v)


def _as2d(a):
    if a.ndim == 1:
        return a.reshape(1, a.shape[0])
    return a.reshape(-1, a.shape[-1])


_WEIGHTS = ['meta_tokens', 'norm_w', 'ffn_w_gate', 'ffn_w_up', 'ffn_w_down', 'mlstm_w_in', 'mlstm_b_if', 'mlstm_norm_w',
            'mlstm_w_out', 'pool_w', 'pool_scale', 'gdn_w_in', 'gdn_conv_w', 'gdn_a_log', 'gdn_dt_bias', 'gdn_norm_w',
            'gdn_w_out', 'swa_w_qkv', 'swa_b_qkv', 'swa_sinks', 'swa_w_out', 'swa_b_out', 'final_norm_w']
_SMALL = [('meta_tokens', True), ('norm_w', True), ('pool_scale', True), ('gdn_conv_w', True), ('swa_b_qkv', True),
          ('swa_b_out', True), ('mlstm_b_if', False), ('mlstm_norm_w', False), ('gdn_a_log', False),
          ('gdn_dt_bias', False), ('gdn_norm_w', False), ('swa_sinks', False), ('final_norm_w', False)]


def _pack(vals):
    flat = jnp.concatenate([v.reshape(-1).astype(F32) for v in vals])
    n = _round_up(flat.shape[0], 8 * LANES)
    return jnp.pad(flat, (0, n - flat.shape[0])).reshape(n // LANES, LANES)


def _unpack(packed, shapes):
    flat = packed.reshape(-1)
    out, off = [], 0
    for s in shapes:
        n = int(np.prod(s))
        out.append(flat[off:off + n].reshape(s))
        off += n
    return out


def _to_chunks(a, n_meta, seq):
    pad = jnp.zeros((CHUNK - n_meta,) + a.shape[1:], a.dtype)
    return jnp.concatenate([a[:n_meta], pad, a[n_meta:n_meta + seq]], axis=0)


def _from_chunks(a, n_meta, seq, tp):
    pad = jnp.zeros((tp - n_meta - seq,) + a.shape[1:], a.dtype)
    return jnp.concatenate([a[:n_meta], a[CHUNK:CHUNK + seq], pad], axis=0)


def _col_row(g, heads):
    t = g.T
    return t[:, :, None], t.reshape(heads, -1, 1, CHUNK)


def _from_col_row(dc, dr):
    heads = dc.shape[0]
    return (dc[:, :, 0] + dr.reshape(heads, -1)).T


def kernel(x, meta_tokens, norm_w, ffn_w_gate, ffn_w_up, ffn_w_down, mlstm_w_in, mlstm_b_if, mlstm_norm_w, mlstm_w_out, pool_w, pool_scale, gdn_w_in, gdn_conv_w, gdn_a_log, gdn_dt_bias, gdn_norm_w, gdn_w_out, swa_w_qkv, swa_b_qkv, swa_sinks, swa_w_out, swa_b_out, final_norm_w, loss_target, m_meta_tokens, m_norm_w, m_ffn_w_gate, m_ffn_w_up, m_ffn_w_down, m_mlstm_w_in, m_mlstm_b_if, m_mlstm_norm_w, m_mlstm_w_out, m_pool_w, m_pool_scale, m_gdn_w_in, m_gdn_conv_w, m_gdn_a_log, m_gdn_dt_bias, m_gdn_norm_w, m_gdn_w_out, m_swa_w_qkv, m_swa_b_qkv, m_swa_sinks, m_swa_w_out, m_swa_b_out, m_final_norm_w, v_meta_tokens, v_norm_w, v_ffn_w_gate, v_ffn_w_up, v_ffn_w_down, v_mlstm_w_in, v_mlstm_b_if, v_mlstm_norm_w, v_mlstm_w_out, v_pool_w, v_pool_scale, v_gdn_w_in, v_gdn_conv_w, v_gdn_a_log, v_gdn_dt_bias, v_gdn_norm_w, v_gdn_w_out, v_swa_w_qkv, v_swa_b_qkv, v_swa_sinks, v_swa_w_out, v_swa_b_out, v_final_norm_w):
    args = locals()
    W = {n: args[n] for n in _WEIGHTS}
    M1 = {n: args["m_" + n] for n in _WEIGHTS}
    V2 = {n: args["v_" + n] for n in _WEIGHTS}

    SEQ, D = x.shape[1], x.shape[2]
    NM = meta_tokens.shape[0]
    T = NM + SEQ
    TP = _round_up(T, ROW_BLOCK)
    DEPTH = ffn_w_gate.shape[0]
    FFS = ffn_w_gate.shape[3]
    ML_H = mlstm_b_if.shape[1] // 2
    ML_DV = D // ML_H
    ML_DK = ML_DV // 2
    ML_IN = 2 * ML_H * ML_DK + 2 * D + 2 * ML_H
    ML_INP = _pad_cols(ML_IN)
    GD_DK = gdn_norm_w.shape[1]
    GD_VH = gdn_a_log.shape[1]
    GD_QH = GD_VH // 2
    GD_QKW = GD_QH * GD_DK
    GD_VW = GD_VH * GD_DK
    GD_CC = 2 * GD_QKW + GD_VW
    GD_IN = GD_CC + GD_VW + 2 * GD_VH
    GD_INP = _pad_cols(GD_IN)
    SW_HQ = swa_sinks.shape[1]
    SW_DH = D // SW_HQ
    SW_HKV = SW_HQ // SWA_GROUP
    SW_KVW = SW_HKV * SW_DH
    SW_IN = D + 2 * SW_KVW
    n_pool = len(POOL_WINDOWS)
    PG = D // n_pool

    cx, cy, cc = _coords()
    s_me = 2 * cx + cy
    core = cc.astype(jnp.int32).reshape(1)
    where = (core, cx.astype(jnp.int32).reshape(1), cy.astype(jnp.int32).reshape(1))

    def my_cols(full, width):
        return lax.dynamic_slice_in_dim(full, s_me * width, width, axis=full.ndim - 1)

    big_names = ['ffn_w_gate', 'ffn_w_up', 'ffn_w_down', 'mlstm_w_in', 'mlstm_w_out', 'pool_w', 'gdn_w_in', 'gdn_w_out',
                 'swa_w_qkv', 'swa_w_out']
    as3 = lambda w: w.reshape((-1,) + w.shape[-2:])
    mixer_shards = [[mlstm_w_in, mlstm_w_out], [pool_w.reshape(1, n_pool * (PG // N_CHIPS), PG)],
                    [gdn_w_in, gdn_w_out], [swa_w_qkv, swa_w_out]]
    ffn3 = [as3(ffn_w_gate), as3(ffn_w_up), as3(ffn_w_down)]
    ag_groups = [[(w, 2 * li, 2, (2,) + w.shape[1:]) for w in ffn3] + [(m, 0, 1, m.shape[1:]) for m in mixer_shards[li % 4]]
                 for li in range(DEPTH)]
    ag_state = {}
    WF = {}

    def cols_full(g, pad_to=None):
        k = g.shape[1]
        full = jnp.transpose(g, (1, 0, 2)).reshape(k, -1)
        if pad_to is not None and pad_to > full.shape[1]:
            full = jnp.pad(full, ((0, 0), (0, pad_to - full.shape[1])))
        return full

    def rows_full(g):
        return g.reshape(-1, g.shape[2])

    def start_gather(li, after):
        lands = [_own_slot(f"ag_own_{li}_{i}", w, lead, cnt, shp, where) for i, (w, lead, cnt, shp) in enumerate(ag_groups[li])]
        sems, _, lands, token = _ici_start(f"ag_start_{li}", [], lands, after, True)
        ag_state[li] = (sems, lands)
        return token

    def gather_layer(li, after):
        sems, lands = ag_state.pop(li)
        _, lands = _ici_wait(f"ag_wait_{li}", [], lands, *sems, after, True)
        full = _ag_finish(f"ag_finish_{li}", lands)
        WF['wg', li], WF['wu', li], WF['wd', li] = full[0], full[1], full[2]
        m = full[3:]
        if li % 4 == 0:
            WF['ml_win'], WF['ml_wout'] = cols_full(m[0], ML_INP), rows_full(m[1])
        elif li % 4 == 1:
            WF['pool'] = jnp.transpose(m[0].reshape(N_CHIPS, n_pool, PG // N_CHIPS, PG), (1, 0, 2, 3)).reshape(n_pool, PG, PG)
        elif li % 4 == 2:
            WF['gd_win'], WF['gd_wout'] = cols_full(m[0], GD_INP), rows_full(m[1])
        else:
            WF['sw_wqkv'], WF['sw_wout'] = cols_full(m[0]), rows_full(m[1])

    small_sharded = [n for n, sh in _SMALL if sh]
    sm_shapes = [W[n].shape for n in small_sharded]
    sm_gath = _allgather("allgather_small", [_pack([W[n] for n in small_sharded])], False)[0]
    sm_parts = [_unpack(sm_gath[s], sm_shapes) for s in range(N_CHIPS)]
    SF = {n: jnp.concatenate([sm_parts[s][i] for s in range(N_CHIPS)], axis=-1) for i, n in enumerate(small_sharded)}
    meta_full, normw_full = SF['meta_tokens'], SF['norm_w']
    pool_scale_full, conv_full = SF['pool_scale'], SF['gdn_conv_w'][0]
    bqkv_full, bout_full = SF['swa_b_qkv'], SF['swa_b_out']

    nps = 1
    ff_tm = _tiles(TP, FFS, D, 2 * 4 + 3 * 2 * 2, n_pairs=2)[0]

    def ffn_fwd(h, li, wi, nw):
        n = _rmsnorm_fwd(f"ffn_norm_{li}_{wi}", h, nw, BF16)
        tm, tn, tk = ff_tm, FFS, _div(D, (512, 256, 128))
        bspec = lambda j, k: (j, wi, k, 0)
        g, u, a = _matmul(
            f"ffn_gateup_{li}_{wi}", "nn", (TP // tm, N_CHIPS, D // tk),
            [(n, (tm, tk), lambda i, j, k: (i, k))] * 2,
            [(WF['wg', li], (None, None, tk, tn), lambda i, j, k: bspec(j, k)),
             (WF['wu', li], (None, None, tk, tn), lambda i, j, k: bspec(j, k))],
            [0, 1], 2, [],
            [(jax.ShapeDtypeStruct((TP, N_CHIPS * FFS), BF16), (tm, tn), lambda i, j, k: (i, j))] * 3,
            lambda accs, ex: [accs[0], accs[1], _silu(accs[0]) * accs[1]], tm, tn)
        tm2, tn2, tk2 = _tiles(TP, D, FFS, 4 + 2 * 4 + 2 * 4, k_cands=(FFS,), n_cands=(1024, 512, 256, 128))
        kps = FFS // tk2
        h2 = _matmul(
            f"ffn_down_{li}_{wi}", "nn", (TP // tm2, D // tn2, N_CHIPS * kps),
            [(a, (tm2, tk2), lambda i, j, k: (i, k))],
            [(WF['wd', li], (None, None, tk2, tn2), lambda i, j, k: (k // kps, wi, k % kps, j))],
            [0], 1, [(h, (tm2, tn2), lambda i, j, k: (i, j))],
            [(jax.ShapeDtypeStruct((TP, D), F32), (tm2, tn2), lambda i, j, k: (i, j))],
            lambda accs, ex: [ex[0] + 0.5 * accs[0]], tm2, tn2)[0]
        return h2, (h, n, g, u, a)

    def ffn_bwd(dh2, saved, li, wi, nw, gbufs):
        h, n, g, u, a = saved
        gg, gu, gd = gbufs
        slot = wi
        tm, tn = ff_tm, FFS
        tk = _div(D, (1024, 512, 256, 128))

        def epi(accs, ex):
            gb, ub = ex[0].astype(F32), ex[1].astype(F32)
            da = 0.5 * accs[0]
            s = _sigmoid(gb)
            return [da * ub * (s * (1.0 + gb * (1.0 - s))), da * (gb * s)]
        dg, du = _matmul(
            f"ffn_dact_{li}_{wi}", "nt", (TP // tm, N_CHIPS, D // tk),
            [(dh2, (tm, tk), lambda i, j, k: (i, k))],
            [(WF['wd', li], (None, None, tn, tk), lambda i, j, k: (j, wi, 0, k))],
            [0], 1, [(g, (tm, tn), lambda i, j, k: (i, j)), (u, (tm, tn), lambda i, j, k: (i, j))],
            [(jax.ShapeDtypeStruct((TP, N_CHIPS * FFS), BF16), (tm, tn), lambda i, j, k: (i, j))] * 2, epi, tm, tn)
        tkr = _div(TP, (1408, 1056, 704, 384, 256, 128))
        tnd = _div(D, (1024, 512, 256, 128))
        gd = _matmul(
            f"ffn_dwd_{li}_{wi}", "tn", (N_CHIPS, D // tnd, TP // tkr),
            [(a, (tkr, FFS), lambda i, j, k: (k, i))], [(dh2, (tkr, tnd), lambda i, j, k: (k, j))],
            [0], 1, [], [(jax.ShapeDtypeStruct(gd.shape, BF16), (None, None, FFS, tnd), lambda i, j, k: (i, slot, 0, j))],
            lambda accs, ex: [0.5 * accs[0]], FFS, tnd, alias_inputs=[gd], alias_map={0: 0})[0]
        tmw = _div(D, (512, 256, 128))
        gg, gu = _matmul(
            f"ffn_dwgu_{li}_{wi}", "tn", (D // tmw, N_CHIPS, TP // tkr),
            [(n, (tkr, tmw), lambda i, j, k: (k, i))] * 2,
            [(dg, (tkr, FFS), lambda i, j, k: (k, j)), (du, (tkr, FFS), lambda i, j, k: (k, j))],
            [0, 1], 2, [],
            [(jax.ShapeDtypeStruct(gg.shape, BF16), (None, None, tmw, FFS), lambda i, j, k: (j, slot, i, 0))] * 2,
            lambda accs, ex: [accs[0], accs[1]], tmw, FFS, alias_inputs=[gg, gu], alias_map={0: 0, 1: 1})
        tm3 = _div(TP, (704, 528, 384, 256, 128))
        tn3 = _div(D, (1024, 512, 256, 128))
        dn = _matmul(
            f"ffn_dn_{li}_{wi}", "nt", (TP // tm3, D // tn3, N_CHIPS),
            [(dg, (tm3, FFS), lambda i, j, k: (i, k)), (du, (tm3, FFS), lambda i, j, k: (i, k))],
            [(WF['wg', li], (None, None, tn3, FFS), lambda i, j, k: (k, wi, j, 0)),
             (WF['wu', li], (None, None, tn3, FFS), lambda i, j, k: (k, wi, j, 0))],
            [0, 0], 1, [], [(jax.ShapeDtypeStruct((TP, D), F32), (tm3, tn3), lambda i, j, k: (i, j))],
            lambda accs, ex: [accs[0]], tm3, tn3)[0]
        dh, dnw = _rmsnorm_bwd(f"ffn_norm_bwd_{li}_{wi}", h, nw, dn, dh2)
        return dh, dnw, (gg, gu, gd)

    def mlstm_fwd(h, nw):
        u = _rmsnorm_fwd("mlstm_norm", h, nw, BF16)
        p = _mm("mlstm_in", "nn", u, WF['ml_win'], F32)
        pc = _to_chunks(p, NM, SEQ)
        qkw = ML_H * ML_DK
        gates = pc[:, 2 * qkw + 2 * D:2 * qkw + 2 * D + 2 * ML_H] + mlstm_b_if
        li_c, li_r = _col_row(gates[:, :ML_H], ML_H)
        lf_c, lf_r = _col_row(gates[:, ML_H:], ML_H)
        hh, cs, ns, ms = _mlstm_core_fwd(pc, li_c, lf_c, li_r, lf_r, ML_H, ML_DK, ML_DV, NM)
        hh_s = _from_chunks(hh, NM, SEQ, TP)
        og_cb = (2 * qkw + D) // D
        out = _headnorm_fwd("mlstm_post", hh_s, p, D, og_cb, mlstm_norm_w, ML_DV, _sigmoid)
        h2 = _mm("mlstm_out", "nn", out, WF['ml_wout'], F32, lambda acc, hb: hb + acc, [h], ["tile"])
        return h2, (h, u, p, pc, (li_c, lf_c, li_r, lf_r), (cs, ns, ms), hh_s, out, og_cb)

    def mlstm_bwd(dh2, saved, nw):
        h, u, p, pc, gts, sts, hh_s, out, og_cb = saved
        dout = _mm("mlstm_out_dx", "nt", dh2, WF['ml_wout'], F32)
        d_wout = _mm("mlstm_out_dw", "tn", out, dh2, BF16)
        dhh, dog, dnormw = _headnorm_bwd("mlstm_post_bwd", hh_s, p, D, og_cb, mlstm_norm_w, dout, ML_DV, _sigmoid)
        dq, dkk, dvv, d0, d1, d2, d3 = _mlstm_core_bwd(pc, *gts, *sts, _to_chunks(dhh, NM, SEQ), ML_H, ML_DK, ML_DV, NM)
        dgates = jnp.concatenate([_from_col_row(d0, d2), _from_col_row(d1, d3)], axis=1)
        dqkvg = _from_chunks(jnp.concatenate([dq, dkk, dvv], axis=1), NM, SEQ, TP)
        dgs = _from_chunks(dgates, NM, SEQ, TP)
        pad = jnp.zeros((TP, ML_INP - ML_IN), F32)
        dp = jnp.concatenate([dqkvg, dog, dgs, pad], axis=1)
        d_bif = _colsum("mlstm_dbias", jnp.pad(dgs, ((0, 0), (0, LANES - 2 * ML_H))))[:, :2 * ML_H]
        d_win = _mm("mlstm_in_dw", "tn", u, dp, BF16)[:, :ML_IN]
        du = _mm("mlstm_in_dx", "nt", dp, WF['ml_win'], F32)
        dh, dnw = _rmsnorm_bwd("mlstm_norm_bwd", h, nw, du, dh2)
        return dh, dnw, {'mlstm_w_in': d_win, 'mlstm_w_out': d_wout, 'mlstm_b_if': d_bif, 'mlstm_norm_w': dnormw}

    def pool_fwd_layer(h, nw):
        u = _rmsnorm_fwd("pool_norm", h, nw, F32)
        pooled = _pool_fwd(u)
        tm = _div(TP, (704, 528, 384, 256, 128))
        tk = _div(PG, (512, 256, 128))
        kpg = PG // tk
        h2, ypre = _matmul(
            "pool_mix", "nn", (TP // tm, n_pool, kpg),
            [(pooled, (tm, tk), lambda i, j, k: (i, j * kpg + k))],
            [(WF['pool'],(None, tk, PG), lambda i, j, k: (j, k, 0))],
            [0], 1, [(h, (tm, PG), lambda i, j, k: (i, j)), (pool_scale_full, (1, PG), lambda i, j, k: (0, j))],
            [(jax.ShapeDtypeStruct((TP, D), F32), (tm, PG), lambda i, j, k: (i, j))] * 2,
            lambda accs, ex: [ex[0] + accs[0] * ex[1], accs[0]], tm, PG)
        return h2, (h, pooled, ypre)

    def pool_bwd_layer(dh2, saved, nw):
        h, pooled, ypre = saved

        def fn(_, dyb, ypb, sb):
            return [dyb * sb, jnp.sum(dyb * ypb, axis=0, keepdims=True)]
        dys, dscale = _rowwise("pool_scale_bwd", fn, [_full(dh2), _full(ypre), _full(pool_scale_full)],
                               [(D, BF16, "row"), (D, F32, "acc")], ROW_BLOCK, TP)
        tm = _div(TP, (704, 528, 384, 256, 128))
        tk = _div(PG, (512, 256, 128))
        kpg = PG // tk
        dpooled = _matmul(
            "pool_mix_dx", "nt", (TP // tm, n_pool, kpg),
            [(dys, (tm, tk), lambda i, j, k: (i, j * kpg + k))],
            [(WF['pool'],(None, PG, tk), lambda i, j, k: (j, 0, k))],
            [0], 1, [], [(jax.ShapeDtypeStruct((TP, D), F32), (tm, PG), lambda i, j, k: (i, j))],
            lambda accs, ex: [accs[0]], tm, PG)[0]
        tkr = _div(TP, (1408, 1056, 704, 384, 256, 128))
        d_pw = _matmul(
            "pool_mix_dw", "tn", (1, n_pool, TP // tkr),
            [(pooled, (tkr, PG), lambda i, j, k: (k, j))], [(dys, (tkr, PG), lambda i, j, k: (k, j))],
            [0], 1, [], [(jax.ShapeDtypeStruct((n_pool, PG, PG), BF16), (None, PG, PG), lambda i, j, k: (j, 0, 0))],
            lambda accs, ex: [accs[0]], PG, PG)[0]
        du = _pool_bwd(dpooled)
        dh, dnw = _rmsnorm_bwd("pool_norm_bwd", h, nw, du, dh2)
        return dh, dnw, {'pool_w': d_pw, 'pool_scale': dscale}

    def gdn_fwd(h, nw):
        u = _rmsnorm_fwd("gdn_norm", h, nw, BF16)
        p = _mm("gdn_in", "nn", u, WF['gd_win'], F32)
        qkv_act = _conv_fwd(p, conv_full, GD_CC)
        b_pre = p[:, GD_CC + GD_VW:GD_CC + GD_VW + GD_VH]
        a_pre = p[:, GD_CC + GD_VW + GD_VH:GD_IN]
        g, beta = _gdn_gates_fwd(a_pre, b_pre, gdn_a_log, gdn_dt_bias)
        qkv_c = _to_chunks(qkv_act, NM, SEQ)
        g_c, g_r = _col_row(_to_chunks(g, NM, SEQ), GD_VH)
        b_c, _ = _col_row(_to_chunks(beta, NM, SEQ), GD_VH)
        o, st = _gdn_core_fwd(qkv_c, g_c, b_c, g_r, GD_QH, GD_DK, NM)
        o_s = _from_chunks(o, NM, SEQ, TP)
        nw_t = jnp.tile(gdn_norm_w, (1, GD_VH))
        z_cb = GD_CC // GD_VW
        out = _headnorm_fwd("gdn_post", o_s, p, GD_VW, z_cb, nw_t, GD_DK, _silu)
        h2 = _mm("gdn_out", "nn", out, WF['gd_wout'], F32, lambda acc, hb: hb + acc, [h], ["tile"])
        return h2, (h, u, p, qkv_c, (g_c, b_c, g_r), st, o_s, out, nw_t, z_cb, a_pre, b_pre)

    def gdn_bwd(dh2, saved, nw):
        h, u, p, qkv_c, gts, st, o_s, out, nw_t, z_cb, a_pre, b_pre = saved
        dout = _mm("gdn_out_dx", "nt", dh2, WF['gd_wout'], F32)
        d_wout = _mm("gdn_out_dw", "tn", out, dh2, BF16)
        do, dz, dnw_t = _headnorm_bwd("gdn_post_bwd", o_s, p, GD_VW, z_cb, nw_t, dout, GD_DK, _silu)
        dnormw = jnp.sum(dnw_t.reshape(GD_VH, GD_DK), axis=0, keepdims=True)
        res = _gdn_core_bwd(qkv_c, *gts, st, _to_chunks(do, NM, SEQ), GD_QH, GD_DK, NM)
        dq, dkk, dvv = res[0], res[1], res[2]
        dgc, dbc, dgr = res[3], res[4], res[5]
        dg = _from_chunks(_from_col_row(dgc, dgr), NM, SEQ, TP)
        dbeta = _from_chunks(dbc[:, :, 0].T, NM, SEQ, TP)
        da_pre, db_pre, d_alog, d_dt = _gdn_gates_bwd(a_pre, b_pre, gdn_a_log, gdn_dt_bias, dg, dbeta)
        dact = _from_chunks(jnp.concatenate([dq, dkk, dvv], axis=1), NM, SEQ, TP)
        dacc, d_conv = _conv_bwd_pre(p, conv_full, dact, GD_CC)
        dqkv_pre = _conv_bwd_dx(dacc, conv_full, GD_CC)
        pad = jnp.zeros((TP, GD_INP - GD_IN), F32)
        dp = jnp.concatenate([dqkv_pre, dz, db_pre, da_pre, pad], axis=1)
        d_win = _mm("gdn_in_dw", "tn", u, dp, BF16)[:, :GD_IN]
        du = _mm("gdn_in_dx", "nt", dp, WF['gd_win'], F32)
        dh, dnw = _rmsnorm_bwd("gdn_norm_bwd", h, nw, du, dh2)
        return dh, dnw, {'gdn_w_in': d_win, 'gdn_w_out': d_wout, 'gdn_conv_w': d_conv, 'gdn_a_log': d_alog,
                         'gdn_dt_bias': d_dt, 'gdn_norm_w': dnormw}

    inv = ROPE_THETA ** (-jnp.arange(0, SW_DH, 2, dtype=F32) / SW_DH)
    ang = jnp.arange(TP, dtype=F32)[:, None] * inv[None, :]
    ang = jnp.concatenate([ang, ang], axis=-1)
    rope_cos, rope_sin = jnp.cos(ang), jnp.sin(ang)

    def swa_split(p):
        q = jnp.transpose(p[:, :D].reshape(TP, SW_HKV, SWA_GROUP, SW_DH), (1, 2, 0, 3))
        k = jnp.transpose(p[:, D:D + SW_KVW].reshape(TP, SW_HKV, SW_DH), (1, 0, 2))
        v = jnp.transpose(p[:, D + SW_KVW:].reshape(TP, SW_HKV, SW_DH), (1, 0, 2))
        return q, k, v

    def swa_fwd(h, nw):
        u = _rmsnorm_fwd("swa_norm", h, nw, BF16)
        p = _mm("swa_in", "nn", u, WF['sw_wqkv'], F32, lambda acc, bb: acc + bb, [bqkv_full], ["row"])
        q, k, v = swa_split(p)
        sink = jnp.repeat(swa_sinks.reshape(SW_HKV, SWA_GROUP), ROW_BLOCK, axis=1)[:, :, None]
        o = _swa_core(q, k, v, rope_cos, rope_sin, sink, T)[0]
        o2 = jnp.transpose(o, (2, 0, 1, 3)).reshape(TP, D).astype(BF16)
        h2 = _mm("swa_out", "nn", o2, WF['sw_wout'], F32, lambda acc, hb, bb: hb + acc + bb, [h, bout_full], ["tile", "row"])
        return h2, (h, u, q, k, v, sink, o2)

    def swa_bwd(dh2, saved, nw):
        h, u, q, k, v, sink, o2 = saved
        do = _mm("swa_out_dx", "nt", dh2, WF['sw_wout'], F32)
        d_wout = _mm("swa_out_dw", "tn", o2, dh2, BF16)
        d_bout = _colsum("swa_dbout", dh2)
        do4 = jnp.transpose(do.reshape(TP, SW_HKV, SWA_GROUP, SW_DH), (1, 2, 0, 3))
        dq, dkp, dkc, dvp, dvc, dsink = _swa_core(q, k, v, rope_cos, rope_sin, sink, T, do=do4)
        shift = lambda a: jnp.concatenate([a[:, ROW_BLOCK:], jnp.zeros_like(a[:, :ROW_BLOCK])], axis=1)
        dk = dkc + shift(dkp)
        dv = dvc + shift(dvp)
        dp = jnp.concatenate([jnp.transpose(dq, (2, 0, 1, 3)).reshape(TP, D),
                              jnp.transpose(dk, (1, 0, 2)).reshape(TP, SW_KVW),
                              jnp.transpose(dv, (1, 0, 2)).reshape(TP, SW_KVW)], axis=1)
        d_sinks = jnp.sum(dsink.reshape(SW_HKV, SWA_GROUP, ROW_BLOCK), axis=2).reshape(1, SW_HQ)
        d_bqkv = _colsum("swa_dbqkv", dp)
        d_wqkv = _mm("swa_in_dw", "tn", u, dp, BF16)
        du = _mm("swa_in_dx", "nt", dp, WF['sw_wqkv'], F32)
        dh, dnw = _rmsnorm_bwd("swa_norm_bwd", h, nw, du, dh2)
        return dh, dnw, {'swa_w_qkv': d_wqkv, 'swa_w_out': d_wout, 'swa_b_qkv': d_bqkv, 'swa_b_out': d_bout,
                         'swa_sinks': d_sinks}

    mixers_fwd = [mlstm_fwd, pool_fwd_layer, gdn_fwd, swa_fwd]
    mixers_bwd = [mlstm_bwd, pool_bwd_layer, gdn_bwd, swa_bwd]

    h = jnp.concatenate([meta_full, x[0], jnp.zeros((TP - T, D), F32)], axis=0)
    saved = []
    token = start_gather(0, sm_gath)
    for li in range(DEPTH):
        nws = [normw_full[li, t].reshape(1, D) for t in range(3)]
        gather_layer(li, token if li == 0 else h)
        if li + 1 < DEPTH:
            nws[0] = nws[0] + start_gather(li + 1, WF['wg', li])[:1, :1]
        h, s0 = ffn_fwd(h, li, 0, nws[0])
        h, s1 = mixers_fwd[li % 4](h, nws[1])
        h, s2 = ffn_fwd(h, li, 1, nws[2])
        saved.append((s0, s1, s2, nws))

    tgt = jnp.concatenate([jnp.zeros((NM, D), F32), loss_target[0], jnp.zeros((TP - T, D), F32)], axis=0)
    dh, d_final_w, loss_vec = _loss_head("loss_head", h, final_norm_w.reshape(1, D), tgt, NM, SEQ)

    def col_shards(g, w):
        return jnp.transpose(g.reshape(g.shape[0], N_CHIPS, w), (1, 0, 2))

    def row_shards(g):
        return g.reshape(N_CHIPS, -1, g.shape[1])

    def mixer_grads(li, gm):
        if li % 4 == 0:
            return ['mlstm_w_in', 'mlstm_w_out'], [col_shards(gm['mlstm_w_in'], ML_IN // N_CHIPS), row_shards(gm['mlstm_w_out'])]
        if li % 4 == 1:
            pw = jnp.transpose(gm['pool_w'].reshape(n_pool, N_CHIPS, PG // N_CHIPS, PG), (1, 0, 2, 3))
            return ['pool_w'], [pw.reshape(N_CHIPS, -1, PG)]
        if li % 4 == 2:
            return ['gdn_w_in', 'gdn_w_out'], [col_shards(gm['gdn_w_in'], GD_IN // N_CHIPS), row_shards(gm['gdn_w_out'])]
        return ['swa_w_qkv', 'swa_w_out'], [col_shards(gm['swa_w_qkv'], SW_IN // N_CHIPS), row_shards(gm['swa_w_out'])]

    ffn_rows = (2 * D, 2 * D, 2 * FFS)
    ffn_out = [lax.empty((DEPTH * r, c_), F32) for r, c_ in zip(ffn_rows, (FFS, FFS, D))]
    grads = {}

    def rs_begin(li, names, glist):
        bufs = _rs_pair_exchange(f"rs_pair_{li}", glist)
        parts = [_rs_pair_sum(f"rs_pairsum_{li}_{i}", g, b, core) for i, (g, b) in enumerate(zip(glist, bufs))]
        sems, parts, lands, token = _ici_start(f"rs_start_{li}", parts, [lax.empty(p.shape, p.dtype) for p in parts], core, False)
        return (li, names, sems, parts, lands), token

    def rs_finish(state, after):
        li, names, sems, parts, lands = state
        parts, lands = _ici_wait(f"rs_wait_{li}", parts, lands, *sems, after, False)
        outs, spans = [], []
        for i, (p, b) in enumerate(zip(parts, lands)):
            if i < 3:
                outs.append(_rs_sum4(f"rs_sum4_{li}_{i}", b, p, where, out=ffn_out[i], row_base=li * ffn_rows[i]))
                spans.append((li * ffn_rows[i], ffn_rows[i]))
            else:
                outs.append(_rs_sum4(f"rs_sum4_{li}_{i}", b, p, where))
                spans.append((0, outs[-1].shape[0]))
        outs = _rs_share_halves(f"rs_share_{li}", outs, spans)
        ffn_out[:] = outs[:3]
        for n, o in zip(names, outs[3:]):
            grads[n] = o.reshape(W[n].shape)

    d_normw = [[None] * 3 for _ in range(DEPTH)]
    GR = {}
    pending = None
    for li in reversed(range(DEPTH)):
        s0, s1, s2, nws = saved[li]
        gbufs = (lax.empty((N_CHIPS, 2, D, FFS), BF16), lax.empty((N_CHIPS, 2, D, FFS), BF16),
                 lax.empty((N_CHIPS, 2, FFS, D), BF16))
        dh, d_normw[li][2], gbufs = ffn_bwd(dh, s2, li, 1, nws[2], gbufs)
        dh, d_normw[li][1], gm = mixers_bwd[li % 4](dh, s1, nws[1])
        GR.update(gm)
        dh, d_normw[li][0], gbufs = ffn_bwd(dh, s0, li, 0, nws[0], gbufs)
        if pending is not None:
            rs_finish(pending, dh)
        names, mg = mixer_grads(li, gm)
        glist = [gbufs[0].reshape(N_CHIPS, 2 * D, FFS), gbufs[1].reshape(N_CHIPS, 2 * D, FFS),
                 gbufs[2].reshape(N_CHIPS, 2 * FFS, D)] + mg
        pending, token = rs_begin(li, names, glist)
        dh = dh + token[:1, :1]
    rs_finish(pending, dh)
    for n, o in zip(['ffn_w_gate', 'ffn_w_up', 'ffn_w_down'], ffn_out):
        grads[n] = o.reshape(W[n].shape)
    grad_x = dh[NM:NM + SEQ][None]
    GR['meta_tokens'] = dh[:NM]
    GR['norm_w'] = jnp.stack([jnp.concatenate(r, axis=0) for r in d_normw], axis=0)
    GR['final_norm_w'] = d_final_w

    small_names = [n for n, _ in _SMALL]
    small_full_shapes = [GR[n].shape for n in small_names]
    packed = _pack([GR[n] for n in small_names] + [loss_vec[:, :1]])
    red = _allreduce_small("allreduce_small", packed)
    parts = _unpack(red, small_full_shapes + [(1, 1)])
    loss = parts[-1].reshape(())
    for (n, sharded), full in zip(_SMALL, parts[:-1]):
        full = full.reshape(W[n].shape[:-1] + (-1,))
        grads[n] = my_cols(full, W[n].shape[-1]) if sharded else full

    delta, new_m, new_v = {}, {}, {}
    for n in _WEIGHTS:
        shp = W[n].shape
        d, nm, nv = _adamw(f"adamw_{n}", _as2d(W[n]), _as2d(grads[n]), _as2d(M1[n]), _as2d(V2[n]))
        delta[n], new_m[n], new_v[n] = d.reshape(shp), nm.reshape(shp), nv.reshape(shp)

    return (loss, grad_x, *[grads[n] for n in _WEIGHTS], *[delta[n] for n in _WEIGHTS],
            *[new_m[n] for n in _WEIGHTS], *[new_v[n] for n in _WEIGHTS])
```
